```python
import jax, jax.numpy as jnp
from jax import lax
import numpy as np

D_MODEL = 1024
BATCH = 8
SEQ = 8192
DEPTH = 2

MEM_LEN = 256
HEAD_DIM = 64
ROPE_THETA = 10000.0
NORM_EPS = 1e-6
BLOCK = 128

SWA_HEADS = 8
SWA_KV_HEADS = 2
SWA_WINDOW = 128
MLA_HEADS = 8
MLA_Q_RANK = 384
MLA_KV_RANK = 256
MLA_NOPE_DIM = 64
MLA_ROPE_DIM = 32
MLA_V_DIM = 64
A_Q = SWA_HEADS * HEAD_DIM
A_KV = SWA_KV_HEADS * HEAD_DIM
EVEN_IN = A_Q + 2 * A_KV + MLA_Q_RANK + MLA_KV_RANK + MLA_ROPE_DIM
EVEN_SPLITS = [A_Q, A_Q + A_KV, A_Q + 2 * A_KV, A_Q + 2 * A_KV + MLA_Q_RANK,
               A_Q + 2 * A_KV + MLA_Q_RANK + MLA_KV_RANK]
EVEN_OUT = SWA_HEADS * HEAD_DIM + MLA_HEADS * MLA_V_DIM
DIL_HEADS = D_MODEL // HEAD_DIM
DIL_PATTERNS = ((128, 1), (512, 4), (2048, 16))
X_HEADS = 4
X_HEAD_DIM = 128
FFN_HIDDEN = -(-8 * D_MODEL // (3 * 256)) * 256

kernel_name = 'hybrid_swa_mla_dilated_block'


def rms_norm(x, g):
    xf = x.astype(jnp.float32)
    y = xf * lax.rsqrt(jnp.mean(xf * xf, axis=-1, keepdims=True) + NORM_EPS)
    return (y * g.astype(jnp.float32)).astype(x.dtype)


def rope(x, positions):
    dh = x.shape[-1]
    inv_freq = ROPE_THETA ** (-jnp.arange(0, dh, 2, dtype=jnp.float32) / dh)
    ang = positions.astype(jnp.float32)[..., None] * inv_freq
    c = jnp.cos(ang)[:, :, None, :]
    s = jnp.sin(ang)[:, :, None, :]
    x1, x2 = jnp.split(x.astype(jnp.float32), 2, axis=-1)
    return jnp.concatenate([x1 * c - x2 * s, x2 * c + x1 * s], axis=-1).astype(x.dtype)


def banded_attention(q, k, v, max_dist, sink=None):
    b, L, h, dh = q.shape
    g = k.shape[2]
    rep = h // g
    nb = L // BLOCK
    qb = q.reshape(b, nb, BLOCK, g, rep, dh)

    def two_blocks(t):
        tb = t.reshape(b, nb, BLOCK, g, t.shape[-1])
        prev = jnp.pad(tb, ((0, 0), (1, 0), (0, 0), (0, 0), (0, 0)))[:, :-1]
        return jnp.concatenate([prev, tb], axis=2)

    kk = two_blocks(k)
    vv = two_blocks(v)
    s = jnp.einsum('bnqgrd,bnkgd->bngrqk', qb, kk).astype(jnp.float32) * (dh ** -0.5)
    qi = jnp.arange(BLOCK)[:, None]
    kj = jnp.arange(2 * BLOCK)[None, :]
    dist = BLOCK + qi - kj
    band = (dist >= 0) & (dist <= max_dist)
    exists = (jnp.arange(nb)[:, None, None] > 0) | (kj >= BLOCK)[None]
    mask = band[None] & exists
    s = jnp.where(mask[None, :, None, None], s, -jnp.inf)
    m = jnp.max(s, axis=-1, keepdims=True)
    if sink is not None:
        sk = sink.astype(jnp.float32).reshape(g, rep)[None, None, :, :, None, None]
        m = jnp.maximum(m, sk)
    p = jnp.exp(s - m)
    l = jnp.sum(p, axis=-1, keepdims=True)
    if sink is not None:
        l = l + jnp.exp(sk - m)
    o = jnp.einsum('bngrqk,bnkgd->bnqgrd', (p / l).astype(v.dtype), vv)
    lse = (m + jnp.log(l))[..., 0].transpose(0, 1, 4, 2, 3).reshape(b, L, h)
    return o.reshape(b, L, h, -1), lse


def causal_mla_attention(q_nope, q_rope, k_nope, k_rope, v):
    S = q_nope.shape[1]
    scale = (q_nope.shape[-1] + q_rope.shape[-1]) ** -0.5
    outs = []
    for i in range(S // BLOCK):
        q0, q1 = i * BLOCK, (i + 1) * BLOCK
        s = (jnp.einsum('bqhd,bkhd->bhqk', q_nope[:, q0:q1], k_nope[:, :q1])
             + jnp.einsum('bqhd,bkd->bhqk', q_rope[:, q0:q1], k_rope[:, :q1])).astype(jnp.float32) * scale
        causal = jnp.arange(q0, q1)[:, None] >= jnp.arange(q1)[None, :]
        p = jax.nn.softmax(jnp.where(causal, s, -jnp.inf), axis=-1).astype(v.dtype)
        outs.append(jnp.einsum('bhqk,bkhd->bqhd', p, v[:, :q1]))
    return jnp.concatenate(outs, axis=1)


def even_mixer(h, positions, w_in, sinks, q_norm, w_uq, kv_norm, w_ukv, w_out):
    b, s, _ = h.shape
    z = h @ w_in
    qa, ka, va, cq, ckv, kr = jnp.split(z, EVEN_SPLITS, axis=-1)
    qa = rope(qa.reshape(b, s, SWA_HEADS, HEAD_DIM), positions)
    ka = rope(ka.reshape(b, s, SWA_KV_HEADS, HEAD_DIM), positions)
    va = va.reshape(b, s, SWA_KV_HEADS, HEAD_DIM)
    oa, _ = banded_attention(qa, ka, va, SWA_WINDOW - 1, sink=sinks)
    qb = (rms_norm(cq, q_norm) @ w_uq).reshape(b, s, MLA_HEADS, MLA_NOPE_DIM + MLA_ROPE_DIM)
    q_nope, q_rope = jnp.split(qb, [MLA_NOPE_DIM], axis=-1)
    q_rope = rope(q_rope, positions)
    kvb = (rms_norm(ckv, kv_norm) @ w_ukv).reshape(b, s, MLA_HEADS, MLA_NOPE_DIM + MLA_V_DIM)
    k_nope, vb = jnp.split(kvb, [MLA_NOPE_DIM], axis=-1)
    k_rope = rope(kr[:, :, None, :], positions)[:, :, 0]
    ob = causal_mla_attention(q_nope, q_rope, k_nope, k_rope, vb)
    o = jnp.concatenate([oa.reshape(b, s, -1), ob.reshape(b, s, -1)], axis=-1)
    return o @ w_out


def dilated_attention(q, k, v):
    b, s, h, dh = q.shape
    outs, lses = [], []
    for window, dil in DIL_PATTERNS:
        span = dil * BLOCK
        L = -(-s // span) * span
        n = L // dil

        def deinterleave(t):
            t = jnp.pad(t, ((0, 0), (0, L - s), (0, 0), (0, 0)))
            return t.reshape(b, n, dil, h, dh).transpose(0, 2, 1, 3, 4).reshape(b * dil, n, h, dh)

        o, lse = banded_attention(deinterleave(q), deinterleave(k), deinterleave(v), window // dil)
        outs.append(o.reshape(b, dil, n, h, dh).transpose(0, 2, 1, 3, 4).reshape(b, L, h, dh)[:, :s])
        lses.append(lse.reshape(b, dil, n, h).transpose(0, 2, 1, 3).reshape(b, L, h)[:, :s])
    wts = jax.nn.softmax(jnp.stack(lses, axis=-1), axis=-1).astype(q.dtype)
    return jnp.einsum('bshdn,bshn->bshd', jnp.stack(outs, axis=-1), wts)


def odd_mixer(h, positions, w_qkv, w_out):
    b, s, _ = h.shape
    q, k, v = jnp.split((h @ w_qkv).reshape(b, s, 3 * DIL_HEADS, HEAD_DIM), 3, axis=2)
    o = dilated_attention(rope(q, positions), rope(k, positions), v)
    return o.reshape(b, s, -1) @ w_out


def memory_cross_attention(h, mem_n, w_q, w_kv, w_o):
    b, s, _ = h.shape
    q = (h @ w_q).reshape(b, s, X_HEADS, X_HEAD_DIM)
    k, v = jnp.split((mem_n @ w_kv).reshape(b, mem_n.shape[1], 2 * X_HEADS, X_HEAD_DIM), 2, axis=2)
    sc = jnp.einsum('bqhd,bkhd->bhqk', q, k).astype(jnp.float32) * (X_HEAD_DIM ** -0.5)
    p = jax.nn.softmax(sc, axis=-1).astype(v.dtype)
    o = jnp.einsum('bhqk,bkhd->bqhd', p, v).reshape(b, s, -1)
    return o @ w_o


def swiglu(h, w_gate, w_up, w_down):
    return (jax.nn.silu(h @ w_gate) * (h @ w_up)) @ w_down


def setup_inputs(seed: int = 0) -> dict:
    key = jax.random.key(seed)
    keys = iter(jax.random.split(key, 64))

    def w(shape, fan_in, gain=1.0):
        return jax.random.normal(next(keys), shape, jnp.float32) * (gain * fan_in ** -0.5)

    def gain_vec(n):
        return 1.0 + 0.02 * jax.random.normal(next(keys), (n,), jnp.float32)

    res_gain = (2.0 * DEPTH) ** -0.5
    inp = {}
    inp['x'] = jax.random.normal(next(keys), (BATCH, SEQ, D_MODEL), jnp.float32)
    inp['mem'] = jax.random.normal(next(keys), (BATCH, MEM_LEN, D_MODEL), jnp.float32)
    offsets = jax.random.randint(next(keys), (BATCH, 1), 0, 4096, dtype=jnp.int32)
    inp['positions'] = jnp.arange(SEQ, dtype=jnp.int32)[None, :] + offsets
    for l in range(DEPTH):
        p = 'l%d_' % l
        inp[p + 'mix_norm'] = gain_vec(D_MODEL)
        if l % 2 == 0:
            inp[p + 'w_in'] = w((D_MODEL, EVEN_IN), D_MODEL)
            inp[p + 'sinks'] = jax.random.normal(next(keys), (SWA_HEADS,), jnp.float32)
            inp[p + 'q_norm'] = gain_vec(MLA_Q_RANK)
            inp[p + 'w_uq'] = w((MLA_Q_RANK, MLA_HEADS * (MLA_NOPE_DIM + MLA_ROPE_DIM)), MLA_Q_RANK)
            inp[p + 'kv_norm'] = gain_vec(MLA_KV_RANK)
            inp[p + 'w_ukv'] = w((MLA_KV_RANK, MLA_HEADS * (MLA_NOPE_DIM + MLA_V_DIM)), MLA_KV_RANK)
            inp[p + 'w_out'] = w((EVEN_OUT, D_MODEL), EVEN_OUT, res_gain)
        else:
            inp[p + 'w_qkv'] = w((D_MODEL, 3 * DIL_HEADS * HEAD_DIM), D_MODEL)
            inp[p + 'w_out'] = w((DIL_HEADS * HEAD_DIM, D_MODEL), DIL_HEADS * HEAD_DIM, res_gain)
        inp[p + 'x_norm'] = gain_vec(D_MODEL)
        inp[p + 'mem_norm'] = gain_vec(D_MODEL)
        inp[p + 'w_xq'] = w((D_MODEL, X_HEADS * X_HEAD_DIM), D_MODEL)
        inp[p + 'w_xkv'] = w((D_MODEL, 2 * X_HEADS * X_HEAD_DIM), D_MODEL)
        inp[p + 'w_xo'] = w((X_HEADS * X_HEAD_DIM, D_MODEL), X_HEADS * X_HEAD_DIM, res_gain)
        inp[p + 'ffn_norm'] = gain_vec(D_MODEL)
        inp[p + 'w_gate'] = w((D_MODEL, FFN_HIDDEN), D_MODEL)
        inp[p + 'w_up'] = w((D_MODEL, FFN_HIDDEN), D_MODEL)
        inp[p + 'w_down'] = w((FFN_HIDDEN, D_MODEL), FFN_HIDDEN, res_gain)
    inp['final_norm'] = gain_vec(D_MODEL)
    return inp


def reference(x, mem, positions,
              l0_mix_norm, l0_w_in, l0_sinks, l0_q_norm, l0_w_uq, l0_kv_norm, l0_w_ukv, l0_w_out,
              l0_x_norm, l0_mem_norm, l0_w_xq, l0_w_xkv, l0_w_xo,
              l0_ffn_norm, l0_w_gate, l0_w_up, l0_w_down,
              l1_mix_norm, l1_w_qkv, l1_w_out,
              l1_x_norm, l1_mem_norm, l1_w_xq, l1_w_xkv, l1_w_xo,
              l1_ffn_norm, l1_w_gate, l1_w_up, l1_w_down,
              final_norm):
    mixers = [
        lambda h: even_mixer(h, positions, l0_w_in, l0_sinks, l0_q_norm, l0_w_uq,
                             l0_kv_norm, l0_w_ukv, l0_w_out),
        lambda h: odd_mixer(h, positions, l1_w_qkv, l1_w_out),
    ]
    mix_norms = [l0_mix_norm, l1_mix_norm]
    xattn = [(l0_x_norm, l0_mem_norm, l0_w_xq, l0_w_xkv, l0_w_xo),
             (l1_x_norm, l1_mem_norm, l1_w_xq, l1_w_xkv, l1_w_xo)]
    ffns = [(l0_ffn_norm, l0_w_gate, l0_w_up, l0_w_down),
            (l1_ffn_norm, l1_w_gate, l1_w_up, l1_w_down)]
    for layer in range(DEPTH):
        x = x + mixers[layer](rms_norm(x, mix_norms[layer]))
        xn, mn, wq, wkv, wo = xattn[layer]
        x = x + memory_cross_attention(rms_norm(x, xn), rms_norm(mem, mn), wq, wkv, wo)
        fn, wg, wu, wd = ffns[layer]
        x = x + swiglu(rms_norm(x, fn), wg, wu, wd)
    return rms_norm(x, final_norm)
```

```python
import functools

import jax
import jax.numpy as jnp
from jax import lax
from jax.experimental import pallas as pl
from jax.experimental.pallas import tpu as pltpu

D_MODEL = 1024
MEM_LEN = 256
HEAD_DIM = 64
ROPE_THETA = 10000.0
NORM_EPS = 1e-6
BLOCK = 128
SWA_HEADS = 8
SWA_KV_HEADS = 2
SWA_WINDOW = 128
MLA_HEADS = 8
MLA_Q_RANK = 384
MLA_KV_RANK = 256
MLA_NOPE_DIM = 64
MLA_ROPE_DIM = 32
MLA_V_DIM = 64
A_Q = SWA_HEADS * HEAD_DIM
A_KV = SWA_KV_HEADS * HEAD_DIM
DIL_HEADS = D_MODEL // HEAD_DIM
DIL_PATTERNS = ((128, 1), (512, 4), (2048, 16))
X_HEADS = 4
X_HEAD_DIM = 128
FFN_HIDDEN = -(-8 * D_MODEL // (3 * 256)) * 256

LANES = 128
V7X_VMEM_LIMIT = 56 * 1024 * 1024
NEG = -1e30

ROW_TILE = 512
DIL_SPAN = 2048
MLA_TQ = 512
MLA_TK = 512
FFN_CHUNKS = 2

F32 = jnp.float32
BF16 = jnp.bfloat16


def _cparams(sem):
    return pltpu.CompilerParams(dimension_semantics=sem, vmem_limit_bytes=V7X_VMEM_LIMIT)


def _const_spec(shape):
    return pl.BlockSpec(shape, lambda *_: (0,) * len(shape))


def _rms(x, g):
    return x * lax.rsqrt(jnp.mean(x * x, axis=-1, keepdims=True) + NORM_EPS) * g


def _rope_chunk(xc, c, s, half):
    lane = lax.broadcasted_iota(jnp.int32, xc.shape, 1)
    up = pltpu.roll(xc, half, 1)
    down = pltpu.roll(xc, LANES - half, 1)
    partner = jnp.where((lane & (2 * half - 1)) < half, down, up)
    return xc * c + partner * s


def _tables_kernel(pos_ref, f64_ref, sg64_ref, f32_ref, sg32_ref, on32_ref, c64_ref, s64_ref, c32_ref, s32_ref):
    pos = pos_ref[...].astype(F32)
    a64 = pos * f64_ref[...]
    c64_ref[...] = jnp.cos(a64)
    s64_ref[...] = jnp.sin(a64) * sg64_ref[...]
    a32 = pos * f32_ref[...]
    on = on32_ref[...] > 0.5
    c32_ref[...] = jnp.where(on, jnp.cos(a32), 1.0)
    s32_ref[...] = jnp.where(on, jnp.sin(a32) * sg32_ref[...], 0.0)


def _rope_tables(positions):
    n = positions.size
    tm = 2048
    lane = jnp.arange(LANES)
    inv64 = ROPE_THETA ** (-jnp.arange(0, HEAD_DIM, 2, dtype=F32) / HEAD_DIM)
    inv32 = ROPE_THETA ** (-jnp.arange(0, MLA_ROPE_DIM, 2, dtype=F32) / MLA_ROPE_DIM)
    f64 = inv64[lane % (HEAD_DIM // 2)][None, :]
    sg64 = jnp.where(lane % HEAD_DIM < HEAD_DIM // 2, -1.0, 1.0).astype(F32)[None, :]
    rl = lane - MLA_NOPE_DIM
    on32 = ((rl >= 0) & (rl < MLA_ROPE_DIM)).astype(F32)[None, :]
    f32v = inv32[jnp.clip(rl, 0, MLA_ROPE_DIM - 1) % (MLA_ROPE_DIM // 2)][None, :]
    sg32 = jnp.where(rl % MLA_ROPE_DIM < MLA_ROPE_DIM // 2, -1.0, 1.0).astype(F32)[None, :]
    row = pl.BlockSpec((tm, LANES), lambda i: (i, 0))
    out = jax.ShapeDtypeStruct((n, LANES), F32)
    return pl.pallas_call(
        _tables_kernel,
        grid=(n // tm,),
        in_specs=[pl.BlockSpec((tm, 1), lambda i: (i, 0))] + [_const_spec((1, LANES))] * 5,
        out_specs=[row] * 4,
        out_shape=[out] * 4,
        compiler_params=_cparams(("arbitrary",)),
        name="rope_tables",
    )(positions.reshape(n, 1), f64, sg64, f32v, sg32, on32)


L0_QA = 0
L0_KA = A_Q
L0_VA = L0_KA + 2 * A_KV
L0_CQ = L0_VA + 2 * A_KV
L0_CKV = L0_CQ + MLA_Q_RANK
L0_KR = L0_CKV + MLA_KV_RANK
L0_COLS = L0_KR + LANES
MLA_QK = MLA_HEADS * LANES
MLA_V = MLA_HEADS * MLA_V_DIM


def _l0_proj_kernel(x_ref, g_ref, win_ref, gq_ref, wuq_ref, gkv_ref, wukv_ref, c64_ref, s64_ref, c32_ref, s32_ref,
                    qa_ref, ka_ref, va_ref, qb_ref, kb_ref, vb_ref):
    h = _rms(x_ref[...], g_ref[...]).astype(BF16)
    z = jnp.dot(h, win_ref[...], preferred_element_type=F32)
    c64, s64, c32, s32 = c64_ref[...], s64_ref[...], c32_ref[...], s32_ref[...]
    qscale = HEAD_DIM ** -0.5
    for c in range(A_Q // LANES):
        sl = slice(c * LANES, (c + 1) * LANES)
        qa_ref[:, sl] = (_rope_chunk(z[:, sl], c64, s64, HEAD_DIM // 2) * qscale).astype(qa_ref.dtype)
    for c in range(2 * A_KV // LANES):
        sl = slice(c * LANES, (c + 1) * LANES)
        ka_ref[:, sl] = _rope_chunk(z[:, L0_KA + c * LANES:L0_KA + (c + 1) * LANES], c64, s64,
                                    HEAD_DIM // 2).astype(ka_ref.dtype)
    va_ref[...] = z[:, L0_VA:L0_CQ].astype(va_ref.dtype)

    cq = _rms(z[:, L0_CQ:L0_CKV], gq_ref[...]).astype(BF16)
    qb = jnp.dot(cq, wuq_ref[...], preferred_element_type=F32)
    ckv = _rms(z[:, L0_CKV:L0_KR], gkv_ref[...]).astype(BF16)
    kv = jnp.dot(ckv, wukv_ref[...], preferred_element_type=F32)
    kr = _rope_chunk(z[:, L0_KR:L0_COLS], c32, s32, MLA_ROPE_DIM // 2)
    for hd in range(MLA_HEADS):
        sl = slice(hd * LANES, (hd + 1) * LANES)
        qb_ref[:, sl] = _rope_chunk(qb[:, sl], c32, s32, MLA_ROPE_DIM // 2).astype(qb_ref.dtype)
        kb_ref[:, sl] = (kv[:, sl] + kr).astype(kb_ref.dtype)
    vb_ref[...] = kv[:, MLA_QK:].astype(vb_ref.dtype)


def _l0_proj(x2, g, w_in, g_q, w_uq, g_kv, w_ukv, tables):
    n = x2.shape[0]
    tm = ROW_TILE
    qa_w, ka_w, va_w, cq_w, ckv_w, kr_w = jnp.split(
        w_in, [A_Q, A_Q + A_KV, A_Q + 2 * A_KV, A_Q + 2 * A_KV + MLA_Q_RANK,
               A_Q + 2 * A_KV + MLA_Q_RANK + MLA_KV_RANK], axis=1)

    def dup(w):
        return jnp.repeat(w.reshape(D_MODEL, SWA_KV_HEADS, 1, HEAD_DIM), 2, axis=2).reshape(D_MODEL, 2 * A_KV)

    kr_pad = jnp.pad(kr_w, ((0, 0), (MLA_NOPE_DIM, LANES - MLA_NOPE_DIM - MLA_ROPE_DIM)))
    win_p = jnp.concatenate([qa_w, dup(ka_w), dup(va_w), cq_w, ckv_w, kr_pad], axis=1).astype(BF16)
    per_head_q = MLA_NOPE_DIM + MLA_ROPE_DIM
    wuq_p = jnp.pad(w_uq.reshape(MLA_Q_RANK, MLA_HEADS, per_head_q),
                    ((0, 0), (0, 0), (0, LANES - per_head_q))).reshape(MLA_Q_RANK, MLA_QK).astype(BF16)
    wukv3 = w_ukv.reshape(MLA_KV_RANK, MLA_HEADS, MLA_NOPE_DIM + MLA_V_DIM)
    wk_p = jnp.pad(wukv3[:, :, :MLA_NOPE_DIM], ((0, 0), (0, 0), (0, LANES - MLA_NOPE_DIM))).reshape(MLA_KV_RANK, MLA_QK)
    wv_p = wukv3[:, :, MLA_NOPE_DIM:].reshape(MLA_KV_RANK, MLA_V)
    wukv_p = jnp.concatenate([wk_p, wv_p], axis=1).astype(BF16)

    def row(w):
        return pl.BlockSpec((tm, w), lambda i: (i, 0))

    outs = [(A_Q, BF16), (2 * A_KV, BF16), (2 * A_KV, BF16), (MLA_QK, BF16), (MLA_QK, BF16), (MLA_V, BF16)]
    return pl.pallas_call(
        _l0_proj_kernel,
        grid=(n // tm,),
        in_specs=[row(D_MODEL), _const_spec((1, D_MODEL)), _const_spec(win_p.shape),
                  _const_spec((1, MLA_Q_RANK)), _const_spec(wuq_p.shape),
                  _const_spec((1, MLA_KV_RANK)), _const_spec(wukv_p.shape)] + [row(LANES)] * 4,
        out_specs=[row(w) for w, _ in outs],
        out_shape=[jax.ShapeDtypeStruct((n, w), dt) for w, dt in outs],
        compiler_params=_cparams(("arbitrary",)),
        name="l0_proj",
    )(x2, g[None, :], win_p, g_q[None, :], wuq_p, g_kv[None, :], wukv_p, *tables)


def _l1_proj_kernel(x_ref, g_ref, w_ref, c64_ref, s64_ref, q_ref, k_ref, v_ref):
    h = _rms(x_ref[...], g_ref[...]).astype(BF16)
    z = jnp.dot(h, w_ref[...], preferred_element_type=F32)
    c64, s64 = c64_ref[...], s64_ref[...]
    qscale = HEAD_DIM ** -0.5
    for c in range(D_MODEL // LANES):
        sl = slice(c * LANES, (c + 1) * LANES)
        q_ref[:, sl] = _rope_chunk(z[:, sl], c64, s64, HEAD_DIM // 2) * qscale
        k_ref[:, sl] = _rope_chunk(z[:, D_MODEL + c * LANES:D_MODEL + (c + 1) * LANES], c64, s64, HEAD_DIM // 2)
    v_ref[...] = z[:, 2 * D_MODEL:]


def _l1_proj(x2, g, w_qkv, tables):
    n = x2.shape[0]
    tm = ROW_TILE
    row = pl.BlockSpec((tm, D_MODEL), lambda i: (i, 0))
    tab = pl.BlockSpec((tm, LANES), lambda i: (i, 0))
    out = jax.ShapeDtypeStruct((n, D_MODEL), F32)
    return pl.pallas_call(
        _l1_proj_kernel,
        grid=(n // tm,),
        in_specs=[row, _const_spec((1, D_MODEL)), _const_spec(w_qkv.shape), tab, tab],
        out_specs=[row] * 3,
        out_shape=[out] * 3,
        compiler_params=_cparams(("arbitrary",)),
        name="l1_proj",
    )(x2, g[None, :], w_qkv.astype(BF16), tables[0], tables[1])


def _banded_kernel(patterns, use_sink, span, *refs):
    refs = list(refs)
    sinks_ref = refs.pop(0) if use_sink else None
    q_ref, kp_ref, kc_ref, vp_ref, vc_ref, o_ref, kk, vv = refs[:8]
    stats = refs[8:]
    multi = len(patterns) > 1
    blk = pl.program_id(1)
    pair = pl.program_id(2)

    kk[0:span, :] = kp_ref[...]
    kk[span:2 * span, :] = kc_ref[...]
    vv[0:span, :] = vp_ref[...]
    vv[span:2 * span, :] = vc_ref[...]

    lo_q = lax.broadcasted_iota(jnp.int32, (BLOCK, LANES), 1) < HEAD_DIM
    lo_kv = lax.broadcasted_iota(jnp.int32, (2 * BLOCK, LANES), 1) < HEAD_DIM
    qi = lax.broadcasted_iota(jnp.int32, (BLOCK, 2 * BLOCK), 0)
    kj = lax.broadcasted_iota(jnp.int32, (BLOCK, 2 * BLOCK), 1)
    dist = BLOCK + qi - kj
    in_cur = kj >= BLOCK

    for pi, (dil, max_dist) in enumerate(patterns):
        per_stream = span // (BLOCK * dil)
        band = (dist >= 0) & (dist <= max_dist)

        def rows(start, count, dil=dil):
            if dil == 1:
                return pl.ds(pl.multiple_of(start, BLOCK), count)
            return pl.ds(start, count, stride=dil)

        def sub_block(idx, carry, pi=pi, dil=dil, per_stream=per_stream, band=band, rows=rows):
            stream = idx // per_stream
            j = idx % per_stream
            start = stream + dil * BLOCK * j
            q = q_ref[rows(start, BLOCK), :].astype(BF16)
            kstart = span + start - dil * BLOCK
            k2 = kk[rows(kstart, 2 * BLOCK), :].astype(BF16)
            v2 = vv[rows(kstart, 2 * BLOCK), :].astype(BF16)
            has_prev = jnp.logical_or(blk > 0, j > 0)
            mask = band & jnp.logical_or(in_cur, has_prev)
            acc = None
            ms, ls = [], []
            for e, (sel_q, sel_kv) in enumerate(((lo_q, lo_kv), (~lo_q, ~lo_kv))):
                qe = jnp.where(sel_q, q, jnp.zeros_like(q))
                s = lax.dot_general(qe, k2, (((1,), (1,)), ((), ())), preferred_element_type=F32)
                s = jnp.where(mask, s, NEG)
                m = jnp.max(s, axis=1, keepdims=True)
                if use_sink:
                    sk = sinks_ref[2 * pair + e]
                    m = jnp.maximum(m, sk)
                p = jnp.exp(s - m)
                l = jnp.sum(p, axis=1, keepdims=True)
                if use_sink:
                    l = l + jnp.exp(sk - m)
                ve = jnp.where(sel_kv, v2, jnp.zeros_like(v2))
                o = jnp.dot(p.astype(BF16), ve, preferred_element_type=F32)
                acc = o if acc is None else acc + o
                ms.append(m)
                ls.append(l)
            l_pair = jnp.where(lo_q, ls[0], ls[1])
            if multi:
                acc_s, m_s, l_s = stats[3 * pi:3 * pi + 3]
                acc_s[rows(start, BLOCK), :] = acc
                m_s[rows(start, BLOCK), :] = jnp.where(lo_q, ms[0], ms[1])
                l_s[rows(start, BLOCK), :] = l_pair
            else:
                o_ref[rows(start, BLOCK), :] = (acc / l_pair).astype(o_ref.dtype)
            return carry

        lax.fori_loop(0, span // BLOCK, sub_block, 0)

    if multi:
        chunk = 2 * BLOCK

        def merge(c, carry):
            rs = pl.ds(pl.multiple_of(c * chunk, chunk), chunk)
            m_all = [stats[3 * pi + 1][rs, :] for pi in range(len(patterns))]
            m_top = functools.reduce(jnp.maximum, m_all)
            num = jnp.zeros((chunk, LANES), F32)
            den = jnp.zeros((chunk, LANES), F32)
            for pi in range(len(patterns)):
                w = jnp.exp(m_all[pi] - m_top)
                num = num + w * stats[3 * pi][rs, :]
                den = den + w * stats[3 * pi + 2][rs, :]
            o_ref[rs, :] = (num / den).astype(o_ref.dtype)
            return carry

        lax.fori_loop(0, span // chunk, merge, 0)


def _banded_attention(q, k, v, batch, seq, n_pairs, kv_chunk_of_pair, patterns, sinks=None):
    n = q.shape[0]
    span = DIL_SPAN
    nblk = seq // span
    use_sink = sinks is not None

    def q_map(b, i, p):
        return (b * nblk + i, p)

    def prev_map(b, i, p):
        return (b * nblk + jnp.maximum(i - 1, 0), kv_chunk_of_pair(p))

    def cur_map(b, i, p):
        return (b * nblk + i, kv_chunk_of_pair(p))

    blk = (span, LANES)
    in_specs = [pl.BlockSpec(blk, q_map), pl.BlockSpec(blk, prev_map), pl.BlockSpec(blk, cur_map),
                pl.BlockSpec(blk, prev_map), pl.BlockSpec(blk, cur_map)]
    args = [q, k, k, v, v]
    if use_sink:
        in_specs = [pl.BlockSpec(memory_space=pltpu.SMEM)] + in_specs
        args = [sinks.astype(F32)] + args
    scratch = [pltpu.VMEM((2 * span, LANES), k.dtype), pltpu.VMEM((2 * span, LANES), v.dtype)]
    if len(patterns) > 1:
        scratch += [pltpu.VMEM((span, LANES), F32)] * (3 * len(patterns))
    return pl.pallas_call(
        functools.partial(_banded_kernel, patterns, use_sink, span),
        grid=(batch, nblk, n_pairs),
        in_specs=in_specs,
        out_specs=pl.BlockSpec(blk, q_map),
        out_shape=jax.ShapeDtypeStruct((n, n_pairs * LANES), BF16),
        scratch_shapes=scratch,
        compiler_params=_cparams(("arbitrary",) * 3),
        name="banded_attention_%d" % len(patterns),
    )(*args)


def _mla_kernel(tq, tk, scale, q_ref, k_ref, v_ref, o_ref):
    qi = pl.program_id(2)
    lo_v = lax.broadcasted_iota(jnp.int32, (tk, LANES), 1) < MLA_V_DIM
    lo_o = lax.broadcasted_iota(jnp.int32, (tq, LANES), 1) < MLA_V_DIM
    qs = (q_ref[:, 0:LANES], q_ref[:, LANES:2 * LANES])
    row_pos = qi * tq + lax.broadcasted_iota(jnp.int32, (tq, tk), 0)
    col_off = lax.broadcasted_iota(jnp.int32, (tq, tk), 1)

    def step(kb, carry, masked):
        m0, l0, m1, l1, acc = carry
        ks = pl.multiple_of(kb * tk, tk)
        kt = k_ref[pl.ds(ks, tk), :]
        vt = v_ref[pl.ds(ks, tk), :]
        new = []
        pv = None
        alphas = []
        for e, (m_old, l_old, sel) in enumerate(((m0, l0, lo_v), (m1, l1, ~lo_v))):
            s = lax.dot_general(qs[e], kt[:, e * LANES:(e + 1) * LANES], (((1,), (1,)), ((), ())),
                                preferred_element_type=F32) * scale
            if masked:
                s = jnp.where(row_pos >= ks + col_off, s, NEG)
            m_new = jnp.maximum(m_old, jnp.max(s, axis=1, keepdims=True))
            alpha = jnp.exp(m_old - m_new)
            p = jnp.exp(s - m_new)
            l_new = alpha * l_old + jnp.sum(p, axis=1, keepdims=True)
            ve = jnp.where(sel, vt, jnp.zeros_like(vt))
            o = jnp.dot(p.astype(BF16), ve, preferred_element_type=F32)
            pv = o if pv is None else pv + o
            new += [m_new, l_new]
            alphas.append(alpha)
        acc = acc * jnp.where(lo_o, alphas[0], alphas[1]) + pv
        return (new[0], new[1], new[2], new[3], acc)

    col = jnp.full((tq, 1), NEG, F32)
    zero = jnp.zeros((tq, 1), F32)
    carry = (col, zero, col, zero, jnp.zeros((tq, LANES), F32))
    n_full = qi * (tq // tk)
    carry = lax.fori_loop(0, n_full, lambda kb, c: step(kb, c, False), carry)
    for c in range(tq // tk):
        carry = step(n_full + c, carry, True)
    _, l0, _, l1, acc = carry
    o_ref[...] = (acc / jnp.where(lo_o, l0, l1)).astype(o_ref.dtype)


def _mla_attention(qb, kb, vb, batch, seq):
    n = qb.shape[0]
    tq, tk = MLA_TQ, MLA_TK
    nq = seq // tq
    n_pairs = MLA_HEADS // 2
    scale = (MLA_NOPE_DIM + MLA_ROPE_DIM) ** -0.5
    return pl.pallas_call(
        functools.partial(_mla_kernel, tq, tk, scale),
        grid=(batch, n_pairs, nq),
        in_specs=[pl.BlockSpec((tq, 2 * LANES), lambda b, p, i: (b * nq + i, p)),
                  pl.BlockSpec((seq, 2 * LANES), lambda b, p, i: (b, p)),
                  pl.BlockSpec((seq, LANES), lambda b, p, i: (b, p))],
        out_specs=pl.BlockSpec((tq, LANES), lambda b, p, i: (b * nq + i, p)),
        out_shape=jax.ShapeDtypeStruct((n, n_pairs * LANES), BF16),
        compiler_params=_cparams(("arbitrary",) * 3),
        name="mla_attention",
    )(qb, kb, vb)


def _outproj_kernel(n_in, *refs):
    x_ref = refs[0]
    a_refs = refs[1:1 + n_in]
    w_refs = refs[1 + n_in:1 + 2 * n_in]
    g_ref, wq_ref, x1_ref, xq_ref = refs[1 + 2 * n_in:]
    x1 = x_ref[...]
    for a_ref, w_ref in zip(a_refs, w_refs):
        x1 = x1 + jnp.dot(a_ref[...], w_ref[...], preferred_element_type=F32)
    x1_ref[...] = x1
    h = _rms(x1, g_ref[...]).astype(BF16)
    xq_ref[...] = jnp.dot(h, wq_ref[...], preferred_element_type=F32).astype(xq_ref.dtype)


def _outproj(x2, acts, weights, g_x, w_xq):
    n = x2.shape[0]
    tm = ROW_TILE
    xq_w = X_HEADS * X_HEAD_DIM

    def row(w):
        return pl.BlockSpec((tm, w), lambda i: (i, 0))

    weights = [w.astype(BF16) for w in weights]
    return pl.pallas_call(
        functools.partial(_outproj_kernel, len(acts)),
        grid=(n // tm,),
        in_specs=[row(D_MODEL)] + [row(a.shape[1]) for a in acts] + [_const_spec(w.shape) for w in weights]
        + [_const_spec((1, D_MODEL)), _const_spec((D_MODEL, xq_w))],
        out_specs=[row(D_MODEL), row(xq_w)],
        out_shape=[jax.ShapeDtypeStruct((n, D_MODEL), F32), jax.ShapeDtypeStruct((n, xq_w), BF16)],
        compiler_params=_cparams(("arbitrary",)),
        name="outproj",
    )(x2, *acts, *weights, g_x[None, :], w_xq.astype(BF16))


def _memkv_kernel(mem_ref, g_ref, w_ref, kv_ref):
    h = _rms(mem_ref[...], g_ref[...]).astype(BF16)
    kv_ref[...] = jnp.dot(h, w_ref[...], preferred_element_type=F32).astype(kv_ref.dtype)


def _memkv(mem2, g, w_xkv):
    n = mem2.shape[0]
    cols = 2 * X_HEADS * X_HEAD_DIM
    return pl.pallas_call(
        _memkv_kernel,
        grid=(n // MEM_LEN,),
        in_specs=[pl.BlockSpec((MEM_LEN, D_MODEL), lambda i: (i, 0)), _const_spec((1, D_MODEL)),
                  _const_spec((D_MODEL, cols))],
        out_specs=pl.BlockSpec((MEM_LEN, cols), lambda i: (i, 0)),
        out_shape=jax.ShapeDtypeStruct((n, cols), BF16),
        compiler_params=_cparams(("arbitrary",)),
        name="memkv",
    )(mem2, g[None, :], w_xkv.astype(BF16))


def _xattn_kernel(x_ref, q_ref, kv_ref, wo_ref, o_ref):
    scale = X_HEAD_DIM ** -0.5
    heads = []
    for hd in range(X_HEADS):
        q = q_ref[:, hd * X_HEAD_DIM:(hd + 1) * X_HEAD_DIM]
        k = kv_ref[:, hd * X_HEAD_DIM:(hd + 1) * X_HEAD_DIM]
        v = kv_ref[:, (X_HEADS + hd) * X_HEAD_DIM:(X_HEADS + hd + 1) * X_HEAD_DIM]
        s = lax.dot_general(q, k, (((1,), (1,)), ((), ())), preferred_element_type=F32) * scale
        m = jnp.max(s, axis=1, keepdims=True)
        p = jnp.exp(s - m)
        l = jnp.sum(p, axis=1, keepdims=True)
        o = jnp.dot(p.astype(BF16), v, preferred_element_type=F32) / l
        heads.append(o.astype(BF16))
    o_all = jnp.concatenate(heads, axis=1)
    o_ref[...] = x_ref[...] + jnp.dot(o_all, wo_ref[...], preferred_element_type=F32)


def _xattn(x2, xq, kv, w_xo, seq):
    n = x2.shape[0]
    tm = ROW_TILE
    per_batch = seq // tm
    xq_w = X_HEADS * X_HEAD_DIM
    return pl.pallas_call(
        _xattn_kernel,
        grid=(n // tm,),
        in_specs=[pl.BlockSpec((tm, D_MODEL), lambda i: (i, 0)), pl.BlockSpec((tm, xq_w), lambda i: (i, 0)),
                  pl.BlockSpec((MEM_LEN, 2 * xq_w), lambda i: (i // per_batch, 0)), _const_spec((xq_w, D_MODEL))],
        out_specs=pl.BlockSpec((tm, D_MODEL), lambda i: (i, 0)),
        out_shape=jax.ShapeDtypeStruct((n, D_MODEL), F32),
        compiler_params=_cparams(("arbitrary",)),
        name="xattn",
    )(x2, xq, kv, w_xo.astype(BF16))


def _ffn_kernel(final, x_ref, g_ref, wg_ref, wu_ref, wd_ref, *rest):
    if final:
        gf_ref, o_ref = rest
    else:
        (o_ref,) = rest
    x = x_ref[...]
    h = _rms(x, g_ref[...]).astype(BF16)
    width = FFN_HIDDEN // FFN_CHUNKS
    acc = x
    for c in range(FFN_CHUNKS):
        sl = slice(c * width, (c + 1) * width)
        gate = jnp.dot(h, wg_ref[:, sl], preferred_element_type=F32)
        up = jnp.dot(h, wu_ref[:, sl], preferred_element_type=F32)
        act = (gate * jax.nn.sigmoid(gate) * up).astype(BF16)
        acc = acc + jnp.dot(act, wd_ref[sl, :], preferred_element_type=F32)
    o_ref[...] = _rms(acc, gf_ref[...]) if final else acc


def _ffn(x2, g, w_gate, w_up, w_down, g_final=None):
    n = x2.shape[0]
    tm = ROW_TILE
    final = g_final is not None
    row = pl.BlockSpec((tm, D_MODEL), lambda i: (i, 0))

    def weight(shape):
        return pl.BlockSpec(shape, lambda i: (0, 0), pipeline_mode=pl.Buffered(1))

    in_specs = [row, _const_spec((1, D_MODEL)), weight((D_MODEL, FFN_HIDDEN)), weight((D_MODEL, FFN_HIDDEN)),
                weight((FFN_HIDDEN, D_MODEL))]
    args = [x2, g[None, :], w_gate.astype(BF16), w_up.astype(BF16), w_down.astype(BF16)]
    if final:
        in_specs.append(_const_spec((1, D_MODEL)))
        args.append(g_final[None, :])
    return pl.pallas_call(
        functools.partial(_ffn_kernel, final),
        grid=(n // tm,),
        in_specs=in_specs,
        out_specs=row,
        out_shape=jax.ShapeDtypeStruct((n, D_MODEL), F32),
        compiler_params=_cparams(("arbitrary",)),
        name="ffn",
    )(*args)


def kernel(x, mem, positions, l0_mix_norm, l0_w_in, l0_sinks, l0_q_norm, l0_w_uq, l0_kv_norm, l0_w_ukv, l0_w_out, l0_x_norm, l0_mem_norm, l0_w_xq, l0_w_xkv, l0_w_xo, l0_ffn_norm, l0_w_gate, l0_w_up, l0_w_down, l1_mix_norm, l1_w_qkv, l1_w_out, l1_x_norm, l1_mem_norm, l1_w_xq, l1_w_xkv, l1_w_xo, l1_ffn_norm, l1_w_gate, l1_w_up, l1_w_down, final_norm):
    batch, seq, _ = x.shape
    assert seq % DIL_SPAN == 0 and seq % MLA_TQ == 0 and seq % ROW_TILE == 0
    n = batch * seq
    x2 = x.reshape(n, D_MODEL)
    mem2 = mem.reshape(batch * MEM_LEN, D_MODEL)
    tables = _rope_tables(positions)

    qa, ka, va, qb, kb, vb = _l0_proj(x2, l0_mix_norm, l0_w_in, l0_q_norm, l0_w_uq, l0_kv_norm, l0_w_ukv, tables)
    oa = _banded_attention(qa, ka, va, batch, seq, SWA_HEADS // 2, lambda p: p // 2,
                           ((1, SWA_WINDOW - 1),), sinks=l0_sinks)
    ob = _mla_attention(qb, kb, vb, batch, seq)
    x2, xq = _outproj(x2, [oa, ob], [l0_w_out[:A_Q], l0_w_out[A_Q:]], l0_x_norm, l0_w_xq)
    x2 = _xattn(x2, xq, _memkv(mem2, l0_mem_norm, l0_w_xkv), l0_w_xo, seq)
    x2 = _ffn(x2, l0_ffn_norm, l0_w_gate, l0_w_up, l0_w_down)

    q, k, v = _l1_proj(x2, l1_mix_norm, l1_w_qkv, tables)
    od = _banded_attention(q, k, v, batch, seq, DIL_HEADS // 2, lambda p: p,
                           tuple((dil, window // dil) for window, dil in DIL_PATTERNS))
    x2, xq = _outproj(x2, [od], [l1_w_out], l1_x_norm, l1_w_xq)
    x2 = _xattn(x2, xq, _memkv(mem2, l1_mem_norm, l1_w_xkv), l1_w_xo, seq)
    x2 = _ffn(x2, l1_ffn_norm, l1_w_gate, l1_w_up, l1_w_down, g_final=final_norm)
    return x2.reshape(batch, seq, D_MODEL)
```

```python
import functools

import jax
import jax.numpy as jnp
from jax import lax
from jax.experimental import pallas as pl
from jax.experimental.pallas import tpu as pltpu

D_MODEL = 1024
MEM_LEN = 256
HEAD_DIM = 64
ROPE_THETA = 10000.0
NORM_EPS = 1e-6
BLOCK = 128
SWA_HEADS = 8
SWA_KV_HEADS = 2
SWA_WINDOW = 128
MLA_HEADS = 8
MLA_Q_RANK = 384
MLA_KV_RANK = 256
MLA_NOPE_DIM = 64
MLA_ROPE_DIM = 32
MLA_V_DIM = 64
A_Q = SWA_HEADS * HEAD_DIM
A_KV = SWA_KV_HEADS * HEAD_DIM
DIL_HEADS = D_MODEL // HEAD_DIM
DIL_PATTERNS = ((128, 1), (512, 4), (2048, 16))
X_HEADS = 4
X_HEAD_DIM = 128
FFN_HIDDEN = -(-8 * D_MODEL // (3 * 256)) * 256

LANES = 128
V7X_VMEM_LIMIT = 56 * 1024 * 1024
NEG = -1e30

ROW_TILE = 512
DIL_SPAN = 2048
BANDED_GROUP = 4
MLA_TQ = 512
MLA_TK = 1024
FFN_CHUNKS = 2

F32 = jnp.float32
BF16 = jnp.bfloat16


def _cparams(sem):
    return pltpu.CompilerParams(dimension_semantics=sem, vmem_limit_bytes=V7X_VMEM_LIMIT)


def _const_spec(shape):
    return pl.BlockSpec(shape, lambda *_: (0,) * len(shape))


def _rms(x, g):
    return x * lax.rsqrt(jnp.mean(x * x, axis=-1, keepdims=True) + NORM_EPS) * g


def _rope_chunk(xc, c, s, half):
    lane = lax.broadcasted_iota(jnp.int32, xc.shape, 1)
    up = pltpu.roll(xc, half, 1)
    down = pltpu.roll(xc, LANES - half, 1)
    partner = jnp.where((lane & (2 * half - 1)) < half, down, up)
    return xc * c + partner * s


def _tables_kernel(pos_ref, f64_ref, sg64_ref, f32_ref, sg32_ref, on32_ref, c64_ref, s64_ref, c32_ref, s32_ref):
    pos = pos_ref[...].astype(F32)
    a64 = pos * f64_ref[...]
    c64_ref[...] = jnp.cos(a64)
    s64_ref[...] = jnp.sin(a64) * sg64_ref[...]
    a32 = pos * f32_ref[...]
    on = on32_ref[...] > 0.5
    c32_ref[...] = jnp.where(on, jnp.cos(a32), 1.0)
    s32_ref[...] = jnp.where(on, jnp.sin(a32) * sg32_ref[...], 0.0)


def _rope_tables(positions):
    n = positions.size
    tm = 2048
    lane = jnp.arange(LANES)
    inv64 = ROPE_THETA ** (-jnp.arange(0, HEAD_DIM, 2, dtype=F32) / HEAD_DIM)
    inv32 = ROPE_THETA ** (-jnp.arange(0, MLA_ROPE_DIM, 2, dtype=F32) / MLA_ROPE_DIM)
    f64 = inv64[lane % (HEAD_DIM // 2)][None, :]
    sg64 = jnp.where(lane % HEAD_DIM < HEAD_DIM // 2, -1.0, 1.0).astype(F32)[None, :]
    rl = lane - MLA_NOPE_DIM
    on32 = ((rl >= 0) & (rl < MLA_ROPE_DIM)).astype(F32)[None, :]
    f32v = inv32[jnp.clip(rl, 0, MLA_ROPE_DIM - 1) % (MLA_ROPE_DIM // 2)][None, :]
    sg32 = jnp.where(rl % MLA_ROPE_DIM < MLA_ROPE_DIM // 2, -1.0, 1.0).astype(F32)[None, :]
    row = pl.BlockSpec((tm, LANES), lambda i: (i, 0))
    out = jax.ShapeDtypeStruct((n, LANES), F32)
    return pl.pallas_call(
        _tables_kernel,
        grid=(n // tm,),
        in_specs=[pl.BlockSpec((tm, 1), lambda i: (i, 0))] + [_const_spec((1, LANES))] * 5,
        out_specs=[row] * 4,
        out_shape=[out] * 4,
        compiler_params=_cparams(("arbitrary",)),
        name="rope_tables",
    )(positions.reshape(n, 1), f64, sg64, f32v, sg32, on32)


L0_QA = 0
L0_KA = A_Q
L0_VA = L0_KA + 2 * A_KV
L0_CQ = L0_VA + 2 * A_KV
L0_CKV = L0_CQ + MLA_Q_RANK
L0_KR = L0_CKV + MLA_KV_RANK
L0_COLS = L0_KR + LANES
MLA_QK = MLA_HEADS * LANES
MLA_Q_LOG2_SCALE = (MLA_NOPE_DIM + MLA_ROPE_DIM) ** -0.5 * 1.4426950408889634
MLA_V = MLA_HEADS * MLA_V_DIM


def _l0_proj_kernel(x_ref, g_ref, win_ref, gq_ref, wuq_ref, gkv_ref, wukv_ref, c64_ref, s64_ref, c32_ref, s32_ref,
                    qa_ref, ka_ref, va_ref, qb_ref, kb_ref, vb_ref):
    h = _rms(x_ref[...], g_ref[...]).astype(BF16)
    z = jnp.dot(h, win_ref[...], preferred_element_type=F32)
    c64, s64, c32, s32 = c64_ref[...], s64_ref[...], c32_ref[...], s32_ref[...]
    qscale = HEAD_DIM ** -0.5
    for c in range(A_Q // LANES):
        sl = slice(c * LANES, (c + 1) * LANES)
        qa_ref[:, sl] = (_rope_chunk(z[:, sl], c64, s64, HEAD_DIM // 2) * qscale).astype(qa_ref.dtype)
    for c in range(2 * A_KV // LANES):
        sl = slice(c * LANES, (c + 1) * LANES)
        ka_ref[:, sl] = _rope_chunk(z[:, L0_KA + c * LANES:L0_KA + (c + 1) * LANES], c64, s64,
                                    HEAD_DIM // 2).astype(ka_ref.dtype)
    va_ref[...] = z[:, L0_VA:L0_CQ].astype(va_ref.dtype)

    cq = _rms(z[:, L0_CQ:L0_CKV], gq_ref[...]).astype(BF16)
    qb = jnp.dot(cq, wuq_ref[...], preferred_element_type=F32)
    ckv = _rms(z[:, L0_CKV:L0_KR], gkv_ref[...]).astype(BF16)
    kv = jnp.dot(ckv, wukv_ref[...], preferred_element_type=F32)
    kr = _rope_chunk(z[:, L0_KR:L0_COLS], c32, s32, MLA_ROPE_DIM // 2)
    for hd in range(MLA_HEADS):
        sl = slice(hd * LANES, (hd + 1) * LANES)
        qb_ref[:, sl] = (_rope_chunk(qb[:, sl], c32, s32, MLA_ROPE_DIM // 2) * MLA_Q_LOG2_SCALE).astype(qb_ref.dtype)
        kb_ref[:, sl] = (kv[:, sl] + kr).astype(kb_ref.dtype)
    vb_ref[...] = kv[:, MLA_QK:].astype(vb_ref.dtype)


def _l0_proj(x2, g, w_in, g_q, w_uq, g_kv, w_ukv, tables):
    n = x2.shape[0]
    tm = ROW_TILE
    qa_w, ka_w, va_w, cq_w, ckv_w, kr_w = jnp.split(
        w_in, [A_Q, A_Q + A_KV, A_Q + 2 * A_KV, A_Q + 2 * A_KV + MLA_Q_RANK,
               A_Q + 2 * A_KV + MLA_Q_RANK + MLA_KV_RANK], axis=1)

    def dup(w):
        return jnp.repeat(w.reshape(D_MODEL, SWA_KV_HEADS, 1, HEAD_DIM), 2, axis=2).reshape(D_MODEL, 2 * A_KV)

    kr_pad = jnp.pad(kr_w, ((0, 0), (MLA_NOPE_DIM, LANES - MLA_NOPE_DIM - MLA_ROPE_DIM)))
    win_p = jnp.concatenate([qa_w, dup(ka_w), dup(va_w), cq_w, ckv_w, kr_pad], axis=1).astype(BF16)
    per_head_q = MLA_NOPE_DIM + MLA_ROPE_DIM
    wuq_p = jnp.pad(w_uq.reshape(MLA_Q_RANK, MLA_HEADS, per_head_q),
                    ((0, 0), (0, 0), (0, LANES - per_head_q))).reshape(MLA_Q_RANK, MLA_QK).astype(BF16)
    wukv3 = w_ukv.reshape(MLA_KV_RANK, MLA_HEADS, MLA_NOPE_DIM + MLA_V_DIM)
    wk_p = jnp.pad(wukv3[:, :, :MLA_NOPE_DIM], ((0, 0), (0, 0), (0, LANES - MLA_NOPE_DIM))).reshape(MLA_KV_RANK, MLA_QK)
    wv_p = wukv3[:, :, MLA_NOPE_DIM:].reshape(MLA_KV_RANK, MLA_V)
    wukv_p = jnp.concatenate([wk_p, wv_p], axis=1).astype(BF16)

    def row(w):
        return pl.BlockSpec((tm, w), lambda i: (i, 0))

    outs = [(A_Q, BF16), (2 * A_KV, BF16), (2 * A_KV, BF16), (MLA_QK, BF16), (MLA_QK, BF16), (MLA_V, BF16)]
    return pl.pallas_call(
        _l0_proj_kernel,
        grid=(n // tm,),
        in_specs=[row(D_MODEL), _const_spec((1, D_MODEL)), _const_spec(win_p.shape),
                  _const_spec((1, MLA_Q_RANK)), _const_spec(wuq_p.shape),
                  _const_spec((1, MLA_KV_RANK)), _const_spec(wukv_p.shape)] + [row(LANES)] * 4,
        out_specs=[row(w) for w, _ in outs],
        out_shape=[jax.ShapeDtypeStruct((n, w), dt) for w, dt in outs],
        compiler_params=_cparams(("arbitrary",)),
        name="l0_proj",
    )(x2, g[None, :], win_p, g_q[None, :], wuq_p, g_kv[None, :], wukv_p, *tables)


def _l1_proj_kernel(x_ref, g_ref, w_ref, c64_ref, s64_ref, q_ref, k_ref, v_ref):
    h = _rms(x_ref[...], g_ref[...]).astype(BF16)
    z = jnp.dot(h, w_ref[...], preferred_element_type=F32)
    c64, s64 = c64_ref[...], s64_ref[...]
    qscale = HEAD_DIM ** -0.5
    for c in range(D_MODEL // LANES):
        sl = slice(c * LANES, (c + 1) * LANES)
        q_ref[:, sl] = _rope_chunk(z[:, sl], c64, s64, HEAD_DIM // 2) * qscale
        k_ref[:, sl] = _rope_chunk(z[:, D_MODEL + c * LANES:D_MODEL + (c + 1) * LANES], c64, s64, HEAD_DIM // 2)
    v_ref[...] = z[:, 2 * D_MODEL:]


def _l1_proj(x2, g, w_qkv, tables):
    n = x2.shape[0]
    tm = ROW_TILE
    row = pl.BlockSpec((tm, D_MODEL), lambda i: (i, 0))
    tab = pl.BlockSpec((tm, LANES), lambda i: (i, 0))
    out = jax.ShapeDtypeStruct((n, D_MODEL), F32)
    return pl.pallas_call(
        _l1_proj_kernel,
        grid=(n // tm,),
        in_specs=[row, _const_spec((1, D_MODEL)), _const_spec(w_qkv.shape), tab, tab],
        out_specs=[row] * 3,
        out_shape=[out] * 3,
        compiler_params=_cparams(("arbitrary",)),
        name="l1_proj",
    )(x2, g[None, :], w_qkv.astype(BF16), tables[0], tables[1])


def _banded_kernel(patterns, use_sink, span, *refs):
    refs = list(refs)
    sinks_ref = refs.pop(0) if use_sink else None
    q_ref, kp_ref, kc_ref, vp_ref, vc_ref, o_ref, kk, vv = refs[:8]
    stats = refs[8:]
    multi = len(patterns) > 1
    blk = pl.program_id(1)
    pair = pl.program_id(2)

    kk[0:span, :] = kp_ref[...]
    kk[span:2 * span, :] = kc_ref[...]
    vv[0:span, :] = vp_ref[...]
    vv[span:2 * span, :] = vc_ref[...]

    lo_q = lax.broadcasted_iota(jnp.int32, (BLOCK, LANES), 1) < HEAD_DIM
    lo_kv = lax.broadcasted_iota(jnp.int32, (2 * BLOCK, LANES), 1) < HEAD_DIM
    qi = lax.broadcasted_iota(jnp.int32, (2 * BLOCK, 2 * BLOCK), 0) & (BLOCK - 1)
    kj = lax.broadcasted_iota(jnp.int32, (2 * BLOCK, 2 * BLOCK), 1)
    dist = BLOCK + qi - kj
    in_cur = kj >= BLOCK
    if use_sink:
        top_rows = lax.broadcasted_iota(jnp.int32, (2 * BLOCK, 1), 0) < BLOCK
        sink2 = jnp.where(top_rows, sinks_ref[2 * pair], sinks_ref[2 * pair + 1])

    for pi, (dil, max_dist) in enumerate(patterns):
        per_stream = span // (BLOCK * dil)
        band = (dist >= 0) & (dist <= max_dist)

        def rows(start, count, dil=dil):
            if dil == 1:
                return pl.ds(pl.multiple_of(start, BLOCK), count)
            return pl.ds(start, count, stride=dil)

        def group(gi, carry, pi=pi, dil=dil, per_stream=per_stream, band=band, rows=rows):
            sub = range(BANDED_GROUP)
            idx = [gi * BANDED_GROUP + u for u in sub]
            j = [i % per_stream for i in idx]
            start = [i // per_stream + dil * BLOCK * jj for i, jj in zip(idx, j)]
            kstart = [span + st - dil * BLOCK for st in start]
            q = [q_ref[rows(st, BLOCK), :].astype(BF16) for st in start]
            k2 = [kk[rows(ks, 2 * BLOCK), :].astype(BF16) for ks in kstart]
            v2 = [vv[rows(ks, 2 * BLOCK), :].astype(BF16) for ks in kstart]
            q2 = [jnp.concatenate([jnp.where(lo_q, x, jnp.zeros_like(x)), jnp.where(lo_q, jnp.zeros_like(x), x)],
                                  axis=0) for x in q]
            s = [lax.dot_general(a, b, (((1,), (1,)), ((), ())), preferred_element_type=F32) for a, b in zip(q2, k2)]
            has_prev = [jnp.logical_or(blk > 0, jj > 0) for jj in j]
            s = [jnp.where(band & jnp.logical_or(in_cur, hp), x, NEG) for x, hp in zip(s, has_prev)]
            m = [jnp.max(x, axis=1, keepdims=True) for x in s]
            if use_sink:
                m = [jnp.maximum(x, sink2) for x in m]
            p = [jnp.exp(x - mm) for x, mm in zip(s, m)]
            l = [jnp.sum(x, axis=1, keepdims=True) for x in p]
            if use_sink:
                l = [x + jnp.exp(sink2 - mm) for x, mm in zip(l, m)]
            pb = [x.astype(BF16) for x in p]
            pcat = [jnp.concatenate([x[:BLOCK], x[BLOCK:]], axis=1) for x in pb]
            vcat = [jnp.concatenate([jnp.where(lo_kv, x, jnp.zeros_like(x)), jnp.where(lo_kv, jnp.zeros_like(x), x)],
                                    axis=0) for x in v2]
            acc = [jnp.dot(a, b, preferred_element_type=F32) for a, b in zip(pcat, vcat)]
            l_pair = [jnp.where(lo_q, x[:BLOCK], x[BLOCK:]) for x in l]
            for u in sub:
                dst = rows(start[u], BLOCK)
                if multi:
                    acc_s, m_s, l_s = stats[3 * pi:3 * pi + 3]
                    acc_s[dst, :] = acc[u]
                    m_s[dst, :] = jnp.where(lo_q, m[u][:BLOCK], m[u][BLOCK:])
                    l_s[dst, :] = l_pair[u]
                else:
                    o_ref[dst, :] = (acc[u] / l_pair[u]).astype(o_ref.dtype)
            return carry

        lax.fori_loop(0, span // (BLOCK * BANDED_GROUP), group, 0)

    if multi:
        chunk = 2 * BLOCK

        def merge(c, carry):
            rs = pl.ds(pl.multiple_of(c * chunk, chunk), chunk)
            m_all = [stats[3 * pi + 1][rs, :] for pi in range(len(patterns))]
            m_top = functools.reduce(jnp.maximum, m_all)
            num = jnp.zeros((chunk, LANES), F32)
            den = jnp.zeros((chunk, LANES), F32)
            for pi in range(len(patterns)):
                w = jnp.exp(m_all[pi] - m_top)
                num = num + w * stats[3 * pi][rs, :]
                den = den + w * stats[3 * pi + 2][rs, :]
            o_ref[rs, :] = (num / den).astype(o_ref.dtype)
            return carry

        lax.fori_loop(0, span // chunk, merge, 0)


def _banded_attention(q, k, v, batch, seq, n_pairs, kv_chunk_of_pair, patterns, sinks=None):
    n = q.shape[0]
    span = DIL_SPAN
    nblk = seq // span
    use_sink = sinks is not None

    def q_map(b, i, p):
        return (b * nblk + i, p)

    def prev_map(b, i, p):
        return (b * nblk + jnp.maximum(i - 1, 0), kv_chunk_of_pair(p))

    def cur_map(b, i, p):
        return (b * nblk + i, kv_chunk_of_pair(p))

    blk = (span, LANES)
    in_specs = [pl.BlockSpec(blk, q_map), pl.BlockSpec(blk, prev_map), pl.BlockSpec(blk, cur_map),
                pl.BlockSpec(blk, prev_map), pl.BlockSpec(blk, cur_map)]
    args = [q, k, k, v, v]
    if use_sink:
        in_specs = [pl.BlockSpec(memory_space=pltpu.SMEM)] + in_specs
        args = [sinks.astype(F32)] + args
    scratch = [pltpu.VMEM((2 * span, LANES), k.dtype), pltpu.VMEM((2 * span, LANES), v.dtype)]
    if len(patterns) > 1:
        scratch += [pltpu.VMEM((span, LANES), F32)] * (3 * len(patterns))
    return pl.pallas_call(
        functools.partial(_banded_kernel, patterns, use_sink, span),
        grid=(batch, nblk, n_pairs),
        in_specs=in_specs,
        out_specs=pl.BlockSpec(blk, q_map),
        out_shape=jax.ShapeDtypeStruct((n, n_pairs * LANES), BF16),
        scratch_shapes=scratch,
        compiler_params=_cparams(("arbitrary",) * 3),
        name="banded_attention_%d" % len(patterns),
    )(*args)


def _mla_kernel(tq, tk, q_ref, k_ref, v_ref, o_ref):
    qi = pl.program_id(2)
    lo_o = lax.broadcasted_iota(jnp.int32, (tq, LANES), 1) < MLA_V_DIM
    qs = (q_ref[:, 0:LANES], q_ref[:, LANES:2 * LANES])

    def step(ks, width, carry, masked):
        m0, l0, m1, l1, acc = carry
        kt = k_ref[pl.ds(ks, width), :]
        vt = v_ref[pl.ds(ks, width), :]
        lo_v = lax.broadcasted_iota(jnp.int32, (width, LANES), 1) < MLA_V_DIM
        new = []
        pv = None
        alphas = []
        for e, (m_old, l_old, sel) in enumerate(((m0, l0, lo_v), (m1, l1, ~lo_v))):
            s = lax.dot_general(qs[e], kt[:, e * LANES:(e + 1) * LANES], (((1,), (1,)), ((), ())),
                                preferred_element_type=F32)
            if masked:
                row_pos = qi * tq + lax.broadcasted_iota(jnp.int32, (tq, width), 0)
                col_pos = ks + lax.broadcasted_iota(jnp.int32, (tq, width), 1)
                s = jnp.where(row_pos >= col_pos, s, NEG)
            m_new = jnp.maximum(m_old, jnp.max(s, axis=1, keepdims=True))
            alpha = jnp.exp2(m_old - m_new)
            p = jnp.exp2(s - m_new)
            l_new = alpha * l_old + jnp.sum(p, axis=1, keepdims=True)
            ve = jnp.where(sel, vt, jnp.zeros_like(vt))
            o = jnp.dot(p.astype(BF16), ve, preferred_element_type=F32)
            pv = o if pv is None else pv + o
            new += [m_new, l_new]
            alphas.append(alpha)
        acc = acc * jnp.where(lo_o, alphas[0], alphas[1]) + pv
        return (new[0], new[1], new[2], new[3], acc)

    col = jnp.full((tq, 1), NEG, F32)
    zero = jnp.zeros((tq, 1), F32)
    carry = (col, zero, col, zero, jnp.zeros((tq, LANES), F32))
    q_start = qi * tq
    n_wide = q_start // tk
    carry = lax.fori_loop(0, n_wide, lambda kb, c: step(pl.multiple_of(kb * tk, tk), tk, c, False), carry)
    carry = lax.fori_loop(n_wide * (tk // tq), qi,
                          lambda kb, c: step(pl.multiple_of(kb * tq, tq), tq, c, False), carry)
    carry = step(pl.multiple_of(q_start, tq), tq, carry, True)
    _, l0, _, l1, acc = carry
    o_ref[...] = (acc / jnp.where(lo_o, l0, l1)).astype(o_ref.dtype)


def _mla_attention(qb, kb, vb, batch, seq):
    n = qb.shape[0]
    tq, tk = MLA_TQ, MLA_TK
    nq = seq // tq
    n_pairs = MLA_HEADS // 2
    return pl.pallas_call(
        functools.partial(_mla_kernel, tq, tk),
        grid=(batch, n_pairs, nq),
        in_specs=[pl.BlockSpec((tq, 2 * LANES), lambda b, p, i: (b * nq + i, p)),
                  pl.BlockSpec((seq, 2 * LANES), lambda b, p, i: (b, p)),
                  pl.BlockSpec((seq, LANES), lambda b, p, i: (b, p))],
        out_specs=pl.BlockSpec((tq, LANES), lambda b, p, i: (b * nq + i, p)),
        out_shape=jax.ShapeDtypeStruct((n, n_pairs * LANES), BF16),
        compiler_params=_cparams(("arbitrary",) * 3),
        name="mla_attention",
    )(qb, kb, vb)


def _outproj_kernel(n_in, *refs):
    x_ref = refs[0]
    a_refs = refs[1:1 + n_in]
    w_refs = refs[1 + n_in:1 + 2 * n_in]
    g_ref, wq_ref, x1_ref, xq_ref = refs[1 + 2 * n_in:]
    x1 = x_ref[...]
    for a_ref, w_ref in zip(a_refs, w_refs):
        x1 = x1 + jnp.dot(a_ref[...], w_ref[...], preferred_element_type=F32)
    x1_ref[...] = x1
    h = _rms(x1, g_ref[...]).astype(BF16)
    xq_ref[...] = jnp.dot(h, wq_ref[...], preferred_element_type=F32).astype(xq_ref.dtype)


def _outproj(x2, acts, weights, g_x, w_xq):
    n = x2.shape[0]
    tm = ROW_TILE
    xq_w = X_HEADS * X_HEAD_DIM

    def row(w):
        return pl.BlockSpec((tm, w), lambda i: (i, 0))

    weights = [w.astype(BF16) for w in weights]
    return pl.pallas_call(
        functools.partial(_outproj_kernel, len(acts)),
        grid=(n // tm,),
        in_specs=[row(D_MODEL)] + [row(a.shape[1]) for a in acts] + [_const_spec(w.shape) for w in weights]
        + [_const_spec((1, D_MODEL)), _const_spec((D_MODEL, xq_w))],
        out_specs=[row(D_MODEL), row(xq_w)],
        out_shape=[jax.ShapeDtypeStruct((n, D_MODEL), F32), jax.ShapeDtypeStruct((n, xq_w), BF16)],
        compiler_params=_cparams(("arbitrary",)),
        name="outproj",
    )(x2, *acts, *weights, g_x[None, :], w_xq.astype(BF16))


def _memkv_kernel(mem_ref, g_ref, w_ref, kv_ref):
    h = _rms(mem_ref[...], g_ref[...]).astype(BF16)
    kv_ref[...] = jnp.dot(h, w_ref[...], preferred_element_type=F32).astype(kv_ref.dtype)


def _memkv(mem2, g, w_xkv):
    n = mem2.shape[0]
    cols = 2 * X_HEADS * X_HEAD_DIM
    return pl.pallas_call(
        _memkv_kernel,
        grid=(n // MEM_LEN,),
        in_specs=[pl.BlockSpec((MEM_LEN, D_MODEL), lambda i: (i, 0)), _const_spec((1, D_MODEL)),
                  _const_spec((D_MODEL, cols))],
        out_specs=pl.BlockSpec((MEM_LEN, cols), lambda i: (i, 0)),
        out_shape=jax.ShapeDtypeStruct((n, cols), BF16),
        compiler_params=_cparams(("arbitrary",)),
        name="memkv",
    )(mem2, g[None, :], w_xkv.astype(BF16))


def _xattn_kernel(x_ref, q_ref, kv_ref, wo_ref, o_ref):
    scale = X_HEAD_DIM ** -0.5
    heads = []
    for hd in range(X_HEADS):
        q = q_ref[:, hd * X_HEAD_DIM:(hd + 1) * X_HEAD_DIM]
        k = kv_ref[:, hd * X_HEAD_DIM:(hd + 1) * X_HEAD_DIM]
        v = kv_ref[:, (X_HEADS + hd) * X_HEAD_DIM:(X_HEADS + hd + 1) * X_HEAD_DIM]
        s = lax.dot_general(q, k, (((1,), (1,)), ((), ())), preferred_element_type=F32) * scale
        m = jnp.max(s, axis=1, keepdims=True)
        p = jnp.exp(s - m)
        l = jnp.sum(p, axis=1, keepdims=True)
        o = jnp.dot(p.astype(BF16), v, preferred_element_type=F32) / l
        heads.append(o.astype(BF16))
    o_all = jnp.concatenate(heads, axis=1)
    o_ref[...] = x_ref[...] + jnp.dot(o_all, wo_ref[...], preferred_element_type=F32)


def _xattn(x2, xq, kv, w_xo, seq):
    n = x2.shape[0]
    tm = ROW_TILE
    per_batch = seq // tm
    xq_w = X_HEADS * X_HEAD_DIM
    return pl.pallas_call(
        _xattn_kernel,
        grid=(n // tm,),
        in_specs=[pl.BlockSpec((tm, D_MODEL), lambda i: (i, 0)), pl.BlockSpec((tm, xq_w), lambda i: (i, 0)),
                  pl.BlockSpec((MEM_LEN, 2 * xq_w), lambda i: (i // per_batch, 0)), _const_spec((xq_w, D_MODEL))],
        out_specs=pl.BlockSpec((tm, D_MODEL), lambda i: (i, 0)),
        out_shape=jax.ShapeDtypeStruct((n, D_MODEL), F32),
        compiler_params=_cparams(("arbitrary",)),
        name="xattn",
    )(x2, xq, kv, w_xo.astype(BF16))


def _ffn_kernel(final, x_ref, g_ref, wg_ref, wu_ref, wd_ref, *rest):
    if final:
        gf_ref, o_ref = rest
    else:
        (o_ref,) = rest
    x = x_ref[...]
    h = _rms(x, g_ref[...]).astype(BF16)
    width = FFN_HIDDEN // FFN_CHUNKS
    acc = x
    for c in range(FFN_CHUNKS):
        sl = slice(c * width, (c + 1) * width)
        gate = jnp.dot(h, wg_ref[:, sl], preferred_element_type=F32)
        up = jnp.dot(h, wu_ref[:, sl], preferred_element_type=F32)
        act = (gate * jax.nn.sigmoid(gate) * up).astype(BF16)
        acc = acc + jnp.dot(act, wd_ref[sl, :], preferred_element_type=F32)
    o_ref[...] = _rms(acc, gf_ref[...]) if final else acc


def _ffn(x2, g, w_gate, w_up, w_down, g_final=None):
    n = x2.shape[0]
    tm = ROW_TILE
    final = g_final is not None
    row = pl.BlockSpec((tm, D_MODEL), lambda i: (i, 0))

    def weight(shape):
        return pl.BlockSpec(shape, lambda i: (0, 0), pipeline_mode=pl.Buffered(1))

    in_specs = [row, _const_spec((1, D_MODEL)), weight((D_MODEL, FFN_HIDDEN)), weight((D_MODEL, FFN_HIDDEN)),
                weight((FFN_HIDDEN, D_MODEL))]
    args = [x2, g[None, :], w_gate.astype(BF16), w_up.astype(BF16), w_down.astype(BF16)]
    if final:
        in_specs.append(_const_spec((1, D_MODEL)))
        args.append(g_final[None, :])
    return pl.pallas_call(
        functools.partial(_ffn_kernel, final),
        grid=(n // tm,),
        in_specs=in_specs,
        out_specs=row,
        out_shape=jax.ShapeDtypeStruct((n, D_MODEL), F32),
        compiler_params=_cparams(("arbitrary",)),
        name="ffn",
    )(*args)


def kernel(x, mem, positions, l0_mix_norm, l0_w_in, l0_sinks, l0_q_norm, l0_w_uq, l0_kv_norm, l0_w_ukv, l0_w_out, l0_x_norm, l0_mem_norm, l0_w_xq, l0_w_xkv, l0_w_xo, l0_ffn_norm, l0_w_gate, l0_w_up, l0_w_down, l1_mix_norm, l1_w_qkv, l1_w_out, l1_x_norm, l1_mem_norm, l1_w_xq, l1_w_xkv, l1_w_xo, l1_ffn_norm, l1_w_gate, l1_w_up, l1_w_down, final_norm):
    batch, seq, _ = x.shape
    assert seq % DIL_SPAN == 0 and seq % MLA_TQ == 0 and seq % ROW_TILE == 0
    n = batch * seq
    x2 = x.reshape(n, D_MODEL)
    mem2 = mem.reshape(batch * MEM_LEN, D_MODEL)
    tables = _rope_tables(positions)

    qa, ka, va, qb, kb, vb = _l0_proj(x2, l0_mix_norm, l0_w_in, l0_q_norm, l0_w_uq, l0_kv_norm, l0_w_ukv, tables)
    oa = _banded_attention(qa, ka, va, batch, seq, SWA_HEADS // 2, lambda p: p // 2,
                           ((1, SWA_WINDOW - 1),), sinks=l0_sinks)
    ob = _mla_attention(qb, kb, vb, batch, seq)
    x2, xq = _outproj(x2, [oa, ob], [l0_w_out[:A_Q], l0_w_out[A_Q:]], l0_x_norm, l0_w_xq)
    x2 = _xattn(x2, xq, _memkv(mem2, l0_mem_norm, l0_w_xkv), l0_w_xo, seq)
    x2 = _ffn(x2, l0_ffn_norm, l0_w_gate, l0_w_up, l0_w_down)

    q, k, v = _l1_proj(x2, l1_mix_norm, l1_w_qkv, tables)
    od = _banded_attention(q, k, v, batch, seq, DIL_HEADS // 2, lambda p: p,
                           tuple((dil, window // dil) for window, dil in DIL_PATTERNS))
    x2, xq = _outproj(x2, [od], [l1_w_out], l1_x_norm, l1_w_xq)
    x2 = _xattn(x2, xq, _memkv(mem2, l1_mem_norm, l1_w_xkv), l1_w_xo, seq)
    x2 = _ffn(x2, l1_ffn_norm, l1_w_gate, l1_w_up, l1_w_down, g_final=final_norm)
    return x2.reshape(batch, seq, D_MODEL)
```

```python
import functools

import jax
import jax.numpy as jnp
from jax import lax
from jax.experimental import pallas as pl
from jax.experimental.pallas import tpu as pltpu

D_MODEL = 1024
MEM_LEN = 256
HEAD_DIM = 64
ROPE_THETA = 10000.0
NORM_EPS = 1e-6
BLOCK = 128
SWA_HEADS = 8
SWA_KV_HEADS = 2
SWA_WINDOW = 128
MLA_HEADS = 8
MLA_Q_RANK = 384
MLA_KV_RANK = 256
MLA_NOPE_DIM = 64
MLA_ROPE_DIM = 32
MLA_V_DIM = 64
A_Q = SWA_HEADS * HEAD_DIM
A_KV = SWA_KV_HEADS * HEAD_DIM
DIL_HEADS = D_MODEL // HEAD_DIM
DIL_PATTERNS = ((128, 1), (512, 4), (2048, 16))
X_HEADS = 4
X_HEAD_DIM = 128
FFN_HIDDEN = -(-8 * D_MODEL // (3 * 256)) * 256

LANES = 128
V7X_VMEM_LIMIT = 56 * 1024 * 1024
NEG = -1e30
LOG2E = 1.4426950408889634

ROW_TILE = 512
DIL_SPAN = 2048
STAGE_DIL = 4
BANDED_GROUP = 4
MLA_TQ = 512
MLA_TK = 2048
MLA_SUBSTEPS = 4
FFN_CHUNKS = 2

F32 = jnp.float32
BF16 = jnp.bfloat16


def _cparams(sem, flags=None):
    return pltpu.CompilerParams(dimension_semantics=sem, vmem_limit_bytes=V7X_VMEM_LIMIT, flags=flags)


def _const_spec(shape):
    return pl.BlockSpec(shape, lambda *_: (0,) * len(shape))


def _rms(x, g):
    return x * lax.rsqrt(jnp.mean(x * x, axis=-1, keepdims=True) + NORM_EPS) * g


def _rope_chunk(xc, c, s, half):
    lane = lax.broadcasted_iota(jnp.int32, xc.shape, 1)
    up = pltpu.roll(xc, half, 1)
    down = pltpu.roll(xc, LANES - half, 1)
    partner = jnp.where((lane & (2 * half - 1)) < half, down, up)
    return xc * c + partner * s


def _tables_kernel(pos_ref, f64_ref, sg64_ref, f32_ref, sg32_ref, on32_ref, c64_ref, s64_ref, c32_ref, s32_ref):
    pos = pos_ref[...].astype(F32)
    a64 = pos * f64_ref[...]
    c64_ref[...] = jnp.cos(a64)
    s64_ref[...] = jnp.sin(a64) * sg64_ref[...]
    a32 = pos * f32_ref[...]
    on = on32_ref[...] > 0.5
    c32_ref[...] = jnp.where(on, jnp.cos(a32), 1.0)
    s32_ref[...] = jnp.where(on, jnp.sin(a32) * sg32_ref[...], 0.0)


def _rope_tables(positions):
    n = positions.size
    tm = 2048
    lane = jnp.arange(LANES)
    inv64 = ROPE_THETA ** (-jnp.arange(0, HEAD_DIM, 2, dtype=F32) / HEAD_DIM)
    inv32 = ROPE_THETA ** (-jnp.arange(0, MLA_ROPE_DIM, 2, dtype=F32) / MLA_ROPE_DIM)
    f64 = inv64[lane % (HEAD_DIM // 2)][None, :]
    sg64 = jnp.where(lane % HEAD_DIM < HEAD_DIM // 2, -1.0, 1.0).astype(F32)[None, :]
    rl = lane - MLA_NOPE_DIM
    on32 = ((rl >= 0) & (rl < MLA_ROPE_DIM)).astype(F32)[None, :]
    f32v = inv32[jnp.clip(rl, 0, MLA_ROPE_DIM - 1) % (MLA_ROPE_DIM // 2)][None, :]
    sg32 = jnp.where(rl % MLA_ROPE_DIM < MLA_ROPE_DIM // 2, -1.0, 1.0).astype(F32)[None, :]
    row = pl.BlockSpec((tm, LANES), lambda i: (i, 0))
    out = jax.ShapeDtypeStruct((n, LANES), F32)
    return pl.pallas_call(
        _tables_kernel,
        grid=(n // tm,),
        in_specs=[pl.BlockSpec((tm, 1), lambda i: (i, 0))] + [_const_spec((1, LANES))] * 5,
        out_specs=[row] * 4,
        out_shape=[out] * 4,
        compiler_params=_cparams(("arbitrary",)),
        name="rope_tables",
    )(positions.reshape(n, 1), f64, sg64, f32v, sg32, on32)


L0_QA = 0
L0_KA = A_Q
L0_VA = L0_KA + 2 * A_KV
L0_CQ = L0_VA + 2 * A_KV
L0_CKV = L0_CQ + MLA_Q_RANK
L0_KR = L0_CKV + MLA_KV_RANK
L0_COLS = L0_KR + LANES
MLA_QK = MLA_HEADS * LANES
MLA_Q_LOG2_SCALE = (MLA_NOPE_DIM + MLA_ROPE_DIM) ** -0.5 * LOG2E
MLA_V = MLA_HEADS * MLA_V_DIM


def _l0_proj_kernel(x_ref, g_ref, win_ref, gq_ref, wuq_ref, gkv_ref, wukv_ref, c64_ref, s64_ref, c32_ref, s32_ref,
                    qa_ref, ka_ref, va_ref, qb_ref, kb_ref, vb_ref):
    h = _rms(x_ref[...], g_ref[...]).astype(BF16)
    z = jnp.dot(h, win_ref[...], preferred_element_type=F32)
    c64, s64, c32, s32 = c64_ref[...], s64_ref[...], c32_ref[...], s32_ref[...]
    qscale = HEAD_DIM ** -0.5 * LOG2E
    for c in range(A_Q // LANES):
        sl = slice(c * LANES, (c + 1) * LANES)
        qa_ref[:, sl] = (_rope_chunk(z[:, sl], c64, s64, HEAD_DIM // 2) * qscale).astype(qa_ref.dtype)
    for c in range(2 * A_KV // LANES):
        sl = slice(c * LANES, (c + 1) * LANES)
        ka_ref[:, sl] = _rope_chunk(z[:, L0_KA + c * LANES:L0_KA + (c + 1) * LANES], c64, s64,
                                    HEAD_DIM // 2).astype(ka_ref.dtype)
    va_ref[...] = z[:, L0_VA:L0_CQ].astype(va_ref.dtype)

    cq = _rms(z[:, L0_CQ:L0_CKV], gq_ref[...]).astype(BF16)
    qb = jnp.dot(cq, wuq_ref[...], preferred_element_type=F32)
    ckv = _rms(z[:, L0_CKV:L0_KR], gkv_ref[...]).astype(BF16)
    kv = jnp.dot(ckv, wukv_ref[...], preferred_element_type=F32)
    kr = _rope_chunk(z[:, L0_KR:L0_COLS], c32, s32, MLA_ROPE_DIM // 2)
    for hd in range(MLA_HEADS):
        sl = slice(hd * LANES, (hd + 1) * LANES)
        qb_ref[:, sl] = (_rope_chunk(qb[:, sl], c32, s32, MLA_ROPE_DIM // 2) * MLA_Q_LOG2_SCALE).astype(qb_ref.dtype)
        kb_ref[:, sl] = (kv[:, sl] + kr).astype(kb_ref.dtype)
    vb_ref[...] = kv[:, MLA_QK:].astype(vb_ref.dtype)


def _l0_proj(x2, g, w_in, g_q, w_uq, g_kv, w_ukv, tables):
    n = x2.shape[0]
    tm = ROW_TILE
    qa_w, ka_w, va_w, cq_w, ckv_w, kr_w = jnp.split(
        w_in, [A_Q, A_Q + A_KV, A_Q + 2 * A_KV, A_Q + 2 * A_KV + MLA_Q_RANK,
               A_Q + 2 * A_KV + MLA_Q_RANK + MLA_KV_RANK], axis=1)

    def dup(w):
        return jnp.repeat(w.reshape(D_MODEL, SWA_KV_HEADS, 1, HEAD_DIM), 2, axis=2).reshape(D_MODEL, 2 * A_KV)

    kr_pad = jnp.pad(kr_w, ((0, 0), (MLA_NOPE_DIM, LANES - MLA_NOPE_DIM - MLA_ROPE_DIM)))
    win_p = jnp.concatenate([qa_w, dup(ka_w), dup(va_w), cq_w, ckv_w, kr_pad], axis=1).astype(BF16)
    per_head_q = MLA_NOPE_DIM + MLA_ROPE_DIM
    wuq_p = jnp.pad(w_uq.reshape(MLA_Q_RANK, MLA_HEADS, per_head_q),
                    ((0, 0), (0, 0), (0, LANES - per_head_q))).reshape(MLA_Q_RANK, MLA_QK).astype(BF16)
    wukv3 = w_ukv.reshape(MLA_KV_RANK, MLA_HEADS, MLA_NOPE_DIM + MLA_V_DIM)
    wk_p = jnp.pad(wukv3[:, :, :MLA_NOPE_DIM], ((0, 0), (0, 0), (0, LANES - MLA_NOPE_DIM))).reshape(MLA_KV_RANK, MLA_QK)
    wv_p = wukv3[:, :, MLA_NOPE_DIM:].reshape(MLA_KV_RANK, MLA_V)
    wukv_p = jnp.concatenate([wk_p, wv_p], axis=1).astype(BF16)

    def row(w):
        return pl.BlockSpec((tm, w), lambda i: (i, 0))

    outs = [(A_Q, BF16), (2 * A_KV, BF16), (2 * A_KV, BF16), (MLA_QK, BF16), (MLA_QK, BF16), (MLA_V, BF16)]
    return pl.pallas_call(
        _l0_proj_kernel,
        grid=(n // tm,),
        in_specs=[row(D_MODEL), _const_spec((1, D_MODEL)), _const_spec(win_p.shape),
                  _const_spec((1, MLA_Q_RANK)), _const_spec(wuq_p.shape),
                  _const_spec((1, MLA_KV_RANK)), _const_spec(wukv_p.shape)] + [row(LANES)] * 4,
        out_specs=[row(w) for w, _ in outs],
        out_shape=[jax.ShapeDtypeStruct((n, w), dt) for w, dt in outs],
        compiler_params=_cparams(("arbitrary",)),
        name="l0_proj",
    )(x2, g[None, :], win_p, g_q[None, :], wuq_p, g_kv[None, :], wukv_p, *tables)


def _l1_proj_kernel(x_ref, g_ref, w_ref, c64_ref, s64_ref, q_ref, k_ref, v_ref):
    h = _rms(x_ref[...], g_ref[...]).astype(BF16)
    z = jnp.dot(h, w_ref[...], preferred_element_type=F32)
    c64, s64 = c64_ref[...], s64_ref[...]
    qscale = HEAD_DIM ** -0.5 * LOG2E
    for c in range(D_MODEL // LANES):
        sl = slice(c * LANES, (c + 1) * LANES)
        q_ref[:, sl] = _rope_chunk(z[:, sl], c64, s64, HEAD_DIM // 2) * qscale
        k_ref[:, sl] = _rope_chunk(z[:, D_MODEL + c * LANES:D_MODEL + (c + 1) * LANES], c64, s64, HEAD_DIM // 2)
    v_ref[...] = z[:, 2 * D_MODEL:]


def _l1_proj(x2, g, w_qkv, tables):
    n = x2.shape[0]
    tm = ROW_TILE
    row = pl.BlockSpec((tm, D_MODEL), lambda i: (i, 0))
    tab = pl.BlockSpec((tm, LANES), lambda i: (i, 0))
    out = jax.ShapeDtypeStruct((n, D_MODEL), F32)
    return pl.pallas_call(
        _l1_proj_kernel,
        grid=(n // tm,),
        in_specs=[row, _const_spec((1, D_MODEL)), _const_spec(w_qkv.shape), tab, tab],
        out_specs=[row] * 3,
        out_shape=[out] * 3,
        compiler_params=_cparams(("arbitrary",)),
        name="l1_proj",
    )(x2, g[None, :], w_qkv.astype(BF16), tables[0], tables[1])


def _banded_kernel(patterns, use_sink, span, *refs):
    refs = list(refs)
    sinks_ref = refs.pop(0) if use_sink else None
    q_ref, k_ref, v_ref, o_ref = refs[:4]
    kv_scr = refs[4:4 + 3 * len(patterns)]
    rest = refs[4 + 3 * len(patterns):]
    staged = any(dil > 1 for dil, _ in patterns)
    k4_s, v4_s = rest[:2] if staged else (None, None)
    stats = rest[2:] if staged else rest
    multi = len(patterns) > 1
    pair = pl.program_id(1)
    blk = pl.program_id(2)

    for first in (True, False):
        @pl.when(blk == 0 if first else blk > 0)
        def _(first=first):
            for pi, (dil, _) in enumerate(patterns):
                stream_len = BLOCK + span // dil
                for r in range(dil):
                    head = slice(r * stream_len, r * stream_len + BLOCK)
                    tail = slice((r + 1) * stream_len - BLOCK, (r + 1) * stream_len)
                    for ref in kv_scr[3 * pi:3 * pi + 3]:
                        ref[head, :] = jnp.zeros((BLOCK, LANES), BF16) if first else ref[tail, :]

    def put(pi, dst, count, k_rows, v_rows):
        k_s, vlo_s, vhi_s = kv_scr[3 * pi:3 * pi + 3]
        k_s[dst:dst + count, :] = k_rows.astype(BF16)
        v_rows = v_rows.astype(BF16)
        lo = lax.broadcasted_iota(jnp.int32, (count, LANES), 1) < HEAD_DIM
        vlo_s[dst:dst + count, :] = jnp.where(lo, v_rows, jnp.zeros_like(v_rows))
        vhi_s[dst:dst + count, :] = jnp.where(lo, jnp.zeros_like(v_rows), v_rows)

    quarter = span // STAGE_DIL
    if staged:
        for c in range(STAGE_DIL):
            k4_s[c * quarter:(c + 1) * quarter, :] = k_ref[pl.ds(c, quarter, stride=STAGE_DIL), :]
            v4_s[c * quarter:(c + 1) * quarter, :] = v_ref[pl.ds(c, quarter, stride=STAGE_DIL), :]
    for pi, (dil, _) in enumerate(patterns):
        cur_len = span // dil
        stream_len = BLOCK + cur_len
        if dil == 1:
            put(pi, BLOCK, span, k_ref[...], v_ref[...])
        elif dil == STAGE_DIL:
            for c in range(dil):
                src = slice(c * quarter, (c + 1) * quarter)
                put(pi, c * stream_len + BLOCK, cur_len, k4_s[src, :], v4_s[src, :])
        else:
            assert dil == STAGE_DIL * STAGE_DIL
            for r in range(dil):
                src = pl.ds((r % STAGE_DIL) * quarter + r // STAGE_DIL, cur_len, stride=STAGE_DIL)
                put(pi, r * stream_len + BLOCK, cur_len, k4_s[src, :], v4_s[src, :])

    lo_q = lax.broadcasted_iota(jnp.int32, (BLOCK, LANES), 1) < HEAD_DIM
    qi = lax.broadcasted_iota(jnp.int32, (2 * BLOCK, 2 * BLOCK), 0) & (BLOCK - 1)
    kj = lax.broadcasted_iota(jnp.int32, (2 * BLOCK, 2 * BLOCK), 1)
    dist = BLOCK + qi - kj
    in_cur = kj >= BLOCK
    if use_sink:
        top_rows = lax.broadcasted_iota(jnp.int32, (2 * BLOCK, 1), 0) < BLOCK
        sink2 = jnp.where(top_rows, sinks_ref[2 * pair], sinks_ref[2 * pair + 1]) * LOG2E

    for pi, (dil, max_dist) in enumerate(patterns):
        per_stream = span // (BLOCK * dil)
        stream_len = BLOCK + span // dil
        band = (dist >= 0) & (dist <= max_dist)
        bias_full = jnp.where(band, 0.0, NEG)
        bias_first = jnp.where(band & in_cur, 0.0, NEG)
        k_s, vlo_s, vhi_s = kv_scr[3 * pi:3 * pi + 3]

        def rows(start, count, dil=dil):
            if dil == 1:
                return pl.ds(pl.multiple_of(start, BLOCK), count)
            return pl.ds(start, count, stride=dil)

        def group(gi, carry, pi=pi, dil=dil, per_stream=per_stream, stream_len=stream_len, rows=rows,
                  bias_full=bias_full, bias_first=bias_first, k_s=k_s, vlo_s=vlo_s, vhi_s=vhi_s):
            sub = range(BANDED_GROUP)
            idx = [gi * BANDED_GROUP + u for u in sub]
            j = [i % per_stream for i in idx]
            stream = [i // per_stream for i in idx]
            start = [r + dil * BLOCK * jj for r, jj in zip(stream, j)]
            krows = [pl.ds(pl.multiple_of(r * stream_len + BLOCK * jj, BLOCK), 2 * BLOCK) for r, jj in zip(stream, j)]
            q = [q_ref[rows(st, BLOCK), :].astype(BF16) for st in start]
            q2 = [jnp.concatenate([jnp.where(lo_q, x, jnp.zeros_like(x)), jnp.where(lo_q, jnp.zeros_like(x), x)],
                                  axis=0) for x in q]
            s = [lax.dot_general(a, k_s[kr, :], (((1,), (1,)), ((), ())), preferred_element_type=F32)
                 for a, kr in zip(q2, krows)]
            has_prev = [jnp.logical_or(blk > 0, jj > 0) for jj in j]
            s = [x + jnp.where(hp, bias_full, bias_first) for x, hp in zip(s, has_prev)]
            m = [jnp.max(x, axis=1, keepdims=True) for x in s]
            if use_sink:
                m = [jnp.maximum(x, sink2) for x in m]
            p = [jnp.exp2(x - mm) for x, mm in zip(s, m)]
            l = [jnp.sum(x, axis=1, keepdims=True) for x in p]
            if use_sink:
                l = [x + jnp.exp2(sink2 - mm) for x, mm in zip(l, m)]
            pb = [x.astype(BF16) for x in p]
            pcat = [jnp.concatenate([x[:BLOCK], x[BLOCK:]], axis=1) for x in pb]
            vcat = [jnp.concatenate([vlo_s[kr, :], vhi_s[kr, :]], axis=0) for kr in krows]
            acc = [jnp.dot(a, b, preferred_element_type=F32) for a, b in zip(pcat, vcat)]
            l_pair = [jnp.where(lo_q, x[:BLOCK], x[BLOCK:]) for x in l]
            for u in sub:
                dst = rows(start[u], BLOCK)
                if multi:
                    acc_s, m_s, l_s = stats[3 * pi:3 * pi + 3]
                    acc_s[dst, :] = acc[u]
                    m_s[dst, :] = jnp.where(lo_q, m[u][:BLOCK], m[u][BLOCK:])
                    l_s[dst, :] = l_pair[u]
                else:
                    o_ref[dst, :] = (acc[u] / l_pair[u]).astype(o_ref.dtype)
            return carry

        lax.fori_loop(0, span // (BLOCK * BANDED_GROUP), group, 0)

    if multi:
        chunk = 2 * BLOCK

        def merge(c, carry):
            rs = pl.ds(pl.multiple_of(c * chunk, chunk), chunk)
            m_all = [stats[3 * pi + 1][rs, :] for pi in range(len(patterns))]
            m_top = functools.reduce(jnp.maximum, m_all)
            num = jnp.zeros((chunk, LANES), F32)
            den = jnp.zeros((chunk, LANES), F32)
            for pi in range(len(patterns)):
                w = jnp.exp2(m_all[pi] - m_top)
                num = num + w * stats[3 * pi][rs, :]
                den = den + w * stats[3 * pi + 2][rs, :]
            o_ref[rs, :] = (num / den).astype(o_ref.dtype)
            return carry

        lax.fori_loop(0, span // chunk, merge, 0)


def _banded_attention(q, k, v, batch, seq, n_pairs, kv_chunk_of_pair, patterns, sinks=None):
    n = q.shape[0]
    span = DIL_SPAN
    nblk = seq // span
    use_sink = sinks is not None

    def q_map(b, p, i):
        return (b * nblk + i, p)

    def kv_map(b, p, i):
        return (b * nblk + i, kv_chunk_of_pair(p))

    blk = (span, LANES)
    in_specs = [pl.BlockSpec(blk, q_map), pl.BlockSpec(blk, kv_map), pl.BlockSpec(blk, kv_map)]
    args = [q, k, v]
    if use_sink:
        in_specs = [pl.BlockSpec(memory_space=pltpu.SMEM)] + in_specs
        args = [sinks.astype(F32)] + args
    scratch = []
    for dil, _ in patterns:
        scratch += [pltpu.VMEM((dil * BLOCK + span, LANES), BF16)] * 3
    if any(dil > 1 for dil, _ in patterns):
        scratch += [pltpu.VMEM((span, LANES), k.dtype), pltpu.VMEM((span, LANES), v.dtype)]
    if len(patterns) > 1:
        scratch += [pltpu.VMEM((span, LANES), F32)] * (3 * len(patterns))
    return pl.pallas_call(
        functools.partial(_banded_kernel, patterns, use_sink, span),
        grid=(batch, n_pairs, nblk),
        in_specs=in_specs,
        out_specs=pl.BlockSpec(blk, q_map),
        out_shape=jax.ShapeDtypeStruct((n, n_pairs * LANES), BF16),
        scratch_shapes=scratch,
        compiler_params=_cparams(("arbitrary",) * 3),
        name="banded_attention_%d" % len(patterns),
    )(*args)


def _mla_kernel(tq, tk, q_ref, k_ref, v_ref, o_ref):
    qi = pl.program_id(2)
    lo_o = lax.broadcasted_iota(jnp.int32, (tq, LANES), 1) < MLA_V_DIM
    qs = (q_ref[:, 0:LANES], q_ref[:, LANES:2 * LANES])

    def step(ks, width, carry, masked):
        m0, l0, m1, l1, acc = carry
        kt = k_ref[pl.ds(ks, width), :]
        vt = v_ref[pl.ds(ks, width), :]
        lo_v = lax.broadcasted_iota(jnp.int32, (width, LANES), 1) < MLA_V_DIM
        new = []
        pv = None
        alphas = []
        for e, (m_old, l_old, sel) in enumerate(((m0, l0, lo_v), (m1, l1, ~lo_v))):
            s = lax.dot_general(qs[e], kt[:, e * LANES:(e + 1) * LANES], (((1,), (1,)), ((), ())),
                                preferred_element_type=F32)
            if masked:
                row_pos = qi * tq + lax.broadcasted_iota(jnp.int32, (tq, width), 0)
                col_pos = ks + lax.broadcasted_iota(jnp.int32, (tq, width), 1)
                s = jnp.where(row_pos >= col_pos, s, NEG)
            m_new = jnp.maximum(m_old, jnp.max(s, axis=1, keepdims=True))
            alpha = jnp.exp2(m_old - m_new)
            p = jnp.exp2(s - m_new)
            l_new = alpha * l_old + jnp.sum(p, axis=1, keepdims=True)
            ve = jnp.where(sel, vt, jnp.zeros_like(vt))
            o = jnp.dot(p.astype(BF16), ve, preferred_element_type=F32)
            pv = o if pv is None else pv + o
            new += [m_new, l_new]
            alphas.append(alpha)
        acc = acc * jnp.where(lo_o, alphas[0], alphas[1]) + pv
        return (new[0], new[1], new[2], new[3], acc)

    col = jnp.full((tq, 1), NEG, F32)
    zero = jnp.zeros((tq, 1), F32)
    carry = (col, zero, col, zero, jnp.zeros((tq, LANES), F32))
    q_start = qi * tq
    n_wide = q_start // tk
    sub = tk // MLA_SUBSTEPS

    def wide(kb, c):
        for u in range(MLA_SUBSTEPS):
            c = step(pl.multiple_of(kb * tk + u * sub, sub), sub, c, False)
        return c

    carry = lax.fori_loop(0, n_wide, wide, carry)
    carry = lax.fori_loop(n_wide * (tk // tq), qi,
                          lambda kb, c: step(pl.multiple_of(kb * tq, tq), tq, c, False), carry)
    carry = step(pl.multiple_of(q_start, tq), tq, carry, True)
    _, l0, _, l1, acc = carry
    o_ref[...] = (acc / jnp.where(lo_o, l0, l1)).astype(o_ref.dtype)


def _mla_attention(qb, kb, vb, batch, seq):
    n = qb.shape[0]
    tq, tk = MLA_TQ, MLA_TK
    nq = seq // tq
    n_pairs = MLA_HEADS // 2
    return pl.pallas_call(
        functools.partial(_mla_kernel, tq, tk),
        grid=(batch, n_pairs, nq),
        in_specs=[pl.BlockSpec((tq, 2 * LANES), lambda b, p, i: (b * nq + i, p)),
                  pl.BlockSpec((seq, 2 * LANES), lambda b, p, i: (b, p)),
                  pl.BlockSpec((seq, LANES), lambda b, p, i: (b, p))],
        out_specs=pl.BlockSpec((tq, LANES), lambda b, p, i: (b * nq + i, p)),
        out_shape=jax.ShapeDtypeStruct((n, n_pairs * LANES), BF16),
        compiler_params=_cparams(("arbitrary",) * 3),
        name="mla_attention",
    )(qb, kb, vb)


def _outproj_kernel(n_in, *refs):
    x_ref = refs[0]
    a_refs = refs[1:1 + n_in]
    w_refs = refs[1 + n_in:1 + 2 * n_in]
    g_ref, wq_ref, x1_ref, xq_ref = refs[1 + 2 * n_in:]
    x1 = x_ref[...]
    for a_ref, w_ref in zip(a_refs, w_refs):
        x1 = x1 + jnp.dot(a_ref[...], w_ref[...], preferred_element_type=F32)
    x1_ref[...] = x1
    h = _rms(x1, g_ref[...]).astype(BF16)
    xq_ref[...] = jnp.dot(h, wq_ref[...], preferred_element_type=F32).astype(xq_ref.dtype)


def _outproj(x2, acts, weights, g_x, w_xq):
    n = x2.shape[0]
    tm = ROW_TILE
    xq_w = X_HEADS * X_HEAD_DIM

    def row(w):
        return pl.BlockSpec((tm, w), lambda i: (i, 0))

    weights = [w.astype(BF16) for w in weights]
    return pl.pallas_call(
        functools.partial(_outproj_kernel, len(acts)),
        grid=(n // tm,),
        in_specs=[row(D_MODEL)] + [row(a.shape[1]) for a in acts] + [_const_spec(w.shape) for w in weights]
        + [_const_spec((1, D_MODEL)), _const_spec((D_MODEL, xq_w))],
        out_specs=[row(D_MODEL), row(xq_w)],
        out_shape=[jax.ShapeDtypeStruct((n, D_MODEL), F32), jax.ShapeDtypeStruct((n, xq_w), BF16)],
        compiler_params=_cparams(("arbitrary",)),
        name="outproj",
    )(x2, *acts, *weights, g_x[None, :], w_xq.astype(BF16))


def _memkv_kernel(mem_ref, g_ref, w_ref, kv_ref):
    h = _rms(mem_ref[...], g_ref[...]).astype(BF16)
    kv_ref[...] = jnp.dot(h, w_ref[...], preferred_element_type=F32).astype(kv_ref.dtype)


def _memkv(mem2, g, w_xkv):
    n = mem2.shape[0]
    cols = 2 * X_HEADS * X_HEAD_DIM
    return pl.pallas_call(
        _memkv_kernel,
        grid=(n // MEM_LEN,),
        in_specs=[pl.BlockSpec((MEM_LEN, D_MODEL), lambda i: (i, 0)), _const_spec((1, D_MODEL)),
                  _const_spec((D_MODEL, cols))],
        out_specs=pl.BlockSpec((MEM_LEN, cols), lambda i: (i, 0)),
        out_shape=jax.ShapeDtypeStruct((n, cols), BF16),
        compiler_params=_cparams(("arbitrary",)),
        name="memkv",
    )(mem2, g[None, :], w_xkv.astype(BF16))


def _xattn_kernel(x_ref, q_ref, kv_ref, wo_ref, o_ref):
    scale = X_HEAD_DIM ** -0.5
    heads = []
    for hd in range(X_HEADS):
        q = q_ref[:, hd * X_HEAD_DIM:(hd + 1) * X_HEAD_DIM]
        k = kv_ref[:, hd * X_HEAD_DIM:(hd + 1) * X_HEAD_DIM]
        v = kv_ref[:, (X_HEADS + hd) * X_HEAD_DIM:(X_HEADS + hd + 1) * X_HEAD_DIM]
        s = lax.dot_general(q, k, (((1,), (1,)), ((), ())), preferred_element_type=F32) * scale
        m = jnp.max(s, axis=1, keepdims=True)
        p = jnp.exp(s - m)
        l = jnp.sum(p, axis=1, keepdims=True)
        o = jnp.dot(p.astype(BF16), v, preferred_element_type=F32) / l
        heads.append(o.astype(BF16))
    o_all = jnp.concatenate(heads, axis=1)
    o_ref[...] = x_ref[...] + jnp.dot(o_all, wo_ref[...], preferred_element_type=F32)


def _xattn(x2, xq, kv, w_xo, seq):
    n = x2.shape[0]
    tm = ROW_TILE
    per_batch = seq // tm
    xq_w = X_HEADS * X_HEAD_DIM
    return pl.pallas_call(
        _xattn_kernel,
        grid=(n // tm,),
        in_specs=[pl.BlockSpec((tm, D_MODEL), lambda i: (i, 0)), pl.BlockSpec((tm, xq_w), lambda i: (i, 0)),
                  pl.BlockSpec((MEM_LEN, 2 * xq_w), lambda i: (i // per_batch, 0)), _const_spec((xq_w, D_MODEL))],
        out_specs=pl.BlockSpec((tm, D_MODEL), lambda i: (i, 0)),
        out_shape=jax.ShapeDtypeStruct((n, D_MODEL), F32),
        compiler_params=_cparams(("arbitrary",)),
        name="xattn",
    )(x2, xq, kv, w_xo.astype(BF16))


def _ffn_kernel(final, x_ref, g_ref, wg_ref, wu_ref, wd_ref, *rest):
    if final:
        gf_ref, o_ref = rest
    else:
        (o_ref,) = rest
    x = x_ref[...]
    h = _rms(x, g_ref[...]).astype(BF16)
    width = FFN_HIDDEN // FFN_CHUNKS
    acc = x
    for c in range(FFN_CHUNKS):
        sl = slice(c * width, (c + 1) * width)
        gate = jnp.dot(h, wg_ref[:, sl], preferred_element_type=F32)
        up = jnp.dot(h, wu_ref[:, sl], preferred_element_type=F32)
        act = (gate * jax.nn.sigmoid(gate) * up).astype(BF16)
        acc = acc + jnp.dot(act, wd_ref[sl, :], preferred_element_type=F32)
    o_ref[...] = _rms(acc, gf_ref[...]) if final else acc


def _ffn(x2, g, w_gate, w_up, w_down, g_final=None):
    n = x2.shape[0]
    tm = ROW_TILE
    final = g_final is not None
    row = pl.BlockSpec((tm, D_MODEL), lambda i: (i, 0))

    def weight(shape):
        return pl.BlockSpec(shape, lambda i: (0, 0), pipeline_mode=pl.Buffered(1))

    in_specs = [row, _const_spec((1, D_MODEL)), weight((D_MODEL, FFN_HIDDEN)), weight((D_MODEL, FFN_HIDDEN)),
                weight((FFN_HIDDEN, D_MODEL))]
    args = [x2, g[None, :], w_gate.astype(BF16), w_up.astype(BF16), w_down.astype(BF16)]
    if final:
        in_specs.append(_const_spec((1, D_MODEL)))
        args.append(g_final[None, :])
    return pl.pallas_call(
        functools.partial(_ffn_kernel, final),
        grid=(n // tm,),
        in_specs=in_specs,
        out_specs=row,
        out_shape=jax.ShapeDtypeStruct((n, D_MODEL), F32),
        compiler_params=_cparams(("arbitrary",)),
        name="ffn",
    )(*args)


def kernel(x, mem, positions, l0_mix_norm, l0_w_in, l0_sinks, l0_q_norm, l0_w_uq, l0_kv_norm, l0_w_ukv, l0_w_out, l0_x_norm, l0_mem_norm, l0_w_xq, l0_w_xkv, l0_w_xo, l0_ffn_norm, l0_w_gate, l0_w_up, l0_w_down, l1_mix_norm, l1_w_qkv, l1_w_out, l1_x_norm, l1_mem_norm, l1_w_xq, l1_w_xkv, l1_w_xo, l1_ffn_norm, l1_w_gate, l1_w_up, l1_w_down, final_norm):
    batch, seq, _ = x.shape
    assert seq % DIL_SPAN == 0 and seq % MLA_TQ == 0 and seq % ROW_TILE == 0
    n = batch * seq
    x2 = x.reshape(n, D_MODEL)
    mem2 = mem.reshape(batch * MEM_LEN, D_MODEL)
    tables = _rope_tables(positions)

    qa, ka, va, qb, kb, vb = _l0_proj(x2, l0_mix_norm, l0_w_in, l0_q_norm, l0_w_uq, l0_kv_norm, l0_w_ukv, tables)
    oa = _banded_attention(qa, ka, va, batch, seq, SWA_HEADS // 2, lambda p: p // 2,
                           ((1, SWA_WINDOW - 1),), sinks=l0_sinks)
    ob = _mla_attention(qb, kb, vb, batch, seq)
    x2, xq = _outproj(x2, [oa, ob], [l0_w_out[:A_Q], l0_w_out[A_Q:]], l0_x_norm, l0_w_xq)
    x2 = _xattn(x2, xq, _memkv(mem2, l0_mem_norm, l0_w_xkv), l0_w_xo, seq)
    x2 = _ffn(x2, l0_ffn_norm, l0_w_gate, l0_w_up, l0_w_down)

    q, k, v = _l1_proj(x2, l1_mix_norm, l1_w_qkv, tables)
    od = _banded_attention(q, k, v, batch, seq, DIL_HEADS // 2, lambda p: p,
                           tuple((dil, window // dil) for window, dil in DIL_PATTERNS))
    x2, xq = _outproj(x2, [od], [l1_w_out], l1_x_norm, l1_w_xq)
    x2 = _xattn(x2, xq, _memkv(mem2, l1_mem_norm, l1_w_xkv), l1_w_xo, seq)
    x2 = _ffn(x2, l1_ffn_norm, l1_w_gate, l1_w_up, l1_w_down, g_final=final_norm)
    return x2.reshape(batch, seq, D_MODEL)
```

```python
import functools

import jax
import jax.numpy as jnp
from jax import lax
from jax.experimental import pallas as pl
from jax.experimental.pallas import tpu as pltpu

D_MODEL = 1024
MEM_LEN = 256
HEAD_DIM = 64
ROPE_THETA = 10000.0
NORM_EPS = 1e-6
BLOCK = 128
SWA_HEADS = 8
SWA_KV_HEADS = 2
SWA_WINDOW = 128
MLA_HEADS = 8
MLA_Q_RANK = 384
MLA_KV_RANK = 256
MLA_NOPE_DIM = 64
MLA_ROPE_DIM = 32
MLA_V_DIM = 64
A_Q = SWA_HEADS * HEAD_DIM
A_KV = SWA_KV_HEADS * HEAD_DIM
DIL_HEADS = D_MODEL // HEAD_DIM
DIL_PATTERNS = ((128, 1), (512, 4), (2048, 16))
X_HEADS = 4
X_HEAD_DIM = 128
FFN_HIDDEN = -(-8 * D_MODEL // (3 * 256)) * 256

LANES = 128
V7X_VMEM_LIMIT = 56 * 1024 * 1024
NEG = -1e30
LOG2E = 1.4426950408889634

ROW_TILE = 512
DIL_SPAN = 2048
STAGE_DIL = 4
BANDED_GROUP = 4
MLA_TQ = 512
MLA_TK = 2048
MLA_SUBSTEPS = 2
FFN_CHUNKS = 2

F32 = jnp.float32
BF16 = jnp.bfloat16


def _cparams(sem, flags=None):
    return pltpu.CompilerParams(dimension_semantics=sem, vmem_limit_bytes=V7X_VMEM_LIMIT, flags=flags)


def _const_spec(shape):
    return pl.BlockSpec(shape, lambda *_: (0,) * len(shape))


def _rms(x, g):
    return x * lax.rsqrt(jnp.mean(x * x, axis=-1, keepdims=True) + NORM_EPS) * g


def _rope_chunk(xc, c, s, half):
    lane = lax.broadcasted_iota(jnp.int32, xc.shape, 1)
    up = pltpu.roll(xc, half, 1)
    down = pltpu.roll(xc, LANES - half, 1)
    partner = jnp.where((lane & (2 * half - 1)) < half, down, up)
    return xc * c + partner * s


def _tables_kernel(pos_ref, freq_ref, sg64_ref, sg32_ref, on32_ref, c64_ref, s64_ref, c32_ref, s32_ref):
    ang = pos_ref[...].astype(F32) * freq_ref[...]
    lane = lax.broadcasted_iota(jnp.int32, ang.shape, 1)
    half64, half32 = HEAD_DIM // 2, MLA_ROPE_DIM // 2
    on = on32_ref[...] > 0.5
    for t, sign64, sign32, off32, o64_ref, o32_ref in ((jnp.cos(ang), None, None, 1.0, c64_ref, c32_ref),
                                                       (jnp.sin(ang), sg64_ref, sg32_ref, 0.0, s64_ref, s32_ref)):
        b64 = jnp.where(lane < half64, t, 0.0)
        b64 = b64 + pltpu.roll(b64, half64, 1)
        b64 = b64 + pltpu.roll(b64, 2 * half64, 1)
        b32 = jnp.where((lane >= half64) & (lane < half64 + half32), t, 0.0)
        b32 = pltpu.roll(b32, MLA_NOPE_DIM - half64, 1) + pltpu.roll(b32, MLA_NOPE_DIM - half64 + half32, 1)
        if sign64 is not None:
            b64 = b64 * sign64[...]
            b32 = b32 * sign32[...]
        o64_ref[...] = b64
        o32_ref[...] = jnp.where(on, b32, off32)


def _rope_tables(positions):
    n = positions.size
    tm = 2048
    lane = jnp.arange(LANES)
    half64, half32 = HEAD_DIM // 2, MLA_ROPE_DIM // 2
    inv64 = ROPE_THETA ** (-jnp.arange(0, HEAD_DIM, 2, dtype=F32) / HEAD_DIM)
    inv32 = ROPE_THETA ** (-jnp.arange(0, MLA_ROPE_DIM, 2, dtype=F32) / MLA_ROPE_DIM)
    freq = jnp.zeros((LANES,), F32).at[:half64].set(inv64).at[half64:half64 + half32].set(inv32)[None, :]
    sg64 = jnp.where(lane % HEAD_DIM < half64, -1.0, 1.0).astype(F32)[None, :]
    rl = lane - MLA_NOPE_DIM
    on32 = ((rl >= 0) & (rl < MLA_ROPE_DIM)).astype(F32)[None, :]
    sg32 = jnp.where(rl % MLA_ROPE_DIM < half32, -1.0, 1.0).astype(F32)[None, :]
    row = pl.BlockSpec((tm, LANES), lambda i: (i, 0))
    out = jax.ShapeDtypeStruct((n, LANES), F32)
    return pl.pallas_call(
        _tables_kernel,
        grid=(n // tm,),
        in_specs=[pl.BlockSpec((tm, 1), lambda i: (i, 0))] + [_const_spec((1, LANES))] * 4,
        out_specs=[row] * 4,
        out_shape=[out] * 4,
        compiler_params=_cparams(("arbitrary",)),
        name="rope_tables",
    )(positions.reshape(n, 1), freq, sg64, sg32, on32)


L0_QA = 0
L0_KA = A_Q
L0_VA = L0_KA + 2 * A_KV
L0_CQ = L0_VA + 2 * A_KV
L0_CKV = L0_CQ + MLA_Q_RANK
L0_KR = L0_CKV + MLA_KV_RANK
L0_COLS = L0_KR + LANES
MLA_QK = MLA_HEADS * LANES
MLA_Q_LOG2_SCALE = (MLA_NOPE_DIM + MLA_ROPE_DIM) ** -0.5 * LOG2E
MLA_V = MLA_HEADS * MLA_V_DIM


def _l0_proj_kernel(x_ref, g_ref, win_ref, gq_ref, wuq_ref, gkv_ref, wukv_ref, c64_ref, s64_ref, c32_ref, s32_ref,
                    qa_ref, ka_ref, va_ref, qb_ref, kb_ref, vb_ref):
    h = _rms(x_ref[...], g_ref[...]).astype(BF16)
    z = jnp.dot(h, win_ref[...], preferred_element_type=F32)
    c64, s64, c32, s32 = c64_ref[...], s64_ref[...], c32_ref[...], s32_ref[...]
    qscale = HEAD_DIM ** -0.5 * LOG2E
    for c in range(A_Q // LANES):
        sl = slice(c * LANES, (c + 1) * LANES)
        qa_ref[:, sl] = (_rope_chunk(z[:, sl], c64, s64, HEAD_DIM // 2) * qscale).astype(qa_ref.dtype)
    for c in range(2 * A_KV // LANES):
        sl = slice(c * LANES, (c + 1) * LANES)
        ka_ref[:, sl] = _rope_chunk(z[:, L0_KA + c * LANES:L0_KA + (c + 1) * LANES], c64, s64,
                                    HEAD_DIM // 2).astype(ka_ref.dtype)
    va_ref[...] = z[:, L0_VA:L0_CQ].astype(va_ref.dtype)

    cq = _rms(z[:, L0_CQ:L0_CKV], gq_ref[...]).astype(BF16)
    qb = jnp.dot(cq, wuq_ref[...], preferred_element_type=F32)
    ckv = _rms(z[:, L0_CKV:L0_KR], gkv_ref[...]).astype(BF16)
    kv = jnp.dot(ckv, wukv_ref[...], preferred_element_type=F32)
    kr = _rope_chunk(z[:, L0_KR:L0_COLS], c32, s32, MLA_ROPE_DIM // 2)
    for hd in range(MLA_HEADS):
        sl = slice(hd * LANES, (hd + 1) * LANES)
        qb_ref[:, sl] = (_rope_chunk(qb[:, sl], c32, s32, MLA_ROPE_DIM // 2) * MLA_Q_LOG2_SCALE).astype(qb_ref.dtype)
        kb_ref[:, sl] = (kv[:, sl] + kr).astype(kb_ref.dtype)
    vb_ref[...] = kv[:, MLA_QK:].astype(vb_ref.dtype)


def _l0_proj(x2, g, w_in, g_q, w_uq, g_kv, w_ukv, tables):
    n = x2.shape[0]
    tm = ROW_TILE
    qa_w, ka_w, va_w, cq_w, ckv_w, kr_w = jnp.split(
        w_in, [A_Q, A_Q + A_KV, A_Q + 2 * A_KV, A_Q + 2 * A_KV + MLA_Q_RANK,
               A_Q + 2 * A_KV + MLA_Q_RANK + MLA_KV_RANK], axis=1)

    def dup(w):
        return jnp.repeat(w.reshape(D_MODEL, SWA_KV_HEADS, 1, HEAD_DIM), 2, axis=2).reshape(D_MODEL, 2 * A_KV)

    kr_pad = jnp.pad(kr_w, ((0, 0), (MLA_NOPE_DIM, LANES - MLA_NOPE_DIM - MLA_ROPE_DIM)))
    win_p = jnp.concatenate([qa_w, dup(ka_w), dup(va_w), cq_w, ckv_w, kr_pad], axis=1).astype(BF16)
    per_head_q = MLA_NOPE_DIM + MLA_ROPE_DIM
    wuq_p = jnp.pad(w_uq.reshape(MLA_Q_RANK, MLA_HEADS, per_head_q),
                    ((0, 0), (0, 0), (0, LANES - per_head_q))).reshape(MLA_Q_RANK, MLA_QK).astype(BF16)
    wukv3 = w_ukv.reshape(MLA_KV_RANK, MLA_HEADS, MLA_NOPE_DIM + MLA_V_DIM)
    wk_p = jnp.pad(wukv3[:, :, :MLA_NOPE_DIM], ((0, 0), (0, 0), (0, LANES - MLA_NOPE_DIM))).reshape(MLA_KV_RANK, MLA_QK)
    wv_p = wukv3[:, :, MLA_NOPE_DIM:].reshape(MLA_KV_RANK, MLA_V)
    wukv_p = jnp.concatenate([wk_p, wv_p], axis=1).astype(BF16)

    def row(w):
        return pl.BlockSpec((tm, w), lambda i: (i, 0))

    outs = [(A_Q, BF16), (2 * A_KV, BF16), (2 * A_KV, BF16), (MLA_QK, BF16), (MLA_QK, BF16), (MLA_V, BF16)]
    return pl.pallas_call(
        _l0_proj_kernel,
        grid=(n // tm,),
        in_specs=[row(D_MODEL), _const_spec((1, D_MODEL)), _const_spec(win_p.shape),
                  _const_spec((1, MLA_Q_RANK)), _const_spec(wuq_p.shape),
                  _const_spec((1, MLA_KV_RANK)), _const_spec(wukv_p.shape)] + [row(LANES)] * 4,
        out_specs=[row(w) for w, _ in outs],
        out_shape=[jax.ShapeDtypeStruct((n, w), dt) for w, dt in outs],
        compiler_params=_cparams(("arbitrary",)),
        name="l0_proj",
    )(x2, g[None, :], win_p, g_q[None, :], wuq_p, g_kv[None, :], wukv_p, *tables)


def _l1_proj_kernel(x_ref, g_ref, w_ref, c64_ref, s64_ref, q_ref, k_ref, v_ref):
    h = _rms(x_ref[...], g_ref[...]).astype(BF16)
    z = jnp.dot(h, w_ref[...], preferred_element_type=F32)
    c64, s64 = c64_ref[...], s64_ref[...]
    qscale = HEAD_DIM ** -0.5 * LOG2E
    for c in range(D_MODEL // LANES):
        sl = slice(c * LANES, (c + 1) * LANES)
        q_ref[:, sl] = _rope_chunk(z[:, sl], c64, s64, HEAD_DIM // 2) * qscale
        k_ref[:, sl] = _rope_chunk(z[:, D_MODEL + c * LANES:D_MODEL + (c + 1) * LANES], c64, s64, HEAD_DIM // 2)
    v_ref[...] = z[:, 2 * D_MODEL:]


def _l1_proj(x2, g, w_qkv, tables):
    n = x2.shape[0]
    tm = ROW_TILE
    row = pl.BlockSpec((tm, D_MODEL), lambda i: (i, 0))
    tab = pl.BlockSpec((tm, LANES), lambda i: (i, 0))
    out = jax.ShapeDtypeStruct((n, D_MODEL), F32)
    return pl.pallas_call(
        _l1_proj_kernel,
        grid=(n // tm,),
        in_specs=[row, _const_spec((1, D_MODEL)), _const_spec(w_qkv.shape), tab, tab],
        out_specs=[row] * 3,
        out_shape=[out] * 3,
        compiler_params=_cparams(("arbitrary",)),
        name="l1_proj",
    )(x2, g[None, :], w_qkv.astype(BF16), tables[0], tables[1])


def _banded_kernel(patterns, use_sink, span, *refs):
    refs = list(refs)
    sinks_ref = refs.pop(0) if use_sink else None
    q_ref, k_ref, v_ref, o_ref = refs[:4]
    kv_scr = refs[4:4 + 3 * len(patterns)]
    rest = refs[4 + 3 * len(patterns):]
    staged = any(dil > 1 for dil, _ in patterns)
    k4_s, v4_s = rest[:2] if staged else (None, None)
    stats = rest[2:] if staged else rest
    multi = len(patterns) > 1
    pair = pl.program_id(1)
    blk = pl.program_id(2)

    for first in (True, False):
        @pl.when(blk == 0 if first else blk > 0)
        def _(first=first):
            for pi, (dil, _) in enumerate(patterns):
                stream_len = BLOCK + span // dil
                for r in range(dil):
                    head = slice(r * stream_len, r * stream_len + BLOCK)
                    tail = slice((r + 1) * stream_len - BLOCK, (r + 1) * stream_len)
                    for ref in kv_scr[3 * pi:3 * pi + 3]:
                        ref[head, :] = jnp.zeros((BLOCK, LANES), BF16) if first else ref[tail, :]

    def put(pi, dst, count, k_rows, v_rows):
        k_s, vlo_s, vhi_s = kv_scr[3 * pi:3 * pi + 3]
        k_s[dst:dst + count, :] = k_rows.astype(BF16)
        v_rows = v_rows.astype(BF16)
        lo = lax.broadcasted_iota(jnp.int32, (count, LANES), 1) < HEAD_DIM
        vlo_s[dst:dst + count, :] = jnp.where(lo, v_rows, jnp.zeros_like(v_rows))
        vhi_s[dst:dst + count, :] = jnp.where(lo, jnp.zeros_like(v_rows), v_rows)

    quarter = span // STAGE_DIL
    if staged:
        for c in range(STAGE_DIL):
            k4_s[c * quarter:(c + 1) * quarter, :] = k_ref[pl.ds(c, quarter, stride=STAGE_DIL), :]
            v4_s[c * quarter:(c + 1) * quarter, :] = v_ref[pl.ds(c, quarter, stride=STAGE_DIL), :]
    for pi, (dil, _) in enumerate(patterns):
        cur_len = span // dil
        stream_len = BLOCK + cur_len
        if dil == 1:
            put(pi, BLOCK, span, k_ref[...], v_ref[...])
        elif dil == STAGE_DIL:
            for c in range(dil):
                src = slice(c * quarter, (c + 1) * quarter)
                put(pi, c * stream_len + BLOCK, cur_len, k4_s[src, :], v4_s[src, :])
        else:
            assert dil == STAGE_DIL * STAGE_DIL
            for r in range(dil):
                src = pl.ds((r % STAGE_DIL) * quarter + r // STAGE_DIL, cur_len, stride=STAGE_DIL)
                put(pi, r * stream_len + BLOCK, cur_len, k4_s[src, :], v4_s[src, :])

    lo_q = lax.broadcasted_iota(jnp.int32, (BLOCK, LANES), 1) < HEAD_DIM
    qi = lax.broadcasted_iota(jnp.int32, (2 * BLOCK, 2 * BLOCK), 0) & (BLOCK - 1)
    kj = lax.broadcasted_iota(jnp.int32, (2 * BLOCK, 2 * BLOCK), 1)
    dist = BLOCK + qi - kj
    in_cur = kj >= BLOCK
    if use_sink:
        top_rows = lax.broadcasted_iota(jnp.int32, (2 * BLOCK, 1), 0) < BLOCK
        sink2 = jnp.where(top_rows, sinks_ref[2 * pair], sinks_ref[2 * pair + 1]) * LOG2E

    for pi, (dil, max_dist) in enumerate(patterns):
        per_stream = span // (BLOCK * dil)
        stream_len = BLOCK + span // dil
        band = (dist >= 0) & (dist <= max_dist)
        bias_full = jnp.where(band, 0.0, NEG)
        bias_first = jnp.where(band & in_cur, 0.0, NEG)
        k_s, vlo_s, vhi_s = kv_scr[3 * pi:3 * pi + 3]

        def rows(start, count, dil=dil):
            if dil == 1:
                return pl.ds(pl.multiple_of(start, BLOCK), count)
            return pl.ds(start, count, stride=dil)

        def group(gi, carry, pi=pi, dil=dil, per_stream=per_stream, stream_len=stream_len, rows=rows,
                  bias_full=bias_full, bias_first=bias_first, k_s=k_s, vlo_s=vlo_s, vhi_s=vhi_s):
            sub = range(BANDED_GROUP)
            idx = [gi * BANDED_GROUP + u for u in sub]
            j = [i % per_stream for i in idx]
            stream = [i // per_stream for i in idx]
            start = [r + dil * BLOCK * jj for r, jj in zip(stream, j)]
            krows = [pl.ds(pl.multiple_of(r * stream_len + BLOCK * jj, BLOCK), 2 * BLOCK) for r, jj in zip(stream, j)]
            q = [q_ref[rows(st, BLOCK), :].astype(BF16) for st in start]
            q2 = [jnp.concatenate([jnp.where(lo_q, x, jnp.zeros_like(x)), jnp.where(lo_q, jnp.zeros_like(x), x)],
                                  axis=0) for x in q]
            s = [lax.dot_general(a, k_s[kr, :], (((1,), (1,)), ((), ())), preferred_element_type=F32)
                 for a, kr in zip(q2, krows)]
            has_prev = [jnp.logical_or(blk > 0, jj > 0) for jj in j]
            s = [x + jnp.where(hp, bias_full, bias_first) for x, hp in zip(s, has_prev)]
            m = [jnp.max(x, axis=1, keepdims=True) for x in s]
            if use_sink:
                m = [jnp.maximum(x, sink2) for x in m]
            p = [jnp.exp2(x - mm) for x, mm in zip(s, m)]
            l = [jnp.sum(x, axis=1, keepdims=True) for x in p]
            if use_sink:
                l = [x + jnp.exp2(sink2 - mm) for x, mm in zip(l, m)]
            pb = [x.astype(BF16) for x in p]
            pcat = [jnp.concatenate([x[:BLOCK], x[BLOCK:]], axis=1) for x in pb]
            vcat = [jnp.concatenate([vlo_s[kr, :], vhi_s[kr, :]], axis=0) for kr in krows]
            acc = [jnp.dot(a, b, preferred_element_type=F32) for a, b in zip(pcat, vcat)]
            l_pair = [jnp.where(lo_q, x[:BLOCK], x[BLOCK:]) for x in l]
            for u in sub:
                dst = rows(start[u], BLOCK)
                if multi:
                    acc_s, m_s, l_s = stats[3 * pi:3 * pi + 3]
                    acc_s[dst, :] = acc[u]
                    m_s[dst, :] = jnp.where(lo_q, m[u][:BLOCK], m[u][BLOCK:])
                    l_s[dst, :] = l_pair[u]
                else:
                    o_ref[dst, :] = (acc[u] / l_pair[u]).astype(o_ref.dtype)
            return carry

        lax.fori_loop(0, span // (BLOCK * BANDED_GROUP), group, 0)

    if multi:
        chunk = 2 * BLOCK

        def merge(c, carry):
            rs = pl.ds(pl.multiple_of(c * chunk, chunk), chunk)
            m_all = [stats[3 * pi + 1][rs, :] for pi in range(len(patterns))]
            m_top = functools.reduce(jnp.maximum, m_all)
            num = jnp.zeros((chunk, LANES), F32)
            den = jnp.zeros((chunk, LANES), F32)
            for pi in range(len(patterns)):
                w = jnp.exp2(m_all[pi] - m_top)
                num = num + w * stats[3 * pi][rs, :]
                den = den + w * stats[3 * pi + 2][rs, :]
            o_ref[rs, :] = (num / den).astype(o_ref.dtype)
            return carry

        lax.fori_loop(0, span // chunk, merge, 0)


def _banded_attention(q, k, v, batch, seq, n_pairs, kv_chunk_of_pair, patterns, sinks=None):
    n = q.shape[0]
    span = DIL_SPAN
    nblk = seq // span
    use_sink = sinks is not None

    def q_map(b, p, i):
        return (b * nblk + i, p)

    def kv_map(b, p, i):
        return (b * nblk + i, kv_chunk_of_pair(p))

    blk = (span, LANES)
    in_specs = [pl.BlockSpec(blk, q_map), pl.BlockSpec(blk, kv_map), pl.BlockSpec(blk, kv_map)]
    args = [q, k, v]
    if use_sink:
        in_specs = [pl.BlockSpec(memory_space=pltpu.SMEM)] + in_specs
        args = [sinks.astype(F32)] + args
    scratch = []
    for dil, _ in patterns:
        scratch += [pltpu.VMEM((dil * BLOCK + span, LANES), BF16)] * 3
    if any(dil > 1 for dil, _ in patterns):
        scratch += [pltpu.VMEM((span, LANES), k.dtype), pltpu.VMEM((span, LANES), v.dtype)]
    if len(patterns) > 1:
        scratch += [pltpu.VMEM((span, LANES), F32)] * (3 * len(patterns))
    return pl.pallas_call(
        functools.partial(_banded_kernel, patterns, use_sink, span),
        grid=(batch, n_pairs, nblk),
        in_specs=in_specs,
        out_specs=pl.BlockSpec(blk, q_map),
        out_shape=jax.ShapeDtypeStruct((n, n_pairs * LANES), BF16),
        scratch_shapes=scratch,
        compiler_params=_cparams(("arbitrary",) * 3),
        name="banded_attention_%d" % len(patterns),
    )(*args)


def _mla_kernel(tq, tk, q_ref, k_ref, v_ref, o_ref):
    qi = pl.program_id(2)
    lo_o = lax.broadcasted_iota(jnp.int32, (tq, LANES), 1) < MLA_V_DIM
    qs = (q_ref[:, 0:LANES], q_ref[:, LANES:2 * LANES])

    def step(ks, width, carry, masked):
        m0, l0, m1, l1, acc = carry
        kt = k_ref[pl.ds(ks, width), :]
        vt = v_ref[pl.ds(ks, width), :]
        lo_v = lax.broadcasted_iota(jnp.int32, (width, LANES), 1) < MLA_V_DIM
        new = []
        pv = None
        alphas = []
        for e, (m_old, l_old, sel) in enumerate(((m0, l0, lo_v), (m1, l1, ~lo_v))):
            s = lax.dot_general(qs[e], kt[:, e * LANES:(e + 1) * LANES], (((1,), (1,)), ((), ())),
                                preferred_element_type=F32)
            if masked:
                row_pos = qi * tq + lax.broadcasted_iota(jnp.int32, (tq, width), 0)
                col_pos = ks + lax.broadcasted_iota(jnp.int32, (tq, width), 1)
                s = jnp.where(row_pos >= col_pos, s, NEG)
            m_new = jnp.maximum(m_old, jnp.max(s, axis=1, keepdims=True))
            alpha = jnp.exp2(m_old - m_new)
            p = jnp.exp2(s - m_new)
            l_new = alpha * l_old + jnp.sum(p, axis=1, keepdims=True)
            ve = jnp.where(sel, vt, jnp.zeros_like(vt))
            o = jnp.dot(p.astype(BF16), ve, preferred_element_type=F32)
            pv = o if pv is None else pv + o
            new += [m_new, l_new]
            alphas.append(alpha)
        acc = acc * jnp.where(lo_o, alphas[0], alphas[1]) + pv
        return (new[0], new[1], new[2], new[3], acc)

    col = jnp.full((tq, 1), NEG, F32)
    zero = jnp.zeros((tq, 1), F32)
    carry = (col, zero, col, zero, jnp.zeros((tq, LANES), F32))
    q_start = qi * tq
    n_wide = q_start // tk
    sub = tk // MLA_SUBSTEPS

    def wide(kb, c):
        for u in range(MLA_SUBSTEPS):
            c = step(pl.multiple_of(kb * tk + u * sub, sub), sub, c, False)
        return c

    carry = lax.fori_loop(0, n_wide, wide, carry)
    def tail(widths):
        def run(c):
            if widths:
                c = step(pl.multiple_of(n_wide * tk, tk), widths * tq, c, False)
            return step(pl.multiple_of(q_start, tq), tq, c, True)
        return run

    carry = lax.switch(qi % (tk // tq), [tail(w) for w in range(tk // tq)], carry)
    _, l0, _, l1, acc = carry
    o_ref[...] = (acc / jnp.where(lo_o, l0, l1)).astype(o_ref.dtype)


def _mla_attention(qb, kb, vb, batch, seq):
    n = qb.shape[0]
    tq, tk = MLA_TQ, MLA_TK
    nq = seq // tq
    n_pairs = MLA_HEADS // 2
    return pl.pallas_call(
        functools.partial(_mla_kernel, tq, tk),
        grid=(batch, n_pairs, nq),
        in_specs=[pl.BlockSpec((tq, 2 * LANES), lambda b, p, i: (b * nq + i, p)),
                  pl.BlockSpec((seq, 2 * LANES), lambda b, p, i: (b, p)),
                  pl.BlockSpec((seq, LANES), lambda b, p, i: (b, p))],
        out_specs=pl.BlockSpec((tq, LANES), lambda b, p, i: (b * nq + i, p)),
        out_shape=jax.ShapeDtypeStruct((n, n_pairs * LANES), BF16),
        compiler_params=_cparams(("arbitrary",) * 3),
        name="mla_attention",
    )(qb, kb, vb)


def _outproj_kernel(n_in, *refs):
    x_ref = refs[0]
    a_refs = refs[1:1 + n_in]
    w_refs = refs[1 + n_in:1 + 2 * n_in]
    g_ref, wq_ref, x1_ref, xq_ref = refs[1 + 2 * n_in:]
    x1 = x_ref[...]
    for a_ref, w_ref in zip(a_refs, w_refs):
        x1 = x1 + jnp.dot(a_ref[...], w_ref[...], preferred_element_type=F32)
    x1_ref[...] = x1
    h = _rms(x1, g_ref[...]).astype(BF16)
    xq_ref[...] = jnp.dot(h, wq_ref[...], preferred_element_type=F32).astype(xq_ref.dtype)


def _outproj(x2, acts, weights, g_x, w_xq):
    n = x2.shape[0]
    tm = ROW_TILE
    xq_w = X_HEADS * X_HEAD_DIM

    def row(w):
        return pl.BlockSpec((tm, w), lambda i: (i, 0))

    weights = [w.astype(BF16) for w in weights]
    return pl.pallas_call(
        functools.partial(_outproj_kernel, len(acts)),
        grid=(n // tm,),
        in_specs=[row(D_MODEL)] + [row(a.shape[1]) for a in acts] + [_const_spec(w.shape) for w in weights]
        + [_const_spec((1, D_MODEL)), _const_spec((D_MODEL, xq_w))],
        out_specs=[row(D_MODEL), row(xq_w)],
        out_shape=[jax.ShapeDtypeStruct((n, D_MODEL), F32), jax.ShapeDtypeStruct((n, xq_w), BF16)],
        compiler_params=_cparams(("arbitrary",)),
        name="outproj",
    )(x2, *acts, *weights, g_x[None, :], w_xq.astype(BF16))


def _memkv_kernel(mem_ref, g_ref, w_ref, kv_ref):
    h = _rms(mem_ref[...], g_ref[...]).astype(BF16)
    kv_ref[...] = jnp.dot(h, w_ref[...], preferred_element_type=F32).astype(kv_ref.dtype)


def _memkv(mem2, g, w_xkv):
    n = mem2.shape[0]
    cols = 2 * X_HEADS * X_HEAD_DIM
    return pl.pallas_call(
        _memkv_kernel,
        grid=(n // MEM_LEN,),
        in_specs=[pl.BlockSpec((MEM_LEN, D_MODEL), lambda i: (i, 0)), _const_spec((1, D_MODEL)),
                  _const_spec((D_MODEL, cols))],
        out_specs=pl.BlockSpec((MEM_LEN, cols), lambda i: (i, 0)),
        out_shape=jax.ShapeDtypeStruct((n, cols), BF16),
        compiler_params=_cparams(("arbitrary",)),
        name="memkv",
    )(mem2, g[None, :], w_xkv.astype(BF16))


def _xattn_kernel(x_ref, q_ref, kv_ref, wo_ref, o_ref):
    scale = X_HEAD_DIM ** -0.5
    heads = []
    for hd in range(X_HEADS):
        q = q_ref[:, hd * X_HEAD_DIM:(hd + 1) * X_HEAD_DIM]
        k = kv_ref[:, hd * X_HEAD_DIM:(hd + 1) * X_HEAD_DIM]
        v = kv_ref[:, (X_HEADS + hd) * X_HEAD_DIM:(X_HEADS + hd + 1) * X_HEAD_DIM]
        s = lax.dot_general(q, k, (((1,), (1,)), ((), ())), preferred_element_type=F32) * scale
        m = jnp.max(s, axis=1, keepdims=True)
        p = jnp.exp(s - m)
        l = jnp.sum(p, axis=1, keepdims=True)
        o = jnp.dot(p.astype(BF16), v, preferred_element_type=F32) / l
        heads.append(o.astype(BF16))
    o_all = jnp.concatenate(heads, axis=1)
    o_ref[...] = x_ref[...] + jnp.dot(o_all, wo_ref[...], preferred_element_type=F32)


def _xattn(x2, xq, kv, w_xo, seq):
    n = x2.shape[0]
    tm = ROW_TILE
    per_batch = seq // tm
    xq_w = X_HEADS * X_HEAD_DIM
    return pl.pallas_call(
        _xattn_kernel,
        grid=(n // tm,),
        in_specs=[pl.BlockSpec((tm, D_MODEL), lambda i: (i, 0)), pl.BlockSpec((tm, xq_w), lambda i: (i, 0)),
                  pl.BlockSpec((MEM_LEN, 2 * xq_w), lambda i: (i // per_batch, 0)), _const_spec((xq_w, D_MODEL))],
        out_specs=pl.BlockSpec((tm, D_MODEL), lambda i: (i, 0)),
        out_shape=jax.ShapeDtypeStruct((n, D_MODEL), F32),
        compiler_params=_cparams(("arbitrary",)),
        name="xattn",
    )(x2, xq, kv, w_xo.astype(BF16))


def _ffn_kernel(final, x_ref, g_ref, wg_ref, wu_ref, wd_ref, *rest):
    if final:
        gf_ref, o_ref = rest
    else:
        (o_ref,) = rest
    x = x_ref[...]
    h = _rms(x, g_ref[...]).astype(BF16)
    width = FFN_HIDDEN // FFN_CHUNKS
    acc = x
    for c in range(FFN_CHUNKS):
        sl = slice(c * width, (c + 1) * width)
        gate = jnp.dot(h, wg_ref[:, sl], preferred_element_type=F32)
        up = jnp.dot(h, wu_ref[:, sl], preferred_element_type=F32)
        act = (gate * jax.nn.sigmoid(gate) * up).astype(BF16)
        acc = acc + jnp.dot(act, wd_ref[sl, :], preferred_element_type=F32)
    o_ref[...] = _rms(acc, gf_ref[...]) if final else acc


def _ffn(x2, g, w_gate, w_up, w_down, g_final=None):
    n = x2.shape[0]
    tm = ROW_TILE
    final = g_final is not None
    row = pl.BlockSpec((tm, D_MODEL), lambda i: (i, 0))

    def weight(shape):
        return pl.BlockSpec(shape, lambda i: (0, 0), pipeline_mode=pl.Buffered(1))

    in_specs = [row, _const_spec((1, D_MODEL)), weight((D_MODEL, FFN_HIDDEN)), weight((D_MODEL, FFN_HIDDEN)),
                weight((FFN_HIDDEN, D_MODEL))]
    args = [x2, g[None, :], w_gate.astype(BF16), w_up.astype(BF16), w_down.astype(BF16)]
    if final:
        in_specs.append(_const_spec((1, D_MODEL)))
        args.append(g_final[None, :])
    return pl.pallas_call(
        functools.partial(_ffn_kernel, final),
        grid=(n // tm,),
        in_specs=in_specs,
        out_specs=row,
        out_shape=jax.ShapeDtypeStruct((n, D_MODEL), F32),
        compiler_params=_cparams(("arbitrary",)),
        name="ffn",
    )(*args)


def kernel(x, mem, positions, l0_mix_norm, l0_w_in, l0_sinks, l0_q_norm, l0_w_uq, l0_kv_norm, l0_w_ukv, l0_w_out, l0_x_norm, l0_mem_norm, l0_w_xq, l0_w_xkv, l0_w_xo, l0_ffn_norm, l0_w_gate, l0_w_up, l0_w_down, l1_mix_norm, l1_w_qkv, l1_w_out, l1_x_norm, l1_mem_norm, l1_w_xq, l1_w_xkv, l1_w_xo, l1_ffn_norm, l1_w_gate, l1_w_up, l1_w_down, final_norm):
    batch, seq, _ = x.shape
    assert seq % DIL_SPAN == 0 and seq % MLA_TQ == 0 and seq % ROW_TILE == 0
    n = batch * seq
    x2 = x.reshape(n, D_MODEL)
    mem2 = mem.reshape(batch * MEM_LEN, D_MODEL)
    tables = _rope_tables(positions)

    qa, ka, va, qb, kb, vb = _l0_proj(x2, l0_mix_norm, l0_w_in, l0_q_norm, l0_w_uq, l0_kv_norm, l0_w_ukv, tables)
    oa = _banded_attention(qa, ka, va, batch, seq, SWA_HEADS // 2, lambda p: p // 2,
                           ((1, SWA_WINDOW - 1),), sinks=l0_sinks)
    ob = _mla_attention(qb, kb, vb, batch, seq)
    x2, xq = _outproj(x2, [oa, ob], [l0_w_out[:A_Q], l0_w_out[A_Q:]], l0_x_norm, l0_w_xq)
    x2 = _xattn(x2, xq, _memkv(mem2, l0_mem_norm, l0_w_xkv), l0_w_xo, seq)
    x2 = _ffn(x2, l0_ffn_norm, l0_w_gate, l0_w_up, l0_w_down)

    q, k, v = _l1_proj(x2, l1_mix_norm, l1_w_qkv, tables)
    od = _banded_attention(q, k, v, batch, seq, DIL_HEADS // 2, lambda p: p,
                           tuple((dil, window // dil) for window, dil in DIL_PATTERNS))
    x2, xq = _outproj(x2, [od], [l1_w_out], l1_x_norm, l1_w_xq)
    x2 = _xattn(x2, xq, _memkv(mem2, l1_mem_norm, l1_w_xkv), l1_w_xo, seq)
    x2 = _ffn(x2, l1_ffn_norm, l1_w_gate, l1_w_up, l1_w_down, g_final=final_norm)
    return x2.reshape(batch, seq, D_MODEL)
```

```python
import functools

import jax
import jax.numpy as jnp
from jax import lax
from jax.experimental import pallas as pl
from jax.experimental.pallas import tpu as pltpu

D_MODEL = 1024
MEM_LEN = 256
HEAD_DIM = 64
ROPE_THETA = 10000.0
NORM_EPS = 1e-6
BLOCK = 128
SWA_HEADS = 8
SWA_KV_HEADS = 2
SWA_WINDOW = 128
MLA_HEADS = 8
MLA_Q_RANK = 384
MLA_KV_RANK = 256
MLA_NOPE_DIM = 64
MLA_ROPE_DIM = 32
MLA_V_DIM = 64
A_Q = SWA_HEADS * HEAD_DIM
A_KV = SWA_KV_HEADS * HEAD_DIM
DIL_HEADS = D_MODEL // HEAD_DIM
DIL_PATTERNS = ((128, 1), (512, 4), (2048, 16))
X_HEADS = 4
X_HEAD_DIM = 128
FFN_HIDDEN = -(-8 * D_MODEL // (3 * 256)) * 256

LANES = 128
V7X_VMEM_LIMIT = 56 * 1024 * 1024
NEG = -1e30
LOG2E = 1.4426950408889634

ROW_TILE = 512
DIL_SPAN = 2048
STAGE_DIL = 4
BANDED_UNROLL = 4
BANDED_GROUP = 4
MLA_TQ = 512
MLA_TK = 2048
MLA_SUBSTEPS = 2
FFN_CHUNKS = 2

F32 = jnp.float32
BF16 = jnp.bfloat16


def _cparams(sem, flags=None):
    return pltpu.CompilerParams(dimension_semantics=sem, vmem_limit_bytes=V7X_VMEM_LIMIT, flags=flags)


def _const_spec(shape):
    return pl.BlockSpec(shape, lambda *_: (0,) * len(shape))


def _rms(x, g):
    return x * lax.rsqrt(jnp.mean(x * x, axis=-1, keepdims=True) + NORM_EPS) * g


def _rope_chunk(xc, c, s, half):
    lane = lax.broadcasted_iota(jnp.int32, xc.shape, 1)
    up = pltpu.roll(xc, half, 1)
    down = pltpu.roll(xc, LANES - half, 1)
    partner = jnp.where((lane & (2 * half - 1)) < half, down, up)
    return xc * c + partner * s


def _tables_kernel(pos_ref, freq_ref, sg64_ref, sg32_ref, on32_ref, c64_ref, s64_ref, c32_ref, s32_ref):
    ang = pos_ref[...].astype(F32) * freq_ref[...]
    lane = lax.broadcasted_iota(jnp.int32, ang.shape, 1)
    half64, half32 = HEAD_DIM // 2, MLA_ROPE_DIM // 2
    on = on32_ref[...] > 0.5
    for t, sign64, sign32, off32, o64_ref, o32_ref in ((jnp.cos(ang), None, None, 1.0, c64_ref, c32_ref),
                                                       (jnp.sin(ang), sg64_ref, sg32_ref, 0.0, s64_ref, s32_ref)):
        b64 = jnp.where(lane < half64, t, 0.0)
        b64 = b64 + pltpu.roll(b64, half64, 1)
        b64 = b64 + pltpu.roll(b64, 2 * half64, 1)
        b32 = jnp.where((lane >= half64) & (lane < half64 + half32), t, 0.0)
        b32 = pltpu.roll(b32, MLA_NOPE_DIM - half64, 1) + pltpu.roll(b32, MLA_NOPE_DIM - half64 + half32, 1)
        if sign64 is not None:
            b64 = b64 * sign64[...]
            b32 = b32 * sign32[...]
        o64_ref[...] = b64
        o32_ref[...] = jnp.where(on, b32, off32)


def _rope_tables(positions):
    n = positions.size
    tm = 2048
    lane = jnp.arange(LANES)
    half64, half32 = HEAD_DIM // 2, MLA_ROPE_DIM // 2
    inv64 = ROPE_THETA ** (-jnp.arange(0, HEAD_DIM, 2, dtype=F32) / HEAD_DIM)
    inv32 = ROPE_THETA ** (-jnp.arange(0, MLA_ROPE_DIM, 2, dtype=F32) / MLA_ROPE_DIM)
    freq = jnp.zeros((LANES,), F32).at[:half64].set(inv64).at[half64:half64 + half32].set(inv32)[None, :]
    sg64 = jnp.where(lane % HEAD_DIM < half64, -1.0, 1.0).astype(F32)[None, :]
    rl = lane - MLA_NOPE_DIM
    on32 = ((rl >= 0) & (rl < MLA_ROPE_DIM)).astype(F32)[None, :]
    sg32 = jnp.where(rl % MLA_ROPE_DIM < half32, -1.0, 1.0).astype(F32)[None, :]
    row = pl.BlockSpec((tm, LANES), lambda i: (i, 0))
    out = jax.ShapeDtypeStruct((n, LANES), F32)
    return pl.pallas_call(
        _tables_kernel,
        grid=(n // tm,),
        in_specs=[pl.BlockSpec((tm, 1), lambda i: (i, 0))] + [_const_spec((1, LANES))] * 4,
        out_specs=[row] * 4,
        out_shape=[out] * 4,
        compiler_params=_cparams(("arbitrary",)),
        name="rope_tables",
    )(positions.reshape(n, 1), freq, sg64, sg32, on32)


L0_QA = 0
L0_KA = A_Q
L0_VA = L0_KA + 2 * A_KV
L0_CQ = L0_VA + 2 * A_KV
L0_CKV = L0_CQ + MLA_Q_RANK
L0_KR = L0_CKV + MLA_KV_RANK
L0_COLS = L0_KR + LANES
MLA_QK = MLA_HEADS * LANES
MLA_Q_LOG2_SCALE = (MLA_NOPE_DIM + MLA_ROPE_DIM) ** -0.5 * LOG2E
MLA_V = MLA_HEADS * MLA_V_DIM


def _l0_proj_kernel(x_ref, g_ref, win_ref, gq_ref, wuq_ref, gkv_ref, wukv_ref, c64_ref, s64_ref, c32_ref, s32_ref,
                    qa_ref, ka_ref, va_ref, qb_ref, kb_ref, vb_ref):
    h = _rms(x_ref[...], g_ref[...]).astype(BF16)
    z = jnp.dot(h, win_ref[...], preferred_element_type=F32)
    c64, s64, c32, s32 = c64_ref[...], s64_ref[...], c32_ref[...], s32_ref[...]
    qscale = HEAD_DIM ** -0.5 * LOG2E
    for c in range(A_Q // LANES):
        sl = slice(c * LANES, (c + 1) * LANES)
        qa_ref[:, sl] = (_rope_chunk(z[:, sl], c64, s64, HEAD_DIM // 2) * qscale).astype(qa_ref.dtype)
    for c in range(2 * A_KV // LANES):
        sl = slice(c * LANES, (c + 1) * LANES)
        ka_ref[:, sl] = _rope_chunk(z[:, L0_KA + c * LANES:L0_KA + (c + 1) * LANES], c64, s64,
                                    HEAD_DIM // 2).astype(ka_ref.dtype)
    va_ref[...] = z[:, L0_VA:L0_CQ].astype(va_ref.dtype)

    cq = _rms(z[:, L0_CQ:L0_CKV], gq_ref[...]).astype(BF16)
    qb = jnp.dot(cq, wuq_ref[...], preferred_element_type=F32)
    ckv = _rms(z[:, L0_CKV:L0_KR], gkv_ref[...]).astype(BF16)
    kv = jnp.dot(ckv, wukv_ref[...], preferred_element_type=F32)
    kr = _rope_chunk(z[:, L0_KR:L0_COLS], c32, s32, MLA_ROPE_DIM // 2)
    for hd in range(MLA_HEADS):
        sl = slice(hd * LANES, (hd + 1) * LANES)
        qb_ref[:, sl] = (_rope_chunk(qb[:, sl], c32, s32, MLA_ROPE_DIM // 2) * MLA_Q_LOG2_SCALE).astype(qb_ref.dtype)
        kb_ref[:, sl] = (kv[:, sl] + kr).astype(kb_ref.dtype)
    vb_ref[...] = kv[:, MLA_QK:].astype(vb_ref.dtype)


def _l0_proj(x2, g, w_in, g_q, w_uq, g_kv, w_ukv, tables):
    n = x2.shape[0]
    tm = ROW_TILE
    qa_w, ka_w, va_w, cq_w, ckv_w, kr_w = jnp.split(
        w_in, [A_Q, A_Q + A_KV, A_Q + 2 * A_KV, A_Q + 2 * A_KV + MLA_Q_RANK,
               A_Q + 2 * A_KV + MLA_Q_RANK + MLA_KV_RANK], axis=1)

    def dup(w):
        return jnp.repeat(w.reshape(D_MODEL, SWA_KV_HEADS, 1, HEAD_DIM), 2, axis=2).reshape(D_MODEL, 2 * A_KV)

    kr_pad = jnp.pad(kr_w, ((0, 0), (MLA_NOPE_DIM, LANES - MLA_NOPE_DIM - MLA_ROPE_DIM)))
    win_p = jnp.concatenate([qa_w, dup(ka_w), dup(va_w), cq_w, ckv_w, kr_pad], axis=1).astype(BF16)
    per_head_q = MLA_NOPE_DIM + MLA_ROPE_DIM
    wuq_p = jnp.pad(w_uq.reshape(MLA_Q_RANK, MLA_HEADS, per_head_q),
                    ((0, 0), (0, 0), (0, LANES - per_head_q))).reshape(MLA_Q_RANK, MLA_QK).astype(BF16)
    wukv3 = w_ukv.reshape(MLA_KV_RANK, MLA_HEADS, MLA_NOPE_DIM + MLA_V_DIM)
    wk_p = jnp.pad(wukv3[:, :, :MLA_NOPE_DIM], ((0, 0), (0, 0), (0, LANES - MLA_NOPE_DIM))).reshape(MLA_KV_RANK, MLA_QK)
    wv_p = wukv3[:, :, MLA_NOPE_DIM:].reshape(MLA_KV_RANK, MLA_V)
    wukv_p = jnp.concatenate([wk_p, wv_p], axis=1).astype(BF16)

    def row(w):
        return pl.BlockSpec((tm, w), lambda i: (i, 0))

    outs = [(A_Q, BF16), (2 * A_KV, BF16), (2 * A_KV, BF16), (MLA_QK, BF16), (MLA_QK, BF16), (MLA_V, BF16)]
    return pl.pallas_call(
        _l0_proj_kernel,
        grid=(n // tm,),
        in_specs=[row(D_MODEL), _const_spec((1, D_MODEL)), _const_spec(win_p.shape),
                  _const_spec((1, MLA_Q_RANK)), _const_spec(wuq_p.shape),
                  _const_spec((1, MLA_KV_RANK)), _const_spec(wukv_p.shape)] + [row(LANES)] * 4,
        out_specs=[row(w) for w, _ in outs],
        out_shape=[jax.ShapeDtypeStruct((n, w), dt) for w, dt in outs],
        compiler_params=_cparams(("arbitrary",)),
        name="l0_proj",
    )(x2, g[None, :], win_p, g_q[None, :], wuq_p, g_kv[None, :], wukv_p, *tables)


def _l1_proj_kernel(x_ref, g_ref, w_ref, c64_ref, s64_ref, q_ref, k_ref, v_ref):
    h = _rms(x_ref[...], g_ref[...]).astype(BF16)
    z = jnp.dot(h, w_ref[...], preferred_element_type=F32)
    c64, s64 = c64_ref[...], s64_ref[...]
    qscale = HEAD_DIM ** -0.5 * LOG2E
    for c in range(D_MODEL // LANES):
        sl = slice(c * LANES, (c + 1) * LANES)
        q_ref[:, sl] = _rope_chunk(z[:, sl], c64, s64, HEAD_DIM // 2) * qscale
        k_ref[:, sl] = _rope_chunk(z[:, D_MODEL + c * LANES:D_MODEL + (c + 1) * LANES], c64, s64, HEAD_DIM // 2)
    v_ref[...] = z[:, 2 * D_MODEL:]


def _l1_proj(x2, g, w_qkv, tables):
    n = x2.shape[0]
    tm = ROW_TILE
    row = pl.BlockSpec((tm, D_MODEL), lambda i: (i, 0))
    tab = pl.BlockSpec((tm, LANES), lambda i: (i, 0))
    out = jax.ShapeDtypeStruct((n, D_MODEL), F32)
    return pl.pallas_call(
        _l1_proj_kernel,
        grid=(n // tm,),
        in_specs=[row, _const_spec((1, D_MODEL)), _const_spec(w_qkv.shape), tab, tab],
        out_specs=[row] * 3,
        out_shape=[out] * 3,
        compiler_params=_cparams(("arbitrary",)),
        name="l1_proj",
    )(x2, g[None, :], w_qkv.astype(BF16), tables[0], tables[1])


def _banded_kernel(patterns, use_sink, span, *refs):
    refs = list(refs)
    sinks_ref = refs.pop(0) if use_sink else None
    q_ref, k_ref, v_ref, o_ref = refs[:4]
    kv_scr = refs[4:4 + 3 * len(patterns)]
    rest = refs[4 + 3 * len(patterns):]
    staged = any(dil > 1 for dil, _ in patterns)
    k4_s, v4_s = rest[:2] if staged else (None, None)
    stats = rest[2:] if staged else rest
    multi = len(patterns) > 1
    pair = pl.program_id(1)
    blk = pl.program_id(2)

    for first in (True, False):
        @pl.when(blk == 0 if first else blk > 0)
        def _(first=first):
            for pi, (dil, _) in enumerate(patterns):
                stream_len = BLOCK + span // dil
                for r in range(dil):
                    head = slice(r * stream_len, r * stream_len + BLOCK)
                    tail = slice((r + 1) * stream_len - BLOCK, (r + 1) * stream_len)
                    for ref in kv_scr[3 * pi:3 * pi + 3]:
                        ref[head, :] = jnp.zeros((BLOCK, LANES), BF16) if first else ref[tail, :]

    def put(pi, dst, count, k_rows, v_rows):
        k_s, vlo_s, vhi_s = kv_scr[3 * pi:3 * pi + 3]
        k_s[dst:dst + count, :] = k_rows.astype(BF16)
        v_rows = v_rows.astype(BF16)
        lo = lax.broadcasted_iota(jnp.int32, (count, LANES), 1) < HEAD_DIM
        vlo_s[dst:dst + count, :] = jnp.where(lo, v_rows, jnp.zeros_like(v_rows))
        vhi_s[dst:dst + count, :] = jnp.where(lo, jnp.zeros_like(v_rows), v_rows)

    quarter = span // STAGE_DIL
    if staged:
        for c in range(STAGE_DIL):
            k4_s[c * quarter:(c + 1) * quarter, :] = k_ref[pl.ds(c, quarter, stride=STAGE_DIL), :]
            v4_s[c * quarter:(c + 1) * quarter, :] = v_ref[pl.ds(c, quarter, stride=STAGE_DIL), :]
    for pi, (dil, _) in enumerate(patterns):
        cur_len = span // dil
        stream_len = BLOCK + cur_len
        if dil == 1:
            put(pi, BLOCK, span, k_ref[...], v_ref[...])
        elif dil == STAGE_DIL:
            for c in range(dil):
                src = slice(c * quarter, (c + 1) * quarter)
                put(pi, c * stream_len + BLOCK, cur_len, k4_s[src, :], v4_s[src, :])
        else:
            assert dil == STAGE_DIL * STAGE_DIL
            for r in range(dil):
                src = pl.ds((r % STAGE_DIL) * quarter + r // STAGE_DIL, cur_len, stride=STAGE_DIL)
                put(pi, r * stream_len + BLOCK, cur_len, k4_s[src, :], v4_s[src, :])

    lo_q = lax.broadcasted_iota(jnp.int32, (BLOCK, LANES), 1) < HEAD_DIM
    qi = lax.broadcasted_iota(jnp.int32, (2 * BLOCK, 2 * BLOCK), 0) & (BLOCK - 1)
    kj = lax.broadcasted_iota(jnp.int32, (2 * BLOCK, 2 * BLOCK), 1)
    dist = BLOCK + qi - kj
    in_cur = kj >= BLOCK
    if use_sink:
        top_rows = lax.broadcasted_iota(jnp.int32, (2 * BLOCK, 1), 0) < BLOCK
        sink2 = jnp.where(top_rows, sinks_ref[2 * pair], sinks_ref[2 * pair + 1]) * LOG2E

    for pi, (dil, max_dist) in enumerate(patterns):
        per_stream = span // (BLOCK * dil)
        stream_len = BLOCK + span // dil
        band = (dist >= 0) & (dist <= max_dist)
        bias_full = jnp.where(band, 0.0, NEG)
        bias_first = jnp.where(band & in_cur, 0.0, NEG)
        k_s, vlo_s, vhi_s = kv_scr[3 * pi:3 * pi + 3]

        def rows(start, count, dil=dil):
            if dil == 1:
                return pl.ds(pl.multiple_of(start, BLOCK), count)
            return pl.ds(start, count, stride=dil)

        def group(gi, carry, pi=pi, dil=dil, per_stream=per_stream, stream_len=stream_len, rows=rows,
                  bias_full=bias_full, bias_first=bias_first, k_s=k_s, vlo_s=vlo_s, vhi_s=vhi_s):
            sub = range(BANDED_GROUP)
            idx = [gi * BANDED_GROUP + u for u in sub]
            j = [i % per_stream for i in idx]
            stream = [i // per_stream for i in idx]
            start = [r + dil * BLOCK * jj for r, jj in zip(stream, j)]
            krows = [pl.ds(pl.multiple_of(r * stream_len + BLOCK * jj, BLOCK), 2 * BLOCK) for r, jj in zip(stream, j)]
            q = [q_ref[rows(st, BLOCK), :].astype(BF16) for st in start]
            q2 = [jnp.concatenate([jnp.where(lo_q, x, jnp.zeros_like(x)), jnp.where(lo_q, jnp.zeros_like(x), x)],
                                  axis=0) for x in q]
            s = [lax.dot_general(a, k_s[kr, :], (((1,), (1,)), ((), ())), preferred_element_type=F32)
                 for a, kr in zip(q2, krows)]
            has_prev = [jnp.logical_or(blk > 0, jj > 0) for jj in j]
            s = [x + jnp.where(hp, bias_full, bias_first) for x, hp in zip(s, has_prev)]
            m = [jnp.max(x, axis=1, keepdims=True) for x in s]
            if use_sink:
                m = [jnp.maximum(x, sink2) for x in m]
            p = [jnp.exp2(x - mm) for x, mm in zip(s, m)]
            l = [jnp.sum(x, axis=1, keepdims=True) for x in p]
            if use_sink:
                l = [x + jnp.exp2(sink2 - mm) for x, mm in zip(l, m)]
            pb = [x.astype(BF16) for x in p]
            pcat = [jnp.concatenate([x[:BLOCK], x[BLOCK:]], axis=1) for x in pb]
            vcat = [jnp.concatenate([vlo_s[kr, :], vhi_s[kr, :]], axis=0) for kr in krows]
            acc = [jnp.dot(a, b, preferred_element_type=F32) for a, b in zip(pcat, vcat)]
            l_pair = [jnp.where(lo_q, x[:BLOCK], x[BLOCK:]) for x in l]
            for u in sub:
                dst = rows(start[u], BLOCK)
                if multi:
                    acc_s, m_s, l_s = stats[3 * pi:3 * pi + 3]
                    acc_s[dst, :] = acc[u]
                    m_s[dst, :] = jnp.where(lo_q, m[u][:BLOCK], m[u][BLOCK:])
                    l_s[dst, :] = l_pair[u]
                else:
                    o_ref[dst, :] = (acc[u] / l_pair[u]).astype(o_ref.dtype)
            return carry

        lax.fori_loop(0, span // (BLOCK * BANDED_GROUP), group, 0, unroll=BANDED_UNROLL)

    if multi:
        chunk = 2 * BLOCK

        def merge(c, carry):
            rs = pl.ds(pl.multiple_of(c * chunk, chunk), chunk)
            m_all = [stats[3 * pi + 1][rs, :] for pi in range(len(patterns))]
            m_top = functools.reduce(jnp.maximum, m_all)
            num = jnp.zeros((chunk, LANES), F32)
            den = jnp.zeros((chunk, LANES), F32)
            for pi in range(len(patterns)):
                w = jnp.exp2(m_all[pi] - m_top)
                num = num + w * stats[3 * pi][rs, :]
                den = den + w * stats[3 * pi + 2][rs, :]
            o_ref[rs, :] = (num / den).astype(o_ref.dtype)
            return carry

        lax.fori_loop(0, span // chunk, merge, 0)


def _banded_attention(q, k, v, batch, seq, n_pairs, kv_chunk_of_pair, patterns, sinks=None):
    n = q.shape[0]
    span = DIL_SPAN
    nblk = seq // span
    use_sink = sinks is not None

    def q_map(b, p, i):
        return (b * nblk + i, p)

    def kv_map(b, p, i):
        return (b * nblk + i, kv_chunk_of_pair(p))

    blk = (span, LANES)
    in_specs = [pl.BlockSpec(blk, q_map), pl.BlockSpec(blk, kv_map), pl.BlockSpec(blk, kv_map)]
    args = [q, k, v]
    if use_sink:
        in_specs = [pl.BlockSpec(memory_space=pltpu.SMEM)] + in_specs
        args = [sinks.astype(F32)] + args
    scratch = []
    for dil, _ in patterns:
        scratch += [pltpu.VMEM((dil * BLOCK + span, LANES), BF16)] * 3
    if any(dil > 1 for dil, _ in patterns):
        scratch += [pltpu.VMEM((span, LANES), k.dtype), pltpu.VMEM((span, LANES), v.dtype)]
    if len(patterns) > 1:
        scratch += [pltpu.VMEM((span, LANES), F32)] * (3 * len(patterns))
    return pl.pallas_call(
        functools.partial(_banded_kernel, patterns, use_sink, span),
        grid=(batch, n_pairs, nblk),
        in_specs=in_specs,
        out_specs=pl.BlockSpec(blk, q_map),
        out_shape=jax.ShapeDtypeStruct((n, n_pairs * LANES), BF16),
        scratch_shapes=scratch,
        compiler_params=_cparams(("arbitrary",) * 3),
        name="banded_attention_%d" % len(patterns),
    )(*args)


def _mla_kernel(tq, tk, q_ref, k_ref, v_ref, o_ref):
    qi = pl.program_id(2)
    lo_o = lax.broadcasted_iota(jnp.int32, (tq, LANES), 1) < MLA_V_DIM
    qs = (q_ref[:, 0:LANES], q_ref[:, LANES:2 * LANES])

    def step(ks, width, carry, masked):
        m0, l0, m1, l1, acc = carry
        kt = k_ref[pl.ds(ks, width), :]
        vt = v_ref[pl.ds(ks, width), :]
        lo_v = lax.broadcasted_iota(jnp.int32, (width, LANES), 1) < MLA_V_DIM
        new = []
        pv = None
        alphas = []
        for e, (m_old, l_old, sel) in enumerate(((m0, l0, lo_v), (m1, l1, ~lo_v))):
            s = lax.dot_general(qs[e], kt[:, e * LANES:(e + 1) * LANES], (((1,), (1,)), ((), ())),
                                preferred_element_type=F32)
            if masked:
                row_pos = qi * tq + lax.broadcasted_iota(jnp.int32, (tq, width), 0)
                col_pos = ks + lax.broadcasted_iota(jnp.int32, (tq, width), 1)
                s = jnp.where(row_pos >= col_pos, s, NEG)
            m_new = jnp.maximum(m_old, jnp.max(s, axis=1, keepdims=True))
            alpha = jnp.exp2(m_old - m_new)
            p = jnp.exp2(s - m_new)
            l_new = alpha * l_old + jnp.sum(p, axis=1, keepdims=True)
            ve = jnp.where(sel, vt, jnp.zeros_like(vt))
            o = jnp.dot(p.astype(BF16), ve, preferred_element_type=F32)
            pv = o if pv is None else pv + o
            new += [m_new, l_new]
            alphas.append(alpha)
        acc = acc * jnp.where(lo_o, alphas[0], alphas[1]) + pv
        return (new[0], new[1], new[2], new[3], acc)

    col = jnp.full((tq, 1), NEG, F32)
    zero = jnp.zeros((tq, 1), F32)
    carry = (col, zero, col, zero, jnp.zeros((tq, LANES), F32))
    q_start = qi * tq
    n_wide = q_start // tk
    sub = tk // MLA_SUBSTEPS

    def wide(kb, c):
        for u in range(MLA_SUBSTEPS):
            c = step(pl.multiple_of(kb * tk + u * sub, sub), sub, c, False)
        return c

    carry = lax.fori_loop(0, n_wide, wide, carry)
    def tail(widths):
        def run(c):
            if widths:
                c = step(pl.multiple_of(n_wide * tk, tk), widths * tq, c, False)
            return step(pl.multiple_of(q_start, tq), tq, c, True)
        return run

    carry = lax.switch(qi % (tk // tq), [tail(w) for w in range(tk // tq)], carry)
    _, l0, _, l1, acc = carry
    o_ref[...] = (acc / jnp.where(lo_o, l0, l1)).astype(o_ref.dtype)


def _mla_attention(qb, kb, vb, batch, seq):
    n = qb.shape[0]
    tq, tk = MLA_TQ, MLA_TK
    nq = seq // tq
    n_pairs = MLA_HEADS // 2
    return pl.pallas_call(
        functools.partial(_mla_kernel, tq, tk),
        grid=(batch, n_pairs, nq),
        in_specs=[pl.BlockSpec((tq, 2 * LANES), lambda b, p, i: (b * nq + i, p)),
                  pl.BlockSpec((seq, 2 * LANES), lambda b, p, i: (b, p)),
                  pl.BlockSpec((seq, LANES), lambda b, p, i: (b, p))],
        out_specs=pl.BlockSpec((tq, LANES), lambda b, p, i: (b * nq + i, p)),
        out_shape=jax.ShapeDtypeStruct((n, n_pairs * LANES), BF16),
        compiler_params=_cparams(("arbitrary",) * 3),
        name="mla_attention",
    )(qb, kb, vb)


def _outproj_kernel(n_in, *refs):
    x_ref = refs[0]
    a_refs = refs[1:1 + n_in]
    w_refs = refs[1 + n_in:1 + 2 * n_in]
    g_ref, wq_ref, x1_ref, xq_ref = refs[1 + 2 * n_in:]
    x1 = x_ref[...]
    for a_ref, w_ref in zip(a_refs, w_refs):
        x1 = x1 + jnp.dot(a_ref[...], w_ref[...], preferred_element_type=F32)
    x1_ref[...] = x1
    h = _rms(x1, g_ref[...]).astype(BF16)
    xq_ref[...] = jnp.dot(h, wq_ref[...], preferred_element_type=F32).astype(xq_ref.dtype)


def _outproj(x2, acts, weights, g_x, w_xq):
    n = x2.shape[0]
    tm = ROW_TILE
    xq_w = X_HEADS * X_HEAD_DIM

    def row(w):
        return pl.BlockSpec((tm, w), lambda i: (i, 0))

    weights = [w.astype(BF16) for w in weights]
    return pl.pallas_call(
        functools.partial(_outproj_kernel, len(acts)),
        grid=(n // tm,),
        in_specs=[row(D_MODEL)] + [row(a.shape[1]) for a in acts] + [_const_spec(w.shape) for w in weights]
        + [_const_spec((1, D_MODEL)), _const_spec((D_MODEL, xq_w))],
        out_specs=[row(D_MODEL), row(xq_w)],
        out_shape=[jax.ShapeDtypeStruct((n, D_MODEL), F32), jax.ShapeDtypeStruct((n, xq_w), BF16)],
        compiler_params=_cparams(("arbitrary",)),
        name="outproj",
    )(x2, *acts, *weights, g_x[None, :], w_xq.astype(BF16))


def _memkv_kernel(mem_ref, g_ref, w_ref, kv_ref):
    h = _rms(mem_ref[...], g_ref[...]).astype(BF16)
    kv_ref[...] = jnp.dot(h, w_ref[...], preferred_element_type=F32).astype(kv_ref.dtype)


def _memkv(mem2, g, w_xkv):
    n = mem2.shape[0]
    cols = 2 * X_HEADS * X_HEAD_DIM
    return pl.pallas_call(
        _memkv_kernel,
        grid=(n // MEM_LEN,),
        in_specs=[pl.BlockSpec((MEM_LEN, D_MODEL), lambda i: (i, 0)), _const_spec((1, D_MODEL)),
                  _const_spec((D_MODEL, cols))],
        out_specs=pl.BlockSpec((MEM_LEN, cols), lambda i: (i, 0)),
        out_shape=jax.ShapeDtypeStruct((n, cols), BF16),
        compiler_params=_cparams(("arbitrary",)),
        name="memkv",
    )(mem2, g[None, :], w_xkv.astype(BF16))


def _xattn_kernel(x_ref, q_ref, kv_ref, wo_ref, o_ref):
    scale = X_HEAD_DIM ** -0.5
    heads = []
    for hd in range(X_HEADS):
        q = q_ref[:, hd * X_HEAD_DIM:(hd + 1) * X_HEAD_DIM]
        k = kv_ref[:, hd * X_HEAD_DIM:(hd + 1) * X_HEAD_DIM]
        v = kv_ref[:, (X_HEADS + hd) * X_HEAD_DIM:(X_HEADS + hd + 1) * X_HEAD_DIM]
        s = lax.dot_general(q, k, (((1,), (1,)), ((), ())), preferred_element_type=F32) * scale
        m = jnp.max(s, axis=1, keepdims=True)
        p = jnp.exp(s - m)
        l = jnp.sum(p, axis=1, keepdims=True)
        o = jnp.dot(p.astype(BF16), v, preferred_element_type=F32) / l
        heads.append(o.astype(BF16))
    o_all = jnp.concatenate(heads, axis=1)
    o_ref[...] = x_ref[...] + jnp.dot(o_all, wo_ref[...], preferred_element_type=F32)


def _xattn(x2, xq, kv, w_xo, seq):
    n = x2.shape[0]
    tm = ROW_TILE
    per_batch = seq // tm
    xq_w = X_HEADS * X_HEAD_DIM
    return pl.pallas_call(
        _xattn_kernel,
        grid=(n // tm,),
        in_specs=[pl.BlockSpec((tm, D_MODEL), lambda i: (i, 0)), pl.BlockSpec((tm, xq_w), lambda i: (i, 0)),
                  pl.BlockSpec((MEM_LEN, 2 * xq_w), lambda i: (i // per_batch, 0)), _const_spec((xq_w, D_MODEL))],
        out_specs=pl.BlockSpec((tm, D_MODEL), lambda i: (i, 0)),
        out_shape=jax.ShapeDtypeStruct((n, D_MODEL), F32),
        compiler_params=_cparams(("arbitrary",)),
        name="xattn",
    )(x2, xq, kv, w_xo.astype(BF16))


def _ffn_kernel(final, x_ref, g_ref, wg_ref, wu_ref, wd_ref, *rest):
    if final:
        gf_ref, o_ref = rest
    else:
        (o_ref,) = rest
    x = x_ref[...]
    h = _rms(x, g_ref[...]).astype(BF16)
    width = FFN_HIDDEN // FFN_CHUNKS
    acc = x
    for c in range(FFN_CHUNKS):
        sl = slice(c * width, (c + 1) * width)
        gate = jnp.dot(h, wg_ref[:, sl], preferred_element_type=F32)
        up = jnp.dot(h, wu_ref[:, sl], preferred_element_type=F32)
        act = (gate * jax.nn.sigmoid(gate) * up).astype(BF16)
        acc = acc + jnp.dot(act, wd_ref[sl, :], preferred_element_type=F32)
    o_ref[...] = _rms(acc, gf_ref[...]) if final else acc


def _ffn(x2, g, w_gate, w_up, w_down, g_final=None):
    n = x2.shape[0]
    tm = ROW_TILE
    final = g_final is not None
    row = pl.BlockSpec((tm, D_MODEL), lambda i: (i, 0))

    def weight(shape):
        return pl.BlockSpec(shape, lambda i: (0, 0), pipeline_mode=pl.Buffered(1))

    in_specs = [row, _const_spec((1, D_MODEL)), weight((D_MODEL, FFN_HIDDEN)), weight((D_MODEL, FFN_HIDDEN)),
                weight((FFN_HIDDEN, D_MODEL))]
    args = [x2, g[None, :], w_gate.astype(BF16), w_up.astype(BF16), w_down.astype(BF16)]
    if final:
        in_specs.append(_const_spec((1, D_MODEL)))
        args.append(g_final[None, :])
    return pl.pallas_call(
        functools.partial(_ffn_kernel, final),
        grid=(n // tm,),
        in_specs=in_specs,
        out_specs=row,
        out_shape=jax.ShapeDtypeStruct((n, D_MODEL), F32),
        compiler_params=_cparams(("arbitrary",)),
        name="ffn",
    )(*args)


def kernel(x, mem, positions, l0_mix_norm, l0_w_in, l0_sinks, l0_q_norm, l0_w_uq, l0_kv_norm, l0_w_ukv, l0_w_out, l0_x_norm, l0_mem_norm, l0_w_xq, l0_w_xkv, l0_w_xo, l0_ffn_norm, l0_w_gate, l0_w_up, l0_w_down, l1_mix_norm, l1_w_qkv, l1_w_out, l1_x_norm, l1_mem_norm, l1_w_xq, l1_w_xkv, l1_w_xo, l1_ffn_norm, l1_w_gate, l1_w_up, l1_w_down, final_norm):
    batch, seq, _ = x.shape
    assert seq % DIL_SPAN == 0 and seq % MLA_TQ == 0 and seq % ROW_TILE == 0
    n = batch * seq
    x2 = x.reshape(n, D_MODEL)
    mem2 = mem.reshape(batch * MEM_LEN, D_MODEL)
    tables = _rope_tables(positions)

    qa, ka, va, qb, kb, vb = _l0_proj(x2, l0_mix_norm, l0_w_in, l0_q_norm, l0_w_uq, l0_kv_norm, l0_w_ukv, tables)
    oa = _banded_attention(qa, ka, va, batch, seq, SWA_HEADS // 2, lambda p: p // 2,
                           ((1, SWA_WINDOW - 1),), sinks=l0_sinks)
    ob = _mla_attention(qb, kb, vb, batch, seq)
    x2, xq = _outproj(x2, [oa, ob], [l0_w_out[:A_Q], l0_w_out[A_Q:]], l0_x_norm, l0_w_xq)
    x2 = _xattn(x2, xq, _memkv(mem2, l0_mem_norm, l0_w_xkv), l0_w_xo, seq)
    x2 = _ffn(x2, l0_ffn_norm, l0_w_gate, l0_w_up, l0_w_down)

    q, k, v = _l1_proj(x2, l1_mix_norm, l1_w_qkv, tables)
    od = _banded_attention(q, k, v, batch, seq, DIL_HEADS // 2, lambda p: p,
                           tuple((dil, window // dil) for window, dil in DIL_PATTERNS))
    x2, xq = _outproj(x2, [od], [l1_w_out], l1_x_norm, l1_w_xq)
    x2 = _xattn(x2, xq, _memkv(mem2, l1_mem_norm, l1_w_xkv), l1_w_xo, seq)
    x2 = _ffn(x2, l1_ffn_norm, l1_w_gate, l1_w_up, l1_w_down, g_final=final_norm)
    return x2.reshape(batch, seq, D_MODEL)
```

```python
import functools

import jax
import jax.numpy as jnp
from jax import lax
from jax.experimental import pallas as pl
from jax.experimental.pallas import tpu as pltpu

D_MODEL = 1024
MEM_LEN = 256
HEAD_DIM = 64
ROPE_THETA = 10000.0
NORM_EPS = 1e-6
BLOCK = 128
SWA_HEADS = 8
SWA_KV_HEADS = 2
SWA_WINDOW = 128
MLA_HEADS = 8
MLA_Q_RANK = 384
MLA_KV_RANK = 256
MLA_NOPE_DIM = 64
MLA_ROPE_DIM = 32
MLA_V_DIM = 64
A_Q = SWA_HEADS * HEAD_DIM
A_KV = SWA_KV_HEADS * HEAD_DIM
DIL_HEADS = D_MODEL // HEAD_DIM
DIL_PATTERNS = ((128, 1), (512, 4), (2048, 16))
X_HEADS = 4
X_HEAD_DIM = 128
FFN_HIDDEN = -(-8 * D_MODEL // (3 * 256)) * 256

LANES = 128
V7X_VMEM_LIMIT = 56 * 1024 * 1024
NEG = -1e30
LOG2E = 1.4426950408889634

ROW_TILE = 512
DIL_SPAN = 2048
STAGE_DIL = 4
BANDED_UNROLL = 4
BANDED_GROUP = 4
MLA_TQ = 512
MLA_TK = 2048
MLA_SUBSTEPS = 2
FFN_CHUNKS = 2

F32 = jnp.float32
BF16 = jnp.bfloat16


def _cparams(sem, flags=None):
    return pltpu.CompilerParams(dimension_semantics=sem, vmem_limit_bytes=V7X_VMEM_LIMIT, flags=flags)


def _const_spec(shape):
    return pl.BlockSpec(shape, lambda *_: (0,) * len(shape))


def _rms(x, g):
    return x * lax.rsqrt(jnp.mean(x * x, axis=-1, keepdims=True) + NORM_EPS) * g


def _rope_chunk(xc, c, s, half):
    lane = lax.broadcasted_iota(jnp.int32, xc.shape, 1)
    up = pltpu.roll(xc, half, 1)
    down = pltpu.roll(xc, LANES - half, 1)
    partner = jnp.where((lane & (2 * half - 1)) < half, down, up)
    return xc * c + partner * s


def _tables_kernel(pos_ref, freq_ref, sg64_ref, sg32_ref, on32_ref, c64_ref, s64_ref, c32_ref, s32_ref):
    ang = pos_ref[...].astype(F32) * freq_ref[...]
    lane = lax.broadcasted_iota(jnp.int32, ang.shape, 1)
    half64, half32 = HEAD_DIM // 2, MLA_ROPE_DIM // 2
    on = on32_ref[...] > 0.5
    for t, sign64, sign32, off32, o64_ref, o32_ref in ((jnp.cos(ang), None, None, 1.0, c64_ref, c32_ref),
                                                       (jnp.sin(ang), sg64_ref, sg32_ref, 0.0, s64_ref, s32_ref)):
        b64 = jnp.where(lane < half64, t, 0.0)
        b64 = b64 + pltpu.roll(b64, half64, 1)
        b64 = b64 + pltpu.roll(b64, 2 * half64, 1)
        b32 = jnp.where((lane >= half64) & (lane < half64 + half32), t, 0.0)
        b32 = pltpu.roll(b32, MLA_NOPE_DIM - half64, 1) + pltpu.roll(b32, MLA_NOPE_DIM - half64 + half32, 1)
        if sign64 is not None:
            b64 = b64 * sign64[...]
            b32 = b32 * sign32[...]
        o64_ref[...] = b64
        o32_ref[...] = jnp.where(on, b32, off32)


def _rope_tables(positions):
    n = positions.size
    tm = 2048
    lane = jnp.arange(LANES)
    half64, half32 = HEAD_DIM // 2, MLA_ROPE_DIM // 2
    inv64 = ROPE_THETA ** (-jnp.arange(0, HEAD_DIM, 2, dtype=F32) / HEAD_DIM)
    inv32 = ROPE_THETA ** (-jnp.arange(0, MLA_ROPE_DIM, 2, dtype=F32) / MLA_ROPE_DIM)
    freq = jnp.zeros((LANES,), F32).at[:half64].set(inv64).at[half64:half64 + half32].set(inv32)[None, :]
    sg64 = jnp.where(lane % HEAD_DIM < half64, -1.0, 1.0).astype(F32)[None, :]
    rl = lane - MLA_NOPE_DIM
    on32 = ((rl >= 0) & (rl < MLA_ROPE_DIM)).astype(F32)[None, :]
    sg32 = jnp.where(rl % MLA_ROPE_DIM < half32, -1.0, 1.0).astype(F32)[None, :]
    row = pl.BlockSpec((tm, LANES), lambda i: (i, 0))
    out = jax.ShapeDtypeStruct((n, LANES), F32)
    return pl.pallas_call(
        _tables_kernel,
        grid=(n // tm,),
        in_specs=[pl.BlockSpec((tm, 1), lambda i: (i, 0))] + [_const_spec((1, LANES))] * 4,
        out_specs=[row] * 4,
        out_shape=[out] * 4,
        compiler_params=_cparams(("arbitrary",)),
        name="rope_tables",
    )(positions.reshape(n, 1), freq, sg64, sg32, on32)


L0_QA = 0
L0_KA = A_Q
L0_VA = L0_KA + 2 * A_KV
L0_CQ = L0_VA + 2 * A_KV
L0_CKV = L0_CQ + MLA_Q_RANK
L0_KR = L0_CKV + MLA_KV_RANK
L0_COLS = L0_KR + LANES
MLA_QK = MLA_HEADS * LANES
MLA_Q_LOG2_SCALE = (MLA_NOPE_DIM + MLA_ROPE_DIM) ** -0.5 * LOG2E
MLA_V = MLA_HEADS * MLA_V_DIM


def _l0_proj_kernel(x_ref, g_ref, win_ref, gq_ref, wuq_ref, gkv_ref, wukv_ref, c64_ref, s64_ref, c32_ref, s32_ref,
                    qa_ref, ka_ref, va_ref, qb_ref, kb_ref, vb_ref):
    h = _rms(x_ref[...], g_ref[...]).astype(BF16)
    z = jnp.dot(h, win_ref[...], preferred_element_type=F32)
    c64, s64, c32, s32 = c64_ref[...], s64_ref[...], c32_ref[...], s32_ref[...]
    qscale = HEAD_DIM ** -0.5 * LOG2E
    for c in range(A_Q // LANES):
        sl = slice(c * LANES, (c + 1) * LANES)
        qa_ref[:, sl] = (_rope_chunk(z[:, sl], c64, s64, HEAD_DIM // 2) * qscale).astype(qa_ref.dtype)
    for c in range(2 * A_KV // LANES):
        sl = slice(c * LANES, (c + 1) * LANES)
        ka_ref[:, sl] = _rope_chunk(z[:, L0_KA + c * LANES:L0_KA + (c + 1) * LANES], c64, s64,
                                    HEAD_DIM // 2).astype(ka_ref.dtype)
    va_ref[...] = z[:, L0_VA:L0_CQ].astype(va_ref.dtype)

    cq = _rms(z[:, L0_CQ:L0_CKV], gq_ref[...]).astype(BF16)
    qb = jnp.dot(cq, wuq_ref[...], preferred_element_type=F32)
    ckv = _rms(z[:, L0_CKV:L0_KR], gkv_ref[...]).astype(BF16)
    kv = jnp.dot(ckv, wukv_ref[...], preferred_element_type=F32)
    kr = _rope_chunk(z[:, L0_KR:L0_COLS], c32, s32, MLA_ROPE_DIM // 2)
    for hd in range(MLA_HEADS):
        sl = slice(hd * LANES, (hd + 1) * LANES)
        qb_ref[:, sl] = (_rope_chunk(qb[:, sl], c32, s32, MLA_ROPE_DIM // 2) * MLA_Q_LOG2_SCALE).astype(qb_ref.dtype)
        kb_ref[:, sl] = (kv[:, sl] + kr).astype(kb_ref.dtype)
    vb_ref[...] = kv[:, MLA_QK:].astype(vb_ref.dtype)


def _l0_proj(x2, g, w_in, g_q, w_uq, g_kv, w_ukv, tables):
    n = x2.shape[0]
    tm = ROW_TILE
    qa_w, ka_w, va_w, cq_w, ckv_w, kr_w = jnp.split(
        w_in, [A_Q, A_Q + A_KV, A_Q + 2 * A_KV, A_Q + 2 * A_KV + MLA_Q_RANK,
               A_Q + 2 * A_KV + MLA_Q_RANK + MLA_KV_RANK], axis=1)

    def dup(w):
        return jnp.repeat(w.reshape(D_MODEL, SWA_KV_HEADS, 1, HEAD_DIM), 2, axis=2).reshape(D_MODEL, 2 * A_KV)

    kr_pad = jnp.pad(kr_w, ((0, 0), (MLA_NOPE_DIM, LANES - MLA_NOPE_DIM - MLA_ROPE_DIM)))
    win_p = jnp.concatenate([qa_w, dup(ka_w), dup(va_w), cq_w, ckv_w, kr_pad], axis=1).astype(BF16)
    per_head_q = MLA_NOPE_DIM + MLA_ROPE_DIM
    wuq_p = jnp.pad(w_uq.reshape(MLA_Q_RANK, MLA_HEADS, per_head_q),
                    ((0, 0), (0, 0), (0, LANES - per_head_q))).reshape(MLA_Q_RANK, MLA_QK).astype(BF16)
    wukv3 = w_ukv.reshape(MLA_KV_RANK, MLA_HEADS, MLA_NOPE_DIM + MLA_V_DIM)
    wk_p = jnp.pad(wukv3[:, :, :MLA_NOPE_DIM], ((0, 0), (0, 0), (0, LANES - MLA_NOPE_DIM))).reshape(MLA_KV_RANK, MLA_QK)
    wv_p = wukv3[:, :, MLA_NOPE_DIM:].reshape(MLA_KV_RANK, MLA_V)
    wukv_p = jnp.concatenate([wk_p, wv_p], axis=1).astype(BF16)

    def row(w):
        return pl.BlockSpec((tm, w), lambda i: (i, 0))

    outs = [(A_Q, BF16), (2 * A_KV, BF16), (2 * A_KV, BF16), (MLA_QK, BF16), (MLA_QK, BF16), (MLA_V, BF16)]
    return pl.pallas_call(
        _l0_proj_kernel,
        grid=(n // tm,),
        in_specs=[row(D_MODEL), _const_spec((1, D_MODEL)), _const_spec(win_p.shape),
                  _const_spec((1, MLA_Q_RANK)), _const_spec(wuq_p.shape),
                  _const_spec((1, MLA_KV_RANK)), _const_spec(wukv_p.shape)] + [row(LANES)] * 4,
        out_specs=[row(w) for w, _ in outs],
        out_shape=[jax.ShapeDtypeStruct((n, w), dt) for w, dt in outs],
        compiler_params=_cparams(("arbitrary",)),
        name="l0_proj",
    )(x2, g[None, :], win_p, g_q[None, :], wuq_p, g_kv[None, :], wukv_p, *tables)


def _l1_proj_kernel(x_ref, g_ref, w_ref, c64_ref, s64_ref, q_ref, k_ref, v_ref):
    h = _rms(x_ref[...], g_ref[...]).astype(BF16)
    z = jnp.dot(h, w_ref[...], preferred_element_type=F32)
    c64, s64 = c64_ref[...], s64_ref[...]
    qscale = HEAD_DIM ** -0.5 * LOG2E
    for c in range(D_MODEL // LANES):
        sl = slice(c * LANES, (c + 1) * LANES)
        q_ref[:, sl] = _rope_chunk(z[:, sl], c64, s64, HEAD_DIM // 2) * qscale
        k_ref[:, sl] = _rope_chunk(z[:, D_MODEL + c * LANES:D_MODEL + (c + 1) * LANES], c64, s64, HEAD_DIM // 2)
    v_ref[...] = z[:, 2 * D_MODEL:]


def _l1_proj(x2, g, w_qkv, tables):
    n = x2.shape[0]
    tm = ROW_TILE
    row = pl.BlockSpec((tm, D_MODEL), lambda i: (i, 0))
    tab = pl.BlockSpec((tm, LANES), lambda i: (i, 0))
    out = jax.ShapeDtypeStruct((n, D_MODEL), F32)
    return pl.pallas_call(
        _l1_proj_kernel,
        grid=(n // tm,),
        in_specs=[row, _const_spec((1, D_MODEL)), _const_spec(w_qkv.shape), tab, tab],
        out_specs=[row] * 3,
        out_shape=[out] * 3,
        compiler_params=_cparams(("arbitrary",)),
        name="l1_proj",
    )(x2, g[None, :], w_qkv.astype(BF16), tables[0], tables[1])


def _banded_kernel(patterns, use_sink, span, *refs):
    refs = list(refs)
    sinks_ref = refs.pop(0) if use_sink else None
    q_ref, k_ref, v_ref, o_ref = refs[:4]
    kv_scr = refs[4:4 + 3 * len(patterns)]
    rest = refs[4 + 3 * len(patterns):]
    staged = any(dil > 1 for dil, _ in patterns)
    k4_s, v4_s = rest[:2] if staged else (None, None)
    stats = rest[2:] if staged else rest
    multi = len(patterns) > 1
    pair = pl.program_id(1)
    blk = pl.program_id(2)

    for first in (True, False):
        @pl.when(blk == 0 if first else blk > 0)
        def _(first=first):
            for pi, (dil, _) in enumerate(patterns):
                stream_len = BLOCK + span // dil
                for r in range(dil):
                    head = slice(r * stream_len, r * stream_len + BLOCK)
                    tail = slice((r + 1) * stream_len - BLOCK, (r + 1) * stream_len)
                    for ref in kv_scr[3 * pi:3 * pi + 3]:
                        ref[head, :] = jnp.zeros((BLOCK, LANES), BF16) if first else ref[tail, :]

    def put(pi, dst, count, k_rows, v_rows):
        k_s, vlo_s, vhi_s = kv_scr[3 * pi:3 * pi + 3]
        k_s[dst:dst + count, :] = k_rows.astype(BF16)
        v_rows = v_rows.astype(BF16)
        lo = lax.broadcasted_iota(jnp.int32, (count, LANES), 1) < HEAD_DIM
        vlo_s[dst:dst + count, :] = jnp.where(lo, v_rows, jnp.zeros_like(v_rows))
        vhi_s[dst:dst + count, :] = jnp.where(lo, jnp.zeros_like(v_rows), v_rows)

    quarter = span // STAGE_DIL
    if staged:
        for c in range(STAGE_DIL):
            k4_s[c * quarter:(c + 1) * quarter, :] = k_ref[pl.ds(c, quarter, stride=STAGE_DIL), :]
            v4_s[c * quarter:(c + 1) * quarter, :] = v_ref[pl.ds(c, quarter, stride=STAGE_DIL), :]
    for pi, (dil, _) in enumerate(patterns):
        cur_len = span // dil
        stream_len = BLOCK + cur_len
        if dil == 1:
            put(pi, BLOCK, span, k_ref[...], v_ref[...])
        elif dil == STAGE_DIL:
            for c in range(dil):
                src = slice(c * quarter, (c + 1) * quarter)
                put(pi, c * stream_len + BLOCK, cur_len, k4_s[src, :], v4_s[src, :])
        else:
            assert dil == STAGE_DIL * STAGE_DIL
            for r in range(dil):
                src = pl.ds((r % STAGE_DIL) * quarter + r // STAGE_DIL, cur_len, stride=STAGE_DIL)
                put(pi, r * stream_len + BLOCK, cur_len, k4_s[src, :], v4_s[src, :])

    lo_q = lax.broadcasted_iota(jnp.int32, (BLOCK, LANES), 1) < HEAD_DIM
    qi = lax.broadcasted_iota(jnp.int32, (2 * BLOCK, 2 * BLOCK), 0) & (BLOCK - 1)
    kj = lax.broadcasted_iota(jnp.int32, (2 * BLOCK, 2 * BLOCK), 1)
    dist = BLOCK + qi - kj
    in_cur = kj >= BLOCK
    if use_sink:
        top_rows = lax.broadcasted_iota(jnp.int32, (2 * BLOCK, 1), 0) < BLOCK
        sink2 = jnp.where(top_rows, sinks_ref[2 * pair], sinks_ref[2 * pair + 1]) * LOG2E

    for pi, (dil, max_dist) in enumerate(patterns):
        per_stream = span // (BLOCK * dil)
        stream_len = BLOCK + span // dil
        band = (dist >= 0) & (dist <= max_dist)
        bias_full = jnp.where(band, 0.0, NEG)
        bias_first = jnp.where(band & in_cur, 0.0, NEG)
        k_s, vlo_s, vhi_s = kv_scr[3 * pi:3 * pi + 3]

        def rows(start, count, dil=dil):
            if dil == 1:
                return pl.ds(pl.multiple_of(start, BLOCK), count)
            return pl.ds(start, count, stride=dil)

        def group(gi, carry, pi=pi, dil=dil, per_stream=per_stream, stream_len=stream_len, rows=rows,
                  bias_full=bias_full, bias_first=bias_first, k_s=k_s, vlo_s=vlo_s, vhi_s=vhi_s):
            sub = range(BANDED_GROUP)
            idx = [gi * BANDED_GROUP + u for u in sub]
            j = [i % per_stream for i in idx]
            stream = [i // per_stream for i in idx]
            start = [r + dil * BLOCK * jj for r, jj in zip(stream, j)]
            krows = [pl.ds(pl.multiple_of(r * stream_len + BLOCK * jj, BLOCK), 2 * BLOCK) for r, jj in zip(stream, j)]
            q = [q_ref[rows(st, BLOCK), :].astype(BF16) for st in start]
            q2 = [jnp.concatenate([jnp.where(lo_q, x, jnp.zeros_like(x)), jnp.where(lo_q, jnp.zeros_like(x), x)],
                                  axis=0) for x in q]
            s = [lax.dot_general(a, k_s[kr, :], (((1,), (1,)), ((), ())), preferred_element_type=F32)
                 for a, kr in zip(q2, krows)]
            has_prev = [jnp.logical_or(blk > 0, jj > 0) for jj in j]
            s = [x + jnp.where(hp, bias_full, bias_first) for x, hp in zip(s, has_prev)]
            m = [jnp.max(x, axis=1, keepdims=True) for x in s]
            if use_sink:
                m = [jnp.maximum(x, sink2) for x in m]
            p = [jnp.exp2(x - mm) for x, mm in zip(s, m)]
            l = [jnp.sum(x, axis=1, keepdims=True) for x in p]
            if use_sink:
                l = [x + jnp.exp2(sink2 - mm) for x, mm in zip(l, m)]
            pb = [x.astype(BF16) for x in p]
            pcat = [jnp.concatenate([x[:BLOCK], x[BLOCK:]], axis=1) for x in pb]
            vcat = [jnp.concatenate([vlo_s[kr, :], vhi_s[kr, :]], axis=0) for kr in krows]
            acc = [jnp.dot(a, b, preferred_element_type=F32) for a, b in zip(pcat, vcat)]
            l_pair = [jnp.where(lo_q, x[:BLOCK], x[BLOCK:]) for x in l]
            for u in sub:
                dst = rows(start[u], BLOCK)
                if multi:
                    acc_s, m_s, l_s = stats[3 * pi:3 * pi + 3]
                    acc_s[dst, :] = acc[u]
                    m_s[dst, :] = jnp.where(lo_q, m[u][:BLOCK], m[u][BLOCK:])
                    l_s[dst, :] = l_pair[u]
                else:
                    o_ref[dst, :] = (acc[u] / l_pair[u]).astype(o_ref.dtype)
            return carry

        lax.fori_loop(0, span // (BLOCK * BANDED_GROUP), group, 0, unroll=BANDED_UNROLL)

    if multi:
        chunk = 2 * BLOCK

        def merge(c, carry):
            rs = pl.ds(pl.multiple_of(c * chunk, chunk), chunk)
            m_all = [stats[3 * pi + 1][rs, :] for pi in range(len(patterns))]
            m_top = functools.reduce(jnp.maximum, m_all)
            num = jnp.zeros((chunk, LANES), F32)
            den = jnp.zeros((chunk, LANES), F32)
            for pi in range(len(patterns)):
                w = jnp.exp2(m_all[pi] - m_top)
                num = num + w * stats[3 * pi][rs, :]
                den = den + w * stats[3 * pi + 2][rs, :]
            o_ref[rs, :] = (num / den).astype(o_ref.dtype)
            return carry

        lax.fori_loop(0, span // chunk, merge, 0)


def _banded_attention(q, k, v, batch, seq, n_pairs, kv_chunk_of_pair, patterns, sinks=None):
    n = q.shape[0]
    span = DIL_SPAN
    nblk = seq // span
    use_sink = sinks is not None

    def q_map(b, p, i):
        return (b * nblk + i, p)

    def kv_map(b, p, i):
        return (b * nblk + i, kv_chunk_of_pair(p))

    blk = (span, LANES)
    in_specs = [pl.BlockSpec(blk, q_map), pl.BlockSpec(blk, kv_map), pl.BlockSpec(blk, kv_map)]
    args = [q, k, v]
    if use_sink:
        in_specs = [pl.BlockSpec(memory_space=pltpu.SMEM)] + in_specs
        args = [sinks.astype(F32)] + args
    scratch = []
    for dil, _ in patterns:
        scratch += [pltpu.VMEM((dil * BLOCK + span, LANES), BF16)] * 3
    if any(dil > 1 for dil, _ in patterns):
        scratch += [pltpu.VMEM((span, LANES), k.dtype), pltpu.VMEM((span, LANES), v.dtype)]
    if len(patterns) > 1:
        scratch += [pltpu.VMEM((span, LANES), F32)] * (3 * len(patterns))
    return pl.pallas_call(
        functools.partial(_banded_kernel, patterns, use_sink, span),
        grid=(batch, n_pairs, nblk),
        in_specs=in_specs,
        out_specs=pl.BlockSpec(blk, q_map),
        out_shape=jax.ShapeDtypeStruct((n, n_pairs * LANES), BF16),
        scratch_shapes=scratch,
        compiler_params=_cparams(("arbitrary",) * 3),
        name="banded_attention_%d" % len(patterns),
    )(*args)


def _mla_kernel(tq, tk, q_ref, k_ref, v_ref, o_ref, vlo_s, vhi_s):
    qi = pl.program_id(2)
    lo_o = lax.broadcasted_iota(jnp.int32, (tq, LANES), 1) < MLA_V_DIM
    qs = (q_ref[:, 0:LANES], q_ref[:, LANES:2 * LANES])

    @pl.when(qi == 0)
    def _():
        v = v_ref[...]
        lo_v = lax.broadcasted_iota(jnp.int32, v.shape, 1) < MLA_V_DIM
        vlo_s[...] = jnp.where(lo_v, v, jnp.ones_like(v))
        vhi_s[...] = jnp.where(lo_v, jnp.ones_like(v), v)

    def step(ks, width, carry, masked):
        kt = k_ref[pl.ds(ks, width), :]
        out = []
        for e, v_s in enumerate((vlo_s, vhi_s)):
            m_old, acc = carry[2 * e:2 * e + 2]
            s = lax.dot_general(qs[e], kt[:, e * LANES:(e + 1) * LANES], (((1,), (1,)), ((), ())),
                                preferred_element_type=F32)
            if masked:
                row_pos = qi * tq + lax.broadcasted_iota(jnp.int32, (tq, width), 0)
                col_pos = ks + lax.broadcasted_iota(jnp.int32, (tq, width), 1)
                s = jnp.where(row_pos >= col_pos, s, NEG)
            m_new = jnp.maximum(m_old, jnp.max(s, axis=1, keepdims=True))
            p = jnp.exp2(s - m_new).astype(BF16)
            pv = jnp.dot(p, v_s[pl.ds(ks, width), :], preferred_element_type=F32)
            out += [m_new, acc * jnp.exp2(m_old - m_new) + pv]
        return tuple(out)

    col = jnp.full((tq, 1), NEG, F32)
    zero = jnp.zeros((tq, LANES), F32)
    carry = (col, zero, col, zero)
    q_start = qi * tq
    n_wide = q_start // tk
    sub = tk // MLA_SUBSTEPS

    def wide(kb, c):
        for u in range(MLA_SUBSTEPS):
            c = step(pl.multiple_of(kb * tk + u * sub, sub), sub, c, False)
        return c

    carry = lax.fori_loop(0, n_wide, wide, carry)
    def tail(widths):
        def run(c):
            if widths:
                c = step(pl.multiple_of(n_wide * tk, tk), widths * tq, c, False)
            return step(pl.multiple_of(q_start, tq), tq, c, True)
        return run

    carry = lax.switch(qi % (tk // tq), [tail(w) for w in range(tk // tq)], carry)
    _, acc_lo, _, acc_hi = carry
    acc = jnp.where(lo_o, acc_lo, acc_hi)
    den = jnp.where(lo_o, pltpu.roll(acc_lo, MLA_V_DIM, 1), pltpu.roll(acc_hi, MLA_V_DIM, 1))
    o_ref[...] = (acc / den).astype(o_ref.dtype)


def _mla_attention(qb, kb, vb, batch, seq):
    n = qb.shape[0]
    tq, tk = MLA_TQ, MLA_TK
    nq = seq // tq
    n_pairs = MLA_HEADS // 2
    return pl.pallas_call(
        functools.partial(_mla_kernel, tq, tk),
        grid=(batch, n_pairs, nq),
        in_specs=[pl.BlockSpec((tq, 2 * LANES), lambda b, p, i: (b * nq + i, p)),
                  pl.BlockSpec((seq, 2 * LANES), lambda b, p, i: (b, p)),
                  pl.BlockSpec((seq, LANES), lambda b, p, i: (b, p))],
        out_specs=pl.BlockSpec((tq, LANES), lambda b, p, i: (b * nq + i, p)),
        out_shape=jax.ShapeDtypeStruct((n, n_pairs * LANES), BF16),
        scratch_shapes=[pltpu.VMEM((seq, LANES), BF16)] * 2,
        compiler_params=_cparams(("arbitrary",) * 3),
        name="mla_attention",
    )(qb, kb, vb)


X_Q_LOG2_SCALE = X_HEAD_DIM ** -0.5 * LOG2E


def _mixout_xattn_kernel(n_in, *refs):
    x_ref = refs[0]
    a_refs = refs[1:1 + n_in]
    w_refs = refs[1 + n_in:1 + 2 * n_in]
    g_ref, wq_ref, kv_ref, wo_ref, o_ref = refs[1 + 2 * n_in:]
    x1 = x_ref[...]
    for a_ref, w_ref in zip(a_refs, w_refs):
        x1 = x1 + jnp.dot(a_ref[...], w_ref[...], preferred_element_type=F32)
    h = _rms(x1, g_ref[...]).astype(BF16)
    xq = (jnp.dot(h, wq_ref[...], preferred_element_type=F32) * X_Q_LOG2_SCALE).astype(BF16)
    heads = []
    for hd in range(X_HEADS):
        q = xq[:, hd * X_HEAD_DIM:(hd + 1) * X_HEAD_DIM]
        k = kv_ref[:, hd * X_HEAD_DIM:(hd + 1) * X_HEAD_DIM]
        v = kv_ref[:, (X_HEADS + hd) * X_HEAD_DIM:(X_HEADS + hd + 1) * X_HEAD_DIM]
        s = lax.dot_general(q, k, (((1,), (1,)), ((), ())), preferred_element_type=F32)
        p = jnp.exp2(s - jnp.max(s, axis=1, keepdims=True))
        l = jnp.sum(p, axis=1, keepdims=True)
        o = jnp.dot(p.astype(BF16), v, preferred_element_type=F32) / l
        heads.append(o.astype(BF16))
    o_all = jnp.concatenate(heads, axis=1)
    o_ref[...] = x1 + jnp.dot(o_all, wo_ref[...], preferred_element_type=F32)


def _mixout_xattn(x2, acts, weights, g_x, w_xq, kv, w_xo, seq):
    n = x2.shape[0]
    tm = ROW_TILE
    per_batch = seq // tm
    xq_w = X_HEADS * X_HEAD_DIM

    def row(w):
        return pl.BlockSpec((tm, w), lambda i: (i, 0))

    weights = [w.astype(BF16) for w in weights]
    return pl.pallas_call(
        functools.partial(_mixout_xattn_kernel, len(acts)),
        grid=(n // tm,),
        in_specs=[row(D_MODEL)] + [row(a.shape[1]) for a in acts] + [_const_spec(w.shape) for w in weights]
        + [_const_spec((1, D_MODEL)), _const_spec((D_MODEL, xq_w)),
           pl.BlockSpec((MEM_LEN, 2 * xq_w), lambda i: (i // per_batch, 0)), _const_spec((xq_w, D_MODEL))],
        out_specs=row(D_MODEL),
        out_shape=jax.ShapeDtypeStruct((n, D_MODEL), F32),
        compiler_params=_cparams(("arbitrary",)),
        name="mixout_xattn",
    )(x2, *acts, *weights, g_x[None, :], w_xq.astype(BF16), kv, w_xo.astype(BF16))


def _memkv_kernel(mem_ref, g_ref, w_ref, kv_ref):
    h = _rms(mem_ref[...], g_ref[...]).astype(BF16)
    kv_ref[...] = jnp.dot(h, w_ref[...], preferred_element_type=F32).astype(kv_ref.dtype)


def _memkv(mem2, g, w_xkv):
    n = mem2.shape[0]
    cols = 2 * X_HEADS * X_HEAD_DIM
    return pl.pallas_call(
        _memkv_kernel,
        grid=(n // MEM_LEN,),
        in_specs=[pl.BlockSpec((MEM_LEN, D_MODEL), lambda i: (i, 0)), _const_spec((1, D_MODEL)),
                  _const_spec((D_MODEL, cols))],
        out_specs=pl.BlockSpec((MEM_LEN, cols), lambda i: (i, 0)),
        out_shape=jax.ShapeDtypeStruct((n, cols), BF16),
        compiler_params=_cparams(("arbitrary",)),
        name="memkv",
    )(mem2, g[None, :], w_xkv.astype(BF16))


def _ffn_kernel(final, x_ref, g_ref, wg_ref, wu_ref, wd_ref, *rest):
    if final:
        gf_ref, o_ref = rest
    else:
        (o_ref,) = rest
    x = x_ref[...]
    h = _rms(x, g_ref[...]).astype(BF16)
    width = FFN_HIDDEN // FFN_CHUNKS
    acc = x
    for c in range(FFN_CHUNKS):
        sl = slice(c * width, (c + 1) * width)
        gate = jnp.dot(h, wg_ref[:, sl], preferred_element_type=F32)
        up = jnp.dot(h, wu_ref[:, sl], preferred_element_type=F32)
        act = (gate * jax.nn.sigmoid(gate) * up).astype(BF16)
        acc = acc + jnp.dot(act, wd_ref[sl, :], preferred_element_type=F32)
    o_ref[...] = _rms(acc, gf_ref[...]) if final else acc


def _ffn(x2, g, w_gate, w_up, w_down, g_final=None):
    n = x2.shape[0]
    tm = ROW_TILE
    final = g_final is not None
    row = pl.BlockSpec((tm, D_MODEL), lambda i: (i, 0))

    def weight(shape):
        return pl.BlockSpec(shape, lambda i: (0, 0), pipeline_mode=pl.Buffered(1))

    in_specs = [row, _const_spec((1, D_MODEL)), weight((D_MODEL, FFN_HIDDEN)), weight((D_MODEL, FFN_HIDDEN)),
                weight((FFN_HIDDEN, D_MODEL))]
    args = [x2, g[None, :], w_gate.astype(BF16), w_up.astype(BF16), w_down.astype(BF16)]
    if final:
        in_specs.append(_const_spec((1, D_MODEL)))
        args.append(g_final[None, :])
    return pl.pallas_call(
        functools.partial(_ffn_kernel, final),
        grid=(n // tm,),
        in_specs=in_specs,
        out_specs=row,
        out_shape=jax.ShapeDtypeStruct((n, D_MODEL), F32),
        compiler_params=_cparams(("arbitrary",)),
        name="ffn",
    )(*args)


def kernel(x, mem, positions, l0_mix_norm, l0_w_in, l0_sinks, l0_q_norm, l0_w_uq, l0_kv_norm, l0_w_ukv, l0_w_out, l0_x_norm, l0_mem_norm, l0_w_xq, l0_w_xkv, l0_w_xo, l0_ffn_norm, l0_w_gate, l0_w_up, l0_w_down, l1_mix_norm, l1_w_qkv, l1_w_out, l1_x_norm, l1_mem_norm, l1_w_xq, l1_w_xkv, l1_w_xo, l1_ffn_norm, l1_w_gate, l1_w_up, l1_w_down, final_norm):
    batch, seq, _ = x.shape
    assert seq % DIL_SPAN == 0 and seq % MLA_TQ == 0 and seq % ROW_TILE == 0
    n = batch * seq
    x2 = x.reshape(n, D_MODEL)
    mem2 = mem.reshape(batch * MEM_LEN, D_MODEL)
    tables = _rope_tables(positions)

    qa, ka, va, qb, kb, vb = _l0_proj(x2, l0_mix_norm, l0_w_in, l0_q_norm, l0_w_uq, l0_kv_norm, l0_w_ukv, tables)
    oa = _banded_attention(qa, ka, va, batch, seq, SWA_HEADS // 2, lambda p: p // 2,
                           ((1, SWA_WINDOW - 1),), sinks=l0_sinks)
    ob = _mla_attention(qb, kb, vb, batch, seq)
    x2 = _mixout_xattn(x2, [oa, ob], [l0_w_out[:A_Q], l0_w_out[A_Q:]], l0_x_norm, l0_w_xq,
                       _memkv(mem2, l0_mem_norm, l0_w_xkv), l0_w_xo, seq)
    x2 = _ffn(x2, l0_ffn_norm, l0_w_gate, l0_w_up, l0_w_down)

    q, k, v = _l1_proj(x2, l1_mix_norm, l1_w_qkv, tables)
    od = _banded_attention(q, k, v, batch, seq, DIL_HEADS // 2, lambda p: p,
                           tuple((dil, window // dil) for window, dil in DIL_PATTERNS))
    x2 = _mixout_xattn(x2, [od], [l1_w_out], l1_x_norm, l1_w_xq,
                       _memkv(mem2, l1_mem_norm, l1_w_xkv), l1_w_xo, seq)
    x2 = _ffn(x2, l1_ffn_norm, l1_w_gate, l1_w_up, l1_w_down, g_final=final_norm)
    return x2.reshape(batch, seq, D_MODEL)
```

```python
import functools

import jax
import jax.numpy as jnp
from jax import lax
from jax.experimental import pallas as pl
from jax.experimental.pallas import tpu as pltpu

D_MODEL = 1024
MEM_LEN = 256
HEAD_DIM = 64
ROPE_THETA = 10000.0
NORM_EPS = 1e-6
BLOCK = 128
SWA_HEADS = 8
SWA_KV_HEADS = 2
SWA_WINDOW = 128
MLA_HEADS = 8
MLA_Q_RANK = 384
MLA_KV_RANK = 256
MLA_NOPE_DIM = 64
MLA_ROPE_DIM = 32
MLA_V_DIM = 64
A_Q = SWA_HEADS * HEAD_DIM
A_KV = SWA_KV_HEADS * HEAD_DIM
DIL_HEADS = D_MODEL // HEAD_DIM
DIL_PATTERNS = ((128, 1), (512, 4), (2048, 16))
X_HEADS = 4
X_HEAD_DIM = 128
FFN_HIDDEN = -(-8 * D_MODEL // (3 * 256)) * 256

LANES = 128
V7X_VMEM_LIMIT = 56 * 1024 * 1024
NEG = -1e30
LOG2E = 1.4426950408889634

ROW_TILE = 512
DIL_SPAN = 2048
STAGE_DIL = 4
BANDED_UNROLL = 4
BANDED_GROUP = 4
MLA_TQ = 512
MLA_TK = 2048
MLA_PAIRS_PER_STEP = 2
MLA_SUBSTEPS = 2
FFN_CHUNKS = 2

F32 = jnp.float32
BF16 = jnp.bfloat16


def _cparams(sem, flags=None):
    return pltpu.CompilerParams(dimension_semantics=sem, vmem_limit_bytes=V7X_VMEM_LIMIT, flags=flags)


def _const_spec(shape):
    return pl.BlockSpec(shape, lambda *_: (0,) * len(shape))


def _rms(x, g):
    return x * lax.rsqrt(jnp.mean(x * x, axis=-1, keepdims=True) + NORM_EPS) * g


def _rope_chunk(xc, c, s, half):
    lane = lax.broadcasted_iota(jnp.int32, xc.shape, 1)
    up = pltpu.roll(xc, half, 1)
    down = pltpu.roll(xc, LANES - half, 1)
    partner = jnp.where((lane & (2 * half - 1)) < half, down, up)
    return xc * c + partner * s


def _tables_kernel(pos_ref, freq_ref, sg64_ref, sg32_ref, on32_ref, c64_ref, s64_ref, c32_ref, s32_ref):
    ang = pos_ref[...].astype(F32) * freq_ref[...]
    lane = lax.broadcasted_iota(jnp.int32, ang.shape, 1)
    half64, half32 = HEAD_DIM // 2, MLA_ROPE_DIM // 2
    on = on32_ref[...] > 0.5
    for t, sign64, sign32, off32, o64_ref, o32_ref in ((jnp.cos(ang), None, None, 1.0, c64_ref, c32_ref),
                                                       (jnp.sin(ang), sg64_ref, sg32_ref, 0.0, s64_ref, s32_ref)):
        b64 = jnp.where(lane < half64, t, 0.0)
        b64 = b64 + pltpu.roll(b64, half64, 1)
        b64 = b64 + pltpu.roll(b64, 2 * half64, 1)
        b32 = jnp.where((lane >= half64) & (lane < half64 + half32), t, 0.0)
        b32 = pltpu.roll(b32, MLA_NOPE_DIM - half64, 1) + pltpu.roll(b32, MLA_NOPE_DIM - half64 + half32, 1)
        if sign64 is not None:
            b64 = b64 * sign64[...]
            b32 = b32 * sign32[...]
        o64_ref[...] = b64
        o32_ref[...] = jnp.where(on, b32, off32)


def _rope_tables(positions):
    n = positions.size
    tm = 2048
    lane = jnp.arange(LANES)
    half64, half32 = HEAD_DIM // 2, MLA_ROPE_DIM // 2
    inv64 = ROPE_THETA ** (-jnp.arange(0, HEAD_DIM, 2, dtype=F32) / HEAD_DIM)
    inv32 = ROPE_THETA ** (-jnp.arange(0, MLA_ROPE_DIM, 2, dtype=F32) / MLA_ROPE_DIM)
    freq = jnp.zeros((LANES,), F32).at[:half64].set(inv64).at[half64:half64 + half32].set(inv32)[None, :]
    sg64 = jnp.where(lane % HEAD_DIM < half64, -1.0, 1.0).astype(F32)[None, :]
    rl = lane - MLA_NOPE_DIM
    on32 = ((rl >= 0) & (rl < MLA_ROPE_DIM)).astype(F32)[None, :]
    sg32 = jnp.where(rl % MLA_ROPE_DIM < half32, -1.0, 1.0).astype(F32)[None, :]
    row = pl.BlockSpec((tm, LANES), lambda i: (i, 0))
    out = jax.ShapeDtypeStruct((n, LANES), F32)
    return pl.pallas_call(
        _tables_kernel,
        grid=(n // tm,),
        in_specs=[pl.BlockSpec((tm, 1), lambda i: (i, 0))] + [_const_spec((1, LANES))] * 4,
        out_specs=[row] * 4,
        out_shape=[out] * 4,
        compiler_params=_cparams(("arbitrary",)),
        name="rope_tables",
    )(positions.reshape(n, 1), freq, sg64, sg32, on32)


L0_QA = 0
L0_KA = A_Q
L0_VA = L0_KA + 2 * A_KV
L0_CQ = L0_VA + 2 * A_KV
L0_CKV = L0_CQ + MLA_Q_RANK
L0_KR = L0_CKV + MLA_KV_RANK
L0_COLS = L0_KR + LANES
MLA_QK = MLA_HEADS * LANES
MLA_Q_LOG2_SCALE = (MLA_NOPE_DIM + MLA_ROPE_DIM) ** -0.5 * LOG2E
MLA_V = MLA_HEADS * MLA_V_DIM


def _l0_proj_kernel(x_ref, g_ref, win_ref, gq_ref, wuq_ref, gkv_ref, wukv_ref, c64_ref, s64_ref, c32_ref, s32_ref,
                    qa_ref, ka_ref, va_ref, qb_ref, kb_ref, vb_ref):
    h = _rms(x_ref[...], g_ref[...]).astype(BF16)
    z = jnp.dot(h, win_ref[...], preferred_element_type=F32)
    c64, s64, c32, s32 = c64_ref[...], s64_ref[...], c32_ref[...], s32_ref[...]
    qscale = HEAD_DIM ** -0.5 * LOG2E
    for c in range(A_Q // LANES):
        sl = slice(c * LANES, (c + 1) * LANES)
        qa_ref[:, sl] = (_rope_chunk(z[:, sl], c64, s64, HEAD_DIM // 2) * qscale).astype(qa_ref.dtype)
    for c in range(2 * A_KV // LANES):
        sl = slice(c * LANES, (c + 1) * LANES)
        ka_ref[:, sl] = _rope_chunk(z[:, L0_KA + c * LANES:L0_KA + (c + 1) * LANES], c64, s64,
                                    HEAD_DIM // 2).astype(ka_ref.dtype)
    va_ref[...] = z[:, L0_VA:L0_CQ].astype(va_ref.dtype)

    cq = _rms(z[:, L0_CQ:L0_CKV], gq_ref[...]).astype(BF16)
    qb = jnp.dot(cq, wuq_ref[...], preferred_element_type=F32)
    ckv = _rms(z[:, L0_CKV:L0_KR], gkv_ref[...]).astype(BF16)
    kv = jnp.dot(ckv, wukv_ref[...], preferred_element_type=F32)
    kr = _rope_chunk(z[:, L0_KR:L0_COLS], c32, s32, MLA_ROPE_DIM // 2)
    for hd in range(MLA_HEADS):
        sl = slice(hd * LANES, (hd + 1) * LANES)
        qb_ref[:, sl] = (_rope_chunk(qb[:, sl], c32, s32, MLA_ROPE_DIM // 2) * MLA_Q_LOG2_SCALE).astype(qb_ref.dtype)
        kb_ref[:, sl] = (kv[:, sl] + kr).astype(kb_ref.dtype)
    vb_ref[...] = kv[:, MLA_QK:].astype(vb_ref.dtype)


def _l0_proj(x2, g, w_in, g_q, w_uq, g_kv, w_ukv, tables):
    n = x2.shape[0]
    tm = ROW_TILE
    qa_w, ka_w, va_w, cq_w, ckv_w, kr_w = jnp.split(
        w_in, [A_Q, A_Q + A_KV, A_Q + 2 * A_KV, A_Q + 2 * A_KV + MLA_Q_RANK,
               A_Q + 2 * A_KV + MLA_Q_RANK + MLA_KV_RANK], axis=1)

    def dup(w):
        return jnp.repeat(w.reshape(D_MODEL, SWA_KV_HEADS, 1, HEAD_DIM), 2, axis=2).reshape(D_MODEL, 2 * A_KV)

    kr_pad = jnp.pad(kr_w, ((0, 0), (MLA_NOPE_DIM, LANES - MLA_NOPE_DIM - MLA_ROPE_DIM)))
    win_p = jnp.concatenate([qa_w, dup(ka_w), dup(va_w), cq_w, ckv_w, kr_pad], axis=1).astype(BF16)
    per_head_q = MLA_NOPE_DIM + MLA_ROPE_DIM
    wuq_p = jnp.pad(w_uq.reshape(MLA_Q_RANK, MLA_HEADS, per_head_q),
                    ((0, 0), (0, 0), (0, LANES - per_head_q))).reshape(MLA_Q_RANK, MLA_QK).astype(BF16)
    wukv3 = w_ukv.reshape(MLA_KV_RANK, MLA_HEADS, MLA_NOPE_DIM + MLA_V_DIM)
    wk_p = jnp.pad(wukv3[:, :, :MLA_NOPE_DIM], ((0, 0), (0, 0), (0, LANES - MLA_NOPE_DIM))).reshape(MLA_KV_RANK, MLA_QK)
    wv_p = wukv3[:, :, MLA_NOPE_DIM:].reshape(MLA_KV_RANK, MLA_V)
    wukv_p = jnp.concatenate([wk_p, wv_p], axis=1).astype(BF16)

    def row(w):
        return pl.BlockSpec((tm, w), lambda i: (i, 0))

    outs = [(A_Q, BF16), (2 * A_KV, BF16), (2 * A_KV, BF16), (MLA_QK, BF16), (MLA_QK, BF16), (MLA_V, BF16)]
    return pl.pallas_call(
        _l0_proj_kernel,
        grid=(n // tm,),
        in_specs=[row(D_MODEL), _const_spec((1, D_MODEL)), _const_spec(win_p.shape),
                  _const_spec((1, MLA_Q_RANK)), _const_spec(wuq_p.shape),
                  _const_spec((1, MLA_KV_RANK)), _const_spec(wukv_p.shape)] + [row(LANES)] * 4,
        out_specs=[row(w) for w, _ in outs],
        out_shape=[jax.ShapeDtypeStruct((n, w), dt) for w, dt in outs],
        compiler_params=_cparams(("arbitrary",)),
        name="l0_proj",
    )(x2, g[None, :], win_p, g_q[None, :], wuq_p, g_kv[None, :], wukv_p, *tables)


def _l1_proj_kernel(x_ref, g_ref, w_ref, c64_ref, s64_ref, q_ref, k_ref, v_ref):
    h = _rms(x_ref[...], g_ref[...]).astype(BF16)
    z = jnp.dot(h, w_ref[...], preferred_element_type=F32)
    c64, s64 = c64_ref[...], s64_ref[...]
    qscale = HEAD_DIM ** -0.5 * LOG2E
    for c in range(D_MODEL // LANES):
        sl = slice(c * LANES, (c + 1) * LANES)
        q_ref[:, sl] = _rope_chunk(z[:, sl], c64, s64, HEAD_DIM // 2) * qscale
        k_ref[:, sl] = _rope_chunk(z[:, D_MODEL + c * LANES:D_MODEL + (c + 1) * LANES], c64, s64, HEAD_DIM // 2)
    v_ref[...] = z[:, 2 * D_MODEL:]


def _l1_proj(x2, g, w_qkv, tables):
    n = x2.shape[0]
    tm = ROW_TILE
    row = pl.BlockSpec((tm, D_MODEL), lambda i: (i, 0))
    tab = pl.BlockSpec((tm, LANES), lambda i: (i, 0))
    out = jax.ShapeDtypeStruct((n, D_MODEL), F32)
    return pl.pallas_call(
        _l1_proj_kernel,
        grid=(n // tm,),
        in_specs=[row, _const_spec((1, D_MODEL)), _const_spec(w_qkv.shape), tab, tab],
        out_specs=[row] * 3,
        out_shape=[out] * 3,
        compiler_params=_cparams(("arbitrary",)),
        name="l1_proj",
    )(x2, g[None, :], w_qkv.astype(BF16), tables[0], tables[1])


def _banded_kernel(patterns, use_sink, span, *refs):
    refs = list(refs)
    sinks_ref = refs.pop(0) if use_sink else None
    q_ref, k_ref, v_ref, o_ref = refs[:4]
    kv_scr = refs[4:4 + 3 * len(patterns)]
    rest = refs[4 + 3 * len(patterns):]
    staged = any(dil > 1 for dil, _ in patterns)
    k4_s, v4_s = rest[:2] if staged else (None, None)
    stats = rest[2:] if staged else rest
    multi = len(patterns) > 1
    pair = pl.program_id(1)
    blk = pl.program_id(2)

    for first in (True, False):
        @pl.when(blk == 0 if first else blk > 0)
        def _(first=first):
            for pi, (dil, _) in enumerate(patterns):
                stream_len = BLOCK + span // dil
                for r in range(dil):
                    head = slice(r * stream_len, r * stream_len + BLOCK)
                    tail = slice((r + 1) * stream_len - BLOCK, (r + 1) * stream_len)
                    for ref in kv_scr[3 * pi:3 * pi + 3]:
                        ref[head, :] = jnp.zeros((BLOCK, LANES), BF16) if first else ref[tail, :]

    def put(pi, dst, count, k_rows, v_rows):
        k_s, vlo_s, vhi_s = kv_scr[3 * pi:3 * pi + 3]
        k_s[dst:dst + count, :] = k_rows.astype(BF16)
        v_rows = v_rows.astype(BF16)
        lo = lax.broadcasted_iota(jnp.int32, (count, LANES), 1) < HEAD_DIM
        vlo_s[dst:dst + count, :] = jnp.where(lo, v_rows, jnp.zeros_like(v_rows))
        vhi_s[dst:dst + count, :] = jnp.where(lo, jnp.zeros_like(v_rows), v_rows)

    quarter = span // STAGE_DIL
    if staged:
        for c in range(STAGE_DIL):
            k4_s[c * quarter:(c + 1) * quarter, :] = k_ref[pl.ds(c, quarter, stride=STAGE_DIL), :]
            v4_s[c * quarter:(c + 1) * quarter, :] = v_ref[pl.ds(c, quarter, stride=STAGE_DIL), :]
    for pi, (dil, _) in enumerate(patterns):
        cur_len = span // dil
        stream_len = BLOCK + cur_len
        if dil == 1:
            put(pi, BLOCK, span, k_ref[...], v_ref[...])
        elif dil == STAGE_DIL:
            for c in range(dil):
                src = slice(c * quarter, (c + 1) * quarter)
                put(pi, c * stream_len + BLOCK, cur_len, k4_s[src, :], v4_s[src, :])
        else:
            assert dil == STAGE_DIL * STAGE_DIL
            for r in range(dil):
                src = pl.ds((r % STAGE_DIL) * quarter + r // STAGE_DIL, cur_len, stride=STAGE_DIL)
                put(pi, r * stream_len + BLOCK, cur_len, k4_s[src, :], v4_s[src, :])

    lo_q = lax.broadcasted_iota(jnp.int32, (BLOCK, LANES), 1) < HEAD_DIM
    qi = lax.broadcasted_iota(jnp.int32, (2 * BLOCK, 2 * BLOCK), 0) & (BLOCK - 1)
    kj = lax.broadcasted_iota(jnp.int32, (2 * BLOCK, 2 * BLOCK), 1)
    dist = BLOCK + qi - kj
    in_cur = kj >= BLOCK
    if use_sink:
        top_rows = lax.broadcasted_iota(jnp.int32, (2 * BLOCK, 1), 0) < BLOCK
        sink2 = jnp.where(top_rows, sinks_ref[2 * pair], sinks_ref[2 * pair + 1]) * LOG2E

    for pi, (dil, max_dist) in enumerate(patterns):
        per_stream = span // (BLOCK * dil)
        stream_len = BLOCK + span // dil
        band = (dist >= 0) & (dist <= max_dist)
        bias_full = jnp.where(band, 0.0, NEG)
        bias_first = jnp.where(band & in_cur, 0.0, NEG)
        k_s, vlo_s, vhi_s = kv_scr[3 * pi:3 * pi + 3]

        def rows(start, count, dil=dil):
            if dil == 1:
                return pl.ds(pl.multiple_of(start, BLOCK), count)
            return pl.ds(start, count, stride=dil)

        def group(gi, carry, pi=pi, dil=dil, per_stream=per_stream, stream_len=stream_len, rows=rows,
                  bias_full=bias_full, bias_first=bias_first, k_s=k_s, vlo_s=vlo_s, vhi_s=vhi_s):
            sub = range(BANDED_GROUP)
            idx = [gi * BANDED_GROUP + u for u in sub]
            j = [i % per_stream for i in idx]
            stream = [i // per_stream for i in idx]
            start = [r + dil * BLOCK * jj for r, jj in zip(stream, j)]
            krows = [pl.ds(pl.multiple_of(r * stream_len + BLOCK * jj, BLOCK), 2 * BLOCK) for r, jj in zip(stream, j)]
            q = [q_ref[rows(st, BLOCK), :].astype(BF16) for st in start]
            q2 = [jnp.concatenate([jnp.where(lo_q, x, jnp.zeros_like(x)), jnp.where(lo_q, jnp.zeros_like(x), x)],
                                  axis=0) for x in q]
            s = [lax.dot_general(a, k_s[kr, :], (((1,), (1,)), ((), ())), preferred_element_type=F32)
                 for a, kr in zip(q2, krows)]
            has_prev = [jnp.logical_or(blk > 0, jj > 0) for jj in j]
            s = [x + jnp.where(hp, bias_full, bias_first) for x, hp in zip(s, has_prev)]
            m = [jnp.max(x, axis=1, keepdims=True) for x in s]
            if use_sink:
                m = [jnp.maximum(x, sink2) for x in m]
            p = [jnp.exp2(x - mm) for x, mm in zip(s, m)]
            l = [jnp.sum(x, axis=1, keepdims=True) for x in p]
            if use_sink:
                l = [x + jnp.exp2(sink2 - mm) for x, mm in zip(l, m)]
            pb = [x.astype(BF16) for x in p]
            pcat = [jnp.concatenate([x[:BLOCK], x[BLOCK:]], axis=1) for x in pb]
            vcat = [jnp.concatenate([vlo_s[kr, :], vhi_s[kr, :]], axis=0) for kr in krows]
            acc = [jnp.dot(a, b, preferred_element_type=F32) for a, b in zip(pcat, vcat)]
            l_pair = [jnp.where(lo_q, x[:BLOCK], x[BLOCK:]) for x in l]
            for u in sub:
                dst = rows(start[u], BLOCK)
                if multi:
                    acc_s, m_s, l_s = stats[3 * pi:3 * pi + 3]
                    acc_s[dst, :] = acc[u]
                    m_s[dst, :] = jnp.where(lo_q, m[u][:BLOCK], m[u][BLOCK:])
                    l_s[dst, :] = l_pair[u]
                else:
                    o_ref[dst, :] = (acc[u] / l_pair[u]).astype(o_ref.dtype)
            return carry

        lax.fori_loop(0, span // (BLOCK * BANDED_GROUP), group, 0, unroll=BANDED_UNROLL)

    if multi:
        chunk = 2 * BLOCK

        def merge(c, carry):
            rs = pl.ds(pl.multiple_of(c * chunk, chunk), chunk)
            m_all = [stats[3 * pi + 1][rs, :] for pi in range(len(patterns))]
            m_top = functools.reduce(jnp.maximum, m_all)
            num = jnp.zeros((chunk, LANES), F32)
            den = jnp.zeros((chunk, LANES), F32)
            for pi in range(len(patterns)):
                w = jnp.exp2(m_all[pi] - m_top)
                num = num + w * stats[3 * pi][rs, :]
                den = den + w * stats[3 * pi + 2][rs, :]
            o_ref[rs, :] = (num / den).astype(o_ref.dtype)
            return carry

        lax.fori_loop(0, span // chunk, merge, 0)


def _banded_attention(q, k, v, batch, seq, n_pairs, kv_chunk_of_pair, patterns, sinks=None):
    n = q.shape[0]
    span = DIL_SPAN
    nblk = seq // span
    use_sink = sinks is not None

    def q_map(b, p, i):
        return (b * nblk + i, p)

    def kv_map(b, p, i):
        return (b * nblk + i, kv_chunk_of_pair(p))

    blk = (span, LANES)
    in_specs = [pl.BlockSpec(blk, q_map), pl.BlockSpec(blk, kv_map), pl.BlockSpec(blk, kv_map)]
    args = [q, k, v]
    if use_sink:
        in_specs = [pl.BlockSpec(memory_space=pltpu.SMEM)] + in_specs
        args = [sinks.astype(F32)] + args
    scratch = []
    for dil, _ in patterns:
        scratch += [pltpu.VMEM((dil * BLOCK + span, LANES), BF16)] * 3
    if any(dil > 1 for dil, _ in patterns):
        scratch += [pltpu.VMEM((span, LANES), k.dtype), pltpu.VMEM((span, LANES), v.dtype)]
    if len(patterns) > 1:
        scratch += [pltpu.VMEM((span, LANES), F32)] * (3 * len(patterns))
    return pl.pallas_call(
        functools.partial(_banded_kernel, patterns, use_sink, span),
        grid=(batch, n_pairs, nblk),
        in_specs=in_specs,
        out_specs=pl.BlockSpec(blk, q_map),
        out_shape=jax.ShapeDtypeStruct((n, n_pairs * LANES), BF16),
        scratch_shapes=scratch,
        compiler_params=_cparams(("arbitrary",) * 3),
        name="banded_attention_%d" % len(patterns),
    )(*args)


def _mla_kernel(tq, tk, q_ref, k_ref, v_ref, o_ref, *v_scr):
    qi = pl.program_id(2)
    n_heads = len(v_scr)
    lo_o = lax.broadcasted_iota(jnp.int32, (tq, LANES), 1) < MLA_V_DIM
    qs = [q_ref[:, e * LANES:(e + 1) * LANES] for e in range(n_heads)]

    @pl.when(qi == 0)
    def _():
        for pr in range(n_heads // 2):
            v = v_ref[:, pr * LANES:(pr + 1) * LANES]
            lo_v = lax.broadcasted_iota(jnp.int32, v.shape, 1) < MLA_V_DIM
            v_scr[2 * pr][...] = jnp.where(lo_v, v, jnp.ones_like(v))
            v_scr[2 * pr + 1][...] = jnp.where(lo_v, jnp.ones_like(v), v)

    def step(ks, width, carry, masked):
        out = []
        for e, v_s in enumerate(v_scr):
            m_old, acc = carry[2 * e:2 * e + 2]
            s = lax.dot_general(qs[e], k_ref[pl.ds(ks, width), e * LANES:(e + 1) * LANES],
                                (((1,), (1,)), ((), ())), preferred_element_type=F32)
            if masked:
                row_pos = qi * tq + lax.broadcasted_iota(jnp.int32, (tq, width), 0)
                col_pos = ks + lax.broadcasted_iota(jnp.int32, (tq, width), 1)
                s = jnp.where(row_pos >= col_pos, s, NEG)
            m_new = jnp.maximum(m_old, jnp.max(s, axis=1, keepdims=True))
            p = jnp.exp2(s - m_new).astype(BF16)
            pv = jnp.dot(p, v_s[pl.ds(ks, width), :], preferred_element_type=F32)
            out += [m_new, acc * jnp.exp2(m_old - m_new) + pv]
        return tuple(out)

    carry = (jnp.full((tq, 1), NEG, F32), jnp.zeros((tq, LANES), F32)) * n_heads
    q_start = qi * tq
    n_wide = q_start // tk
    sub = tk // MLA_SUBSTEPS

    def wide(kb, c):
        for u in range(MLA_SUBSTEPS):
            c = step(pl.multiple_of(kb * tk + u * sub, sub), sub, c, False)
        return c

    carry = lax.fori_loop(0, n_wide, wide, carry)
    def tail(widths):
        def run(c):
            if widths:
                c = step(pl.multiple_of(n_wide * tk, tk), widths * tq, c, False)
            return step(pl.multiple_of(q_start, tq), tq, c, True)
        return run

    carry = lax.switch(qi % (tk // tq), [tail(w) for w in range(tk // tq)], carry)
    for pr in range(n_heads // 2):
        acc_lo, acc_hi = carry[4 * pr + 1], carry[4 * pr + 3]
        acc = jnp.where(lo_o, acc_lo, acc_hi)
        den = jnp.where(lo_o, pltpu.roll(acc_lo, MLA_V_DIM, 1), pltpu.roll(acc_hi, MLA_V_DIM, 1))
        o_ref[:, pr * LANES:(pr + 1) * LANES] = (acc / den).astype(o_ref.dtype)


def _mla_attention(qb, kb, vb, batch, seq):
    n = qb.shape[0]
    tq, tk = MLA_TQ, MLA_TK
    nq = seq // tq
    pairs = MLA_PAIRS_PER_STEP
    return pl.pallas_call(
        functools.partial(_mla_kernel, tq, tk),
        grid=(batch, MLA_HEADS // (2 * pairs), nq),
        in_specs=[pl.BlockSpec((tq, 2 * pairs * LANES), lambda b, p, i: (b * nq + i, p)),
                  pl.BlockSpec((seq, 2 * pairs * LANES), lambda b, p, i: (b, p)),
                  pl.BlockSpec((seq, pairs * LANES), lambda b, p, i: (b, p))],
        out_specs=pl.BlockSpec((tq, pairs * LANES), lambda b, p, i: (b * nq + i, p)),
        out_shape=jax.ShapeDtypeStruct((n, MLA_HEADS // 2 * LANES), BF16),
        scratch_shapes=[pltpu.VMEM((seq, LANES), BF16)] * (2 * pairs),
        compiler_params=_cparams(("arbitrary",) * 3),
        name="mla_attention",
    )(qb, kb, vb)


X_Q_LOG2_SCALE = X_HEAD_DIM ** -0.5 * LOG2E


def _mixout_xattn_kernel(n_in, *refs):
    x_ref = refs[0]
    a_refs = refs[1:1 + n_in]
    w_refs = refs[1 + n_in:1 + 2 * n_in]
    g_ref, wq_ref, kv_ref, wo_ref, o_ref = refs[1 + 2 * n_in:]
    x1 = x_ref[...]
    for a_ref, w_ref in zip(a_refs, w_refs):
        x1 = x1 + jnp.dot(a_ref[...], w_ref[...], preferred_element_type=F32)
    h = _rms(x1, g_ref[...]).astype(BF16)
    xq = (jnp.dot(h, wq_ref[...], preferred_element_type=F32) * X_Q_LOG2_SCALE).astype(BF16)
    heads = []
    for hd in range(X_HEADS):
        q = xq[:, hd * X_HEAD_DIM:(hd + 1) * X_HEAD_DIM]
        k = kv_ref[:, hd * X_HEAD_DIM:(hd + 1) * X_HEAD_DIM]
        v = kv_ref[:, (X_HEADS + hd) * X_HEAD_DIM:(X_HEADS + hd + 1) * X_HEAD_DIM]
        s = lax.dot_general(q, k, (((1,), (1,)), ((), ())), preferred_element_type=F32)
        p = jnp.exp2(s - jnp.max(s, axis=1, keepdims=True))
        l = jnp.sum(p, axis=1, keepdims=True)
        o = jnp.dot(p.astype(BF16), v, preferred_element_type=F32) / l
        heads.append(o.astype(BF16))
    o_all = jnp.concatenate(heads, axis=1)
    o_ref[...] = x1 + jnp.dot(o_all, wo_ref[...], preferred_element_type=F32)


def _mixout_xattn(x2, acts, weights, g_x, w_xq, kv, w_xo, seq):
    n = x2.shape[0]
    tm = ROW_TILE
    per_batch = seq // tm
    xq_w = X_HEADS * X_HEAD_DIM

    def row(w):
        return pl.BlockSpec((tm, w), lambda i: (i, 0))

    weights = [w.astype(BF16) for w in weights]
    return pl.pallas_call(
        functools.partial(_mixout_xattn_kernel, len(acts)),
        grid=(n // tm,),
        in_specs=[row(D_MODEL)] + [row(a.shape[1]) for a in acts] + [_const_spec(w.shape) for w in weights]
        + [_const_spec((1, D_MODEL)), _const_spec((D_MODEL, xq_w)),
           pl.BlockSpec((MEM_LEN, 2 * xq_w), lambda i: (i // per_batch, 0)), _const_spec((xq_w, D_MODEL))],
        out_specs=row(D_MODEL),
        out_shape=jax.ShapeDtypeStruct((n, D_MODEL), F32),
        compiler_params=_cparams(("arbitrary",)),
        name="mixout_xattn",
    )(x2, *acts, *weights, g_x[None, :], w_xq.astype(BF16), kv, w_xo.astype(BF16))


def _memkv_kernel(mem_ref, g_ref, w_ref, kv_ref):
    h = _rms(mem_ref[...], g_ref[...]).astype(BF16)
    kv_ref[...] = jnp.dot(h, w_ref[...], preferred_element_type=F32).astype(kv_ref.dtype)


def _memkv(mem2, g, w_xkv):
    n = mem2.shape[0]
    cols = 2 * X_HEADS * X_HEAD_DIM
    return pl.pallas_call(
        _memkv_kernel,
        grid=(n // MEM_LEN,),
        in_specs=[pl.BlockSpec((MEM_LEN, D_MODEL), lambda i: (i, 0)), _const_spec((1, D_MODEL)),
                  _const_spec((D_MODEL, cols))],
        out_specs=pl.BlockSpec((MEM_LEN, cols), lambda i: (i, 0)),
        out_shape=jax.ShapeDtypeStruct((n, cols), BF16),
        compiler_params=_cparams(("arbitrary",)),
        name="memkv",
    )(mem2, g[None, :], w_xkv.astype(BF16))


def _ffn_kernel(final, x_ref, g_ref, wg_ref, wu_ref, wd_ref, *rest):
    if final:
        gf_ref, o_ref = rest
    else:
        (o_ref,) = rest
    x = x_ref[...]
    h = _rms(x, g_ref[...]).astype(BF16)
    width = FFN_HIDDEN // FFN_CHUNKS
    acc = x
    for c in range(FFN_CHUNKS):
        sl = slice(c * width, (c + 1) * width)
        gate = jnp.dot(h, wg_ref[:, sl], preferred_element_type=F32)
        up = jnp.dot(h, wu_ref[:, sl], preferred_element_type=F32)
        act = (gate * jax.nn.sigmoid(gate) * up).astype(BF16)
        acc = acc + jnp.dot(act, wd_ref[sl, :], preferred_element_type=F32)
    o_ref[...] = _rms(acc, gf_ref[...]) if final else acc


def _ffn(x2, g, w_gate, w_up, w_down, g_final=None):
    n = x2.shape[0]
    tm = ROW_TILE
    final = g_final is not None
    row = pl.BlockSpec((tm, D_MODEL), lambda i: (i, 0))

    def weight(shape):
        return pl.BlockSpec(shape, lambda i: (0, 0), pipeline_mode=pl.Buffered(1))

    in_specs = [row, _const_spec((1, D_MODEL)), weight((D_MODEL, FFN_HIDDEN)), weight((D_MODEL, FFN_HIDDEN)),
                weight((FFN_HIDDEN, D_MODEL))]
    args = [x2, g[None, :], w_gate.astype(BF16), w_up.astype(BF16), w_down.astype(BF16)]
    if final:
        in_specs.append(_const_spec((1, D_MODEL)))
        args.append(g_final[None, :])
    return pl.pallas_call(
        functools.partial(_ffn_kernel, final),
        grid=(n // tm,),
        in_specs=in_specs,
        out_specs=row,
        out_shape=jax.ShapeDtypeStruct((n, D_MODEL), F32),
        compiler_params=_cparams(("arbitrary",)),
        name="ffn",
    )(*args)


def kernel(x, mem, positions, l0_mix_norm, l0_w_in, l0_sinks, l0_q_norm, l0_w_uq, l0_kv_norm, l0_w_ukv, l0_w_out, l0_x_norm, l0_mem_norm, l0_w_xq, l0_w_xkv, l0_w_xo, l0_ffn_norm, l0_w_gate, l0_w_up, l0_w_down, l1_mix_norm, l1_w_qkv, l1_w_out, l1_x_norm, l1_mem_norm, l1_w_xq, l1_w_xkv, l1_w_xo, l1_ffn_norm, l1_w_gate, l1_w_up, l1_w_down, final_norm):
    batch, seq, _ = x.shape
    assert seq % DIL_SPAN == 0 and seq % MLA_TQ == 0 and seq % ROW_TILE == 0
    n = batch * seq
    x2 = x.reshape(n, D_MODEL)
    mem2 = mem.reshape(batch * MEM_LEN, D_MODEL)
    tables = _rope_tables(positions)

    qa, ka, va, qb, kb, vb = _l0_proj(x2, l0_mix_norm, l0_w_in, l0_q_norm, l0_w_uq, l0_kv_norm, l0_w_ukv, tables)
    oa = _banded_attention(qa, ka, va, batch, seq, SWA_HEADS // 2, lambda p: p // 2,
                           ((1, SWA_WINDOW - 1),), sinks=l0_sinks)
    ob = _mla_attention(qb, kb, vb, batch, seq)
    x2 = _mixout_xattn(x2, [oa, ob], [l0_w_out[:A_Q], l0_w_out[A_Q:]], l0_x_norm, l0_w_xq,
                       _memkv(mem2, l0_mem_norm, l0_w_xkv), l0_w_xo, seq)
    x2 = _ffn(x2, l0_ffn_norm, l0_w_gate, l0_w_up, l0_w_down)

    q, k, v = _l1_proj(x2, l1_mix_norm, l1_w_qkv, tables)
    od = _banded_attention(q, k, v, batch, seq, DIL_HEADS // 2, lambda p: p,
                           tuple((dil, window // dil) for window, dil in DIL_PATTERNS))
    x2 = _mixout_xattn(x2, [od], [l1_w_out], l1_x_norm, l1_w_xq,
                       _memkv(mem2, l1_mem_norm, l1_w_xkv), l1_w_xo, seq)
    x2 = _ffn(x2, l1_ffn_norm, l1_w_gate, l1_w_up, l1_w_down, g_final=final_norm)
    return x2.reshape(batch, seq, D_MODEL)
```

```python
import functools

import jax
import jax.numpy as jnp
from jax import lax
from jax.experimental import pallas as pl
from jax.experimental.pallas import tpu as pltpu

D_MODEL = 1024
MEM_LEN = 256
HEAD_DIM = 64
ROPE_THETA = 10000.0
NORM_EPS = 1e-6
BLOCK = 128
SWA_HEADS = 8
SWA_KV_HEADS = 2
SWA_WINDOW = 128
MLA_HEADS = 8
MLA_Q_RANK = 384
MLA_KV_RANK = 256
MLA_NOPE_DIM = 64
MLA_ROPE_DIM = 32
MLA_V_DIM = 64
A_Q = SWA_HEADS * HEAD_DIM
A_KV = SWA_KV_HEADS * HEAD_DIM
DIL_HEADS = D_MODEL // HEAD_DIM
DIL_PATTERNS = ((128, 1), (512, 4), (2048, 16))
X_HEADS = 4
X_HEAD_DIM = 128
FFN_HIDDEN = -(-8 * D_MODEL // (3 * 256)) * 256

LANES = 128
V7X_VMEM_LIMIT = 56 * 1024 * 1024
NEG = -1e30
LOG2E = 1.4426950408889634

ROW_TILE = 512
DIL_SPAN = 2048
STAGE_DIL = 4
BANDED_GROUP = 4
MLA_TQ = 512
MLA_TK = 2048
MLA_PAIRS_PER_STEP = 2
FFN_CHUNKS = 2

F32 = jnp.float32
BF16 = jnp.bfloat16


def _cparams(sem, flags=None):
    return pltpu.CompilerParams(dimension_semantics=sem, vmem_limit_bytes=V7X_VMEM_LIMIT, flags=flags)


def _const_spec(shape):
    return pl.BlockSpec(shape, lambda *_: (0,) * len(shape))


def _rms(x, g):
    return x * lax.rsqrt(jnp.mean(x * x, axis=-1, keepdims=True) + NORM_EPS) * g


def _rope_chunk(xc, c, s, half):
    lane = lax.broadcasted_iota(jnp.int32, xc.shape, 1)
    up = pltpu.roll(xc, half, 1)
    down = pltpu.roll(xc, LANES - half, 1)
    partner = jnp.where((lane & (2 * half - 1)) < half, down, up)
    return xc * c + partner * s


def _tables_kernel(pos_ref, freq_ref, sg64_ref, sg32_ref, on32_ref, c64_ref, s64_ref, c32_ref, s32_ref):
    ang = pos_ref[...].astype(F32) * freq_ref[...]
    lane = lax.broadcasted_iota(jnp.int32, ang.shape, 1)
    half64, half32 = HEAD_DIM // 2, MLA_ROPE_DIM // 2
    on = on32_ref[...] > 0.5
    for t, sign64, sign32, off32, o64_ref, o32_ref in ((jnp.cos(ang), None, None, 1.0, c64_ref, c32_ref),
                                                       (jnp.sin(ang), sg64_ref, sg32_ref, 0.0, s64_ref, s32_ref)):
        b64 = jnp.where(lane < half64, t, 0.0)
        b64 = b64 + pltpu.roll(b64, half64, 1)
        b64 = b64 + pltpu.roll(b64, 2 * half64, 1)
        b32 = jnp.where((lane >= half64) & (lane < half64 + half32), t, 0.0)
        b32 = pltpu.roll(b32, MLA_NOPE_DIM - half64, 1) + pltpu.roll(b32, MLA_NOPE_DIM - half64 + half32, 1)
        if sign64 is not None:
            b64 = b64 * sign64[...]
            b32 = b32 * sign32[...]
        o64_ref[...] = b64
        o32_ref[...] = jnp.where(on, b32, off32)


def _rope_tables(positions):
    n = positions.size
    tm = 2048
    lane = jnp.arange(LANES)
    half64, half32 = HEAD_DIM // 2, MLA_ROPE_DIM // 2
    inv64 = ROPE_THETA ** (-jnp.arange(0, HEAD_DIM, 2, dtype=F32) / HEAD_DIM)
    inv32 = ROPE_THETA ** (-jnp.arange(0, MLA_ROPE_DIM, 2, dtype=F32) / MLA_ROPE_DIM)
    freq = jnp.zeros((LANES,), F32).at[:half64].set(inv64).at[half64:half64 + half32].set(inv32)[None, :]
    sg64 = jnp.where(lane % HEAD_DIM < half64, -1.0, 1.0).astype(F32)[None, :]
    rl = lane - MLA_NOPE_DIM
    on32 = ((rl >= 0) & (rl < MLA_ROPE_DIM)).astype(F32)[None, :]
    sg32 = jnp.where(rl % MLA_ROPE_DIM < half32, -1.0, 1.0).astype(F32)[None, :]
    row = pl.BlockSpec((tm, LANES), lambda i: (i, 0))
    out = jax.ShapeDtypeStruct((n, LANES), F32)
    return pl.pallas_call(
        _tables_kernel,
        grid=(n // tm,),
        in_specs=[pl.BlockSpec((tm, 1), lambda i: (i, 0))] + [_const_spec((1, LANES))] * 4,
        out_specs=[row] * 4,
        out_shape=[out] * 4,
        compiler_params=_cparams(("arbitrary",)),
        name="rope_tables",
    )(positions.reshape(n, 1), freq, sg64, sg32, on32)


L0_QA = 0
L0_KA = A_Q
L0_VA = L0_KA + 2 * A_KV
L0_CQ = L0_VA + 2 * A_KV
L0_CKV = L0_CQ + MLA_Q_RANK
L0_KR = L0_CKV + MLA_KV_RANK
L0_COLS = L0_KR + LANES
MLA_QK = MLA_HEADS * LANES
MLA_Q_LOG2_SCALE = (MLA_NOPE_DIM + MLA_ROPE_DIM) ** -0.5 * LOG2E
MLA_V = MLA_HEADS * MLA_V_DIM


def _l0_proj_kernel(x_ref, g_ref, win_ref, gq_ref, wuq_ref, gkv_ref, wukv_ref, c64_ref, s64_ref, c32_ref, s32_ref,
                    qa_ref, ka_ref, va_ref, qb_ref, kb_ref, vb_ref):
    h = _rms(x_ref[...], g_ref[...]).astype(BF16)
    z = jnp.dot(h, win_ref[...], preferred_element_type=F32)
    c64, s64, c32, s32 = c64_ref[...], s64_ref[...], c32_ref[...], s32_ref[...]
    qscale = HEAD_DIM ** -0.5 * LOG2E
    for c in range(A_Q // LANES):
        sl = slice(c * LANES, (c + 1) * LANES)
        qa_ref[:, sl] = (_rope_chunk(z[:, sl], c64, s64, HEAD_DIM // 2) * qscale).astype(qa_ref.dtype)
    for c in range(2 * A_KV // LANES):
        sl = slice(c * LANES, (c + 1) * LANES)
        ka_ref[:, sl] = _rope_chunk(z[:, L0_KA + c * LANES:L0_KA + (c + 1) * LANES], c64, s64,
                                    HEAD_DIM // 2).astype(ka_ref.dtype)
    va_ref[...] = z[:, L0_VA:L0_CQ].astype(va_ref.dtype)

    cq = _rms(z[:, L0_CQ:L0_CKV], gq_ref[...]).astype(BF16)
    qb = jnp.dot(cq, wuq_ref[...], preferred_element_type=F32)
    ckv = _rms(z[:, L0_CKV:L0_KR], gkv_ref[...]).astype(BF16)
    kv = jnp.dot(ckv, wukv_ref[...], preferred_element_type=F32)
    kr = _rope_chunk(z[:, L0_KR:L0_COLS], c32, s32, MLA_ROPE_DIM // 2)
    for hd in range(MLA_HEADS):
        sl = slice(hd * LANES, (hd + 1) * LANES)
        qb_ref[:, sl] = (_rope_chunk(qb[:, sl], c32, s32, MLA_ROPE_DIM // 2) * MLA_Q_LOG2_SCALE).astype(qb_ref.dtype)
        kb_ref[:, sl] = (kv[:, sl] + kr).astype(kb_ref.dtype)
    vb_ref[...] = kv[:, MLA_QK:].astype(vb_ref.dtype)


def _l0_proj(x2, g, w_in, g_q, w_uq, g_kv, w_ukv, tables):
    n = x2.shape[0]
    tm = ROW_TILE
    qa_w, ka_w, va_w, cq_w, ckv_w, kr_w = jnp.split(
        w_in, [A_Q, A_Q + A_KV, A_Q + 2 * A_KV, A_Q + 2 * A_KV + MLA_Q_RANK,
               A_Q + 2 * A_KV + MLA_Q_RANK + MLA_KV_RANK], axis=1)

    def dup(w):
        return jnp.repeat(w.reshape(D_MODEL, SWA_KV_HEADS, 1, HEAD_DIM), 2, axis=2).reshape(D_MODEL, 2 * A_KV)

    kr_pad = jnp.pad(kr_w, ((0, 0), (MLA_NOPE_DIM, LANES - MLA_NOPE_DIM - MLA_ROPE_DIM)))
    win_p = jnp.concatenate([qa_w, dup(ka_w), dup(va_w), cq_w, ckv_w, kr_pad], axis=1).astype(BF16)
    per_head_q = MLA_NOPE_DIM + MLA_ROPE_DIM
    wuq_p = jnp.pad(w_uq.reshape(MLA_Q_RANK, MLA_HEADS, per_head_q),
                    ((0, 0), (0, 0), (0, LANES - per_head_q))).reshape(MLA_Q_RANK, MLA_QK).astype(BF16)
    wukv3 = w_ukv.reshape(MLA_KV_RANK, MLA_HEADS, MLA_NOPE_DIM + MLA_V_DIM)
    wk_p = jnp.pad(wukv3[:, :, :MLA_NOPE_DIM], ((0, 0), (0, 0), (0, LANES - MLA_NOPE_DIM))).reshape(MLA_KV_RANK, MLA_QK)
    wv_p = wukv3[:, :, MLA_NOPE_DIM:].reshape(MLA_KV_RANK, MLA_V)
    wukv_p = jnp.concatenate([wk_p, wv_p], axis=1).astype(BF16)

    def row(w):
        return pl.BlockSpec((tm, w), lambda i: (i, 0))

    outs = [(A_Q, BF16), (2 * A_KV, BF16), (2 * A_KV, BF16), (MLA_QK, BF16), (MLA_QK, BF16), (MLA_V, BF16)]
    return pl.pallas_call(
        _l0_proj_kernel,
        grid=(n // tm,),
        in_specs=[row(D_MODEL), _const_spec((1, D_MODEL)), _const_spec(win_p.shape),
                  _const_spec((1, MLA_Q_RANK)), _const_spec(wuq_p.shape),
                  _const_spec((1, MLA_KV_RANK)), _const_spec(wukv_p.shape)] + [row(LANES)] * 4,
        out_specs=[row(w) for w, _ in outs],
        out_shape=[jax.ShapeDtypeStruct((n, w), dt) for w, dt in outs],
        compiler_params=_cparams(("arbitrary",)),
        name="l0_proj",
    )(x2, g[None, :], win_p, g_q[None, :], wuq_p, g_kv[None, :], wukv_p, *tables)


def _l1_proj_kernel(x_ref, g_ref, w_ref, c64_ref, s64_ref, q_ref, k_ref, v_ref):
    h = _rms(x_ref[...], g_ref[...]).astype(BF16)
    z = jnp.dot(h, w_ref[...], preferred_element_type=F32)
    c64, s64 = c64_ref[...], s64_ref[...]
    qscale = HEAD_DIM ** -0.5 * LOG2E
    for c in range(D_MODEL // LANES):
        sl = slice(c * LANES, (c + 1) * LANES)
        q_ref[:, sl] = _rope_chunk(z[:, sl], c64, s64, HEAD_DIM // 2) * qscale
        k_ref[:, sl] = _rope_chunk(z[:, D_MODEL + c * LANES:D_MODEL + (c + 1) * LANES], c64, s64, HEAD_DIM // 2)
    v_ref[...] = z[:, 2 * D_MODEL:]


def _l1_proj(x2, g, w_qkv, tables):
    n = x2.shape[0]
    tm = ROW_TILE
    row = pl.BlockSpec((tm, D_MODEL), lambda i: (i, 0))
    tab = pl.BlockSpec((tm, LANES), lambda i: (i, 0))
    out = jax.ShapeDtypeStruct((n, D_MODEL), F32)
    return pl.pallas_call(
        _l1_proj_kernel,
        grid=(n // tm,),
        in_specs=[row, _const_spec((1, D_MODEL)), _const_spec(w_qkv.shape), tab, tab],
        out_specs=[row] * 3,
        out_shape=[out] * 3,
        compiler_params=_cparams(("arbitrary",)),
        name="l1_proj",
    )(x2, g[None, :], w_qkv.astype(BF16), tables[0], tables[1])


def _banded_kernel(patterns, use_sink, span, *refs):
    refs = list(refs)
    sinks_ref = refs.pop(0) if use_sink else None
    q_ref, k_ref, v_ref, o_ref = refs[:4]
    kv_scr = refs[4:4 + 3 * len(patterns)]
    rest = refs[4 + 3 * len(patterns):]
    staged = any(dil > 1 for dil, _ in patterns)
    k4_s, v4_s = rest[:2] if staged else (None, None)
    stats = rest[2:] if staged else rest
    multi = len(patterns) > 1
    pair = pl.program_id(1)
    blk = pl.program_id(2)

    for first in (True, False):
        @pl.when(blk == 0 if first else blk > 0)
        def _(first=first):
            for pi, (dil, _) in enumerate(patterns):
                stream_len = BLOCK + span // dil
                for r in range(dil):
                    head = slice(r * stream_len, r * stream_len + BLOCK)
                    tail = slice((r + 1) * stream_len - BLOCK, (r + 1) * stream_len)
                    for ref in kv_scr[3 * pi:3 * pi + 3]:
                        ref[head, :] = jnp.zeros((BLOCK, LANES), BF16) if first else ref[tail, :]

    def put(pi, dst, count, k_rows, v_rows):
        k_s, vlo_s, vhi_s = kv_scr[3 * pi:3 * pi + 3]
        k_s[dst:dst + count, :] = k_rows.astype(BF16)
        v_rows = v_rows.astype(BF16)
        lo = lax.broadcasted_iota(jnp.int32, (count, LANES), 1) < HEAD_DIM
        vlo_s[dst:dst + count, :] = jnp.where(lo, v_rows, jnp.zeros_like(v_rows))
        vhi_s[dst:dst + count, :] = jnp.where(lo, jnp.zeros_like(v_rows), v_rows)

    quarter = span // STAGE_DIL
    if staged:
        for c in range(STAGE_DIL):
            k4_s[c * quarter:(c + 1) * quarter, :] = k_ref[pl.ds(c, quarter, stride=STAGE_DIL), :]
            v4_s[c * quarter:(c + 1) * quarter, :] = v_ref[pl.ds(c, quarter, stride=STAGE_DIL), :]
    for pi, (dil, _) in enumerate(patterns):
        cur_len = span // dil
        stream_len = BLOCK + cur_len
        if dil == 1:
            put(pi, BLOCK, span, k_ref[...], v_ref[...])
        elif dil == STAGE_DIL:
            for c in range(dil):
                src = slice(c * quarter, (c + 1) * quarter)
                put(pi, c * stream_len + BLOCK, cur_len, k4_s[src, :], v4_s[src, :])
        else:
            assert dil == STAGE_DIL * STAGE_DIL
            for r in range(dil):
                src = pl.ds((r % STAGE_DIL) * quarter + r // STAGE_DIL, cur_len, stride=STAGE_DIL)
                put(pi, r * stream_len + BLOCK, cur_len, k4_s[src, :], v4_s[src, :])

    lo_q = lax.broadcasted_iota(jnp.int32, (BLOCK, LANES), 1) < HEAD_DIM
    qi = lax.broadcasted_iota(jnp.int32, (2 * BLOCK, 2 * BLOCK), 0) & (BLOCK - 1)
    kj = lax.broadcasted_iota(jnp.int32, (2 * BLOCK, 2 * BLOCK), 1)
    dist = BLOCK + qi - kj
    in_cur = kj >= BLOCK
    if use_sink:
        top_rows = lax.broadcasted_iota(jnp.int32, (2 * BLOCK, 1), 0) < BLOCK
        sink2 = jnp.where(top_rows, sinks_ref[2 * pair], sinks_ref[2 * pair + 1]) * LOG2E

    def merge(rs):
        m_all = [stats[3 * pi + 1][rs, :] for pi in range(len(patterns))]
        m_top = functools.reduce(jnp.maximum, m_all)
        num = den = None
        for pi in range(len(patterns)):
            w = jnp.exp2(m_all[pi] - m_top)
            num = w * stats[3 * pi][rs, :] if num is None else num + w * stats[3 * pi][rs, :]
            den = w * stats[3 * pi + 2][rs, :] if den is None else den + w * stats[3 * pi + 2][rs, :]
        o_ref[rs, :] = (num / den).astype(o_ref.dtype)

    assert patterns[-1][0] == 1
    for pi, (dil, max_dist) in enumerate(patterns):
        per_stream = span // (BLOCK * dil)
        stream_len = BLOCK + span // dil
        band = (dist >= 0) & (dist <= max_dist)
        bias_full = jnp.where(band, 0.0, NEG)
        bias_first = jnp.where(blk > 0, bias_full, jnp.where(band & in_cur, 0.0, NEG))
        k_s, vlo_s, vhi_s = kv_scr[3 * pi:3 * pi + 3]

        def rows(start, count, dil=dil):
            return pl.ds(start, count) if dil == 1 else pl.ds(start, count, stride=dil)

        for gi in range(span // (BLOCK * BANDED_GROUP)):
            sub = range(BANDED_GROUP)
            idx = [gi * BANDED_GROUP + u for u in sub]
            j = [i % per_stream for i in idx]
            stream = [i // per_stream for i in idx]
            start = [r + dil * BLOCK * jj for r, jj in zip(stream, j)]
            krows = [pl.ds(r * stream_len + BLOCK * jj, 2 * BLOCK) for r, jj in zip(stream, j)]
            q = [q_ref[rows(st, BLOCK), :].astype(BF16) for st in start]
            q2 = [jnp.concatenate([jnp.where(lo_q, x, jnp.zeros_like(x)), jnp.where(lo_q, jnp.zeros_like(x), x)],
                                  axis=0) for x in q]
            s = [lax.dot_general(a, k_s[kr, :], (((1,), (1,)), ((), ())), preferred_element_type=F32)
                 for a, kr in zip(q2, krows)]
            s = [x + (bias_full if jj > 0 else bias_first) for x, jj in zip(s, j)]
            m = [jnp.max(x, axis=1, keepdims=True) for x in s]
            if use_sink:
                m = [jnp.maximum(x, sink2) for x in m]
            p = [jnp.exp2(x - mm) for x, mm in zip(s, m)]
            l = [jnp.sum(x, axis=1, keepdims=True) for x in p]
            if use_sink:
                l = [x + jnp.exp2(sink2 - mm) for x, mm in zip(l, m)]
            pb = [x.astype(BF16) for x in p]
            pcat = [jnp.concatenate([x[:BLOCK], x[BLOCK:]], axis=1) for x in pb]
            vcat = [jnp.concatenate([vlo_s[kr, :], vhi_s[kr, :]], axis=0) for kr in krows]
            acc = [jnp.dot(a, b, preferred_element_type=F32) for a, b in zip(pcat, vcat)]
            l_pair = [jnp.where(lo_q, x[:BLOCK], x[BLOCK:]) for x in l]
            for u in sub:
                dst = rows(start[u], BLOCK)
                if multi:
                    acc_s, m_s, l_s = stats[3 * pi:3 * pi + 3]
                    acc_s[dst, :] = acc[u]
                    m_s[dst, :] = jnp.where(lo_q, m[u][:BLOCK], m[u][BLOCK:])
                    l_s[dst, :] = l_pair[u]
                else:
                    o_ref[dst, :] = (acc[u] / l_pair[u]).astype(o_ref.dtype)
            if multi and pi == len(patterns) - 1:
                for u in sub:
                    merge(pl.ds(start[u], BLOCK))


def _banded_attention(q, k, v, batch, seq, n_pairs, kv_chunk_of_pair, patterns, sinks=None):
    n = q.shape[0]
    span = DIL_SPAN
    nblk = seq // span
    use_sink = sinks is not None

    def q_map(b, p, i):
        return (b * nblk + i, p)

    def kv_map(b, p, i):
        return (b * nblk + i, kv_chunk_of_pair(p))

    blk = (span, LANES)
    in_specs = [pl.BlockSpec(blk, q_map), pl.BlockSpec(blk, kv_map), pl.BlockSpec(blk, kv_map)]
    args = [q, k, v]
    if use_sink:
        in_specs = [pl.BlockSpec(memory_space=pltpu.SMEM)] + in_specs
        args = [sinks.astype(F32)] + args
    scratch = []
    for dil, _ in patterns:
        scratch += [pltpu.VMEM((dil * BLOCK + span, LANES), BF16)] * 3
    if any(dil > 1 for dil, _ in patterns):
        scratch += [pltpu.VMEM((span, LANES), k.dtype), pltpu.VMEM((span, LANES), v.dtype)]
    if len(patterns) > 1:
        scratch += [pltpu.VMEM((span, LANES), F32)] * (3 * len(patterns))
    return pl.pallas_call(
        functools.partial(_banded_kernel, patterns, use_sink, span),
        grid=(batch, n_pairs, nblk),
        in_specs=in_specs,
        out_specs=pl.BlockSpec(blk, q_map),
        out_shape=jax.ShapeDtypeStruct((n, n_pairs * LANES), BF16),
        scratch_shapes=scratch,
        compiler_params=_cparams(("arbitrary",) * 3),
        name="banded_attention_%d" % len(patterns),
    )(*args)


def _mla_kernel(tq, tk, q_ref, k_ref, v_ref, o_ref, *v_scr):
    qi = pl.program_id(2)
    n_heads = len(v_scr)
    lo_o = lax.broadcasted_iota(jnp.int32, (tq, LANES), 1) < MLA_V_DIM
    qs = [q_ref[:, e * LANES:(e + 1) * LANES] for e in range(n_heads)]

    @pl.when(qi == 0)
    def _():
        for pr in range(n_heads // 2):
            v = v_ref[:, pr * LANES:(pr + 1) * LANES]
            lo_v = lax.broadcasted_iota(jnp.int32, v.shape, 1) < MLA_V_DIM
            v_scr[2 * pr][...] = jnp.where(lo_v, v, jnp.ones_like(v))
            v_scr[2 * pr + 1][...] = jnp.where(lo_v, jnp.ones_like(v), v)

    def step(ks, width, carry, masked):
        out = []
        for e, v_s in enumerate(v_scr):
            m_old, acc = carry[2 * e:2 * e + 2]
            s = lax.dot_general(qs[e], k_ref[pl.ds(ks, width), e * LANES:(e + 1) * LANES],
                                (((1,), (1,)), ((), ())), preferred_element_type=F32)
            if masked:
                row_pos = qi * tq + lax.broadcasted_iota(jnp.int32, (tq, width), 0)
                col_pos = ks + lax.broadcasted_iota(jnp.int32, (tq, width), 1)
                s = jnp.where(row_pos >= col_pos, s, NEG)
            m_new = jnp.maximum(m_old, jnp.max(s, axis=1, keepdims=True))
            p = jnp.exp2(s - m_new).astype(BF16)
            pv = jnp.dot(p, v_s[pl.ds(ks, width), :], preferred_element_type=F32)
            out += [m_new, acc * jnp.exp2(m_old - m_new) + pv]
        return tuple(out)

    carry = (jnp.full((tq, 1), NEG, F32), jnp.zeros((tq, LANES), F32)) * n_heads
    q_start = qi * tq
    n_wide = q_start // tk
    carry = lax.fori_loop(0, n_wide, lambda kb, c: step(pl.multiple_of(kb * tk, tk), tk, c, False), carry)

    def tail(widths):
        def run(c):
            if widths:
                c = step(pl.multiple_of(n_wide * tk, tk), widths * tq, c, False)
            return step(pl.multiple_of(q_start, tq), tq, c, True)
        return run

    carry = lax.switch(qi % (tk // tq), [tail(w) for w in range(tk // tq)], carry)
    for pr in range(n_heads // 2):
        acc_lo, acc_hi = carry[4 * pr + 1], carry[4 * pr + 3]
        acc = jnp.where(lo_o, acc_lo, acc_hi)
        den = jnp.where(lo_o, pltpu.roll(acc_lo, MLA_V_DIM, 1), pltpu.roll(acc_hi, MLA_V_DIM, 1))
        o_ref[:, pr * LANES:(pr + 1) * LANES] = (acc / den).astype(o_ref.dtype)


def _mla_attention(qb, kb, vb, batch, seq):
    n = qb.shape[0]
    tq, tk = MLA_TQ, MLA_TK
    nq = seq // tq
    pairs = MLA_PAIRS_PER_STEP
    return pl.pallas_call(
        functools.partial(_mla_kernel, tq, tk),
        grid=(batch, MLA_HEADS // (2 * pairs), nq),
        in_specs=[pl.BlockSpec((tq, 2 * pairs * LANES), lambda b, p, i: (b * nq + i, p)),
                  pl.BlockSpec((seq, 2 * pairs * LANES), lambda b, p, i: (b, p)),
                  pl.BlockSpec((seq, pairs * LANES), lambda b, p, i: (b, p))],
        out_specs=pl.BlockSpec((tq, pairs * LANES), lambda b, p, i: (b * nq + i, p)),
        out_shape=jax.ShapeDtypeStruct((n, MLA_HEADS // 2 * LANES), BF16),
        scratch_shapes=[pltpu.VMEM((seq, LANES), BF16)] * (2 * pairs),
        compiler_params=_cparams(("arbitrary",) * 3),
        name="mla_attention",
    )(qb, kb, vb)


X_Q_LOG2_SCALE = X_HEAD_DIM ** -0.5 * LOG2E


def _mixout_xattn_kernel(n_in, *refs):
    x_ref = refs[0]
    a_refs = refs[1:1 + n_in]
    w_refs = refs[1 + n_in:1 + 2 * n_in]
    g_ref, wq_ref, kv_ref, wo_ref, o_ref = refs[1 + 2 * n_in:]
    x1 = x_ref[...]
    for a_ref, w_ref in zip(a_refs, w_refs):
        x1 = x1 + jnp.dot(a_ref[...], w_ref[...], preferred_element_type=F32)
    h = _rms(x1, g_ref[...]).astype(BF16)
    xq = (jnp.dot(h, wq_ref[...], preferred_element_type=F32) * X_Q_LOG2_SCALE).astype(BF16)
    heads = []
    for hd in range(X_HEADS):
        q = xq[:, hd * X_HEAD_DIM:(hd + 1) * X_HEAD_DIM]
        k = kv_ref[:, hd * X_HEAD_DIM:(hd + 1) * X_HEAD_DIM]
        v = kv_ref[:, (X_HEADS + hd) * X_HEAD_DIM:(X_HEADS + hd + 1) * X_HEAD_DIM]
        s = lax.dot_general(q, k, (((1,), (1,)), ((), ())), preferred_element_type=F32)
        p = jnp.exp2(s - jnp.max(s, axis=1, keepdims=True))
        l = jnp.sum(p, axis=1, keepdims=True)
        o = jnp.dot(p.astype(BF16), v, preferred_element_type=F32) / l
        heads.append(o.astype(BF16))
    o_all = jnp.concatenate(heads, axis=1)
    o_ref[...] = x1 + jnp.dot(o_all, wo_ref[...], preferred_element_type=F32)


def _mixout_xattn(x2, acts, weights, g_x, w_xq, kv, w_xo, seq):
    n = x2.shape[0]
    tm = ROW_TILE
    per_batch = seq // tm
    xq_w = X_HEADS * X_HEAD_DIM

    def row(w):
        return pl.BlockSpec((tm, w), lambda i: (i, 0))

    weights = [w.astype(BF16) for w in weights]
    return pl.pallas_call(
        functools.partial(_mixout_xattn_kernel, len(acts)),
        grid=(n // tm,),
        in_specs=[row(D_MODEL)] + [row(a.shape[1]) for a in acts] + [_const_spec(w.shape) for w in weights]
        + [_const_spec((1, D_MODEL)), _const_spec((D_MODEL, xq_w)),
           pl.BlockSpec((MEM_LEN, 2 * xq_w), lambda i: (i // per_batch, 0)), _const_spec((xq_w, D_MODEL))],
        out_specs=row(D_MODEL),
        out_shape=jax.ShapeDtypeStruct((n, D_MODEL), F32),
        compiler_params=_cparams(("arbitrary",)),
        name="mixout_xattn",
    )(x2, *acts, *weights, g_x[None, :], w_xq.astype(BF16), kv, w_xo.astype(BF16))


def _memkv_kernel(mem_ref, g_ref, w_ref, kv_ref):
    h = _rms(mem_ref[...], g_ref[...]).astype(BF16)
    kv_ref[...] = jnp.dot(h, w_ref[...], preferred_element_type=F32).astype(kv_ref.dtype)


def _memkv(mem2, g, w_xkv):
    n = mem2.shape[0]
    cols = 2 * X_HEADS * X_HEAD_DIM
    return pl.pallas_call(
        _memkv_kernel,
        grid=(n // MEM_LEN,),
        in_specs=[pl.BlockSpec((MEM_LEN, D_MODEL), lambda i: (i, 0)), _const_spec((1, D_MODEL)),
                  _const_spec((D_MODEL, cols))],
        out_specs=pl.BlockSpec((MEM_LEN, cols), lambda i: (i, 0)),
        out_shape=jax.ShapeDtypeStruct((n, cols), BF16),
        compiler_params=_cparams(("arbitrary",)),
        name="memkv",
    )(mem2, g[None, :], w_xkv.astype(BF16))


def _ffn_kernel(final, x_ref, g_ref, wg_ref, wu_ref, wd_ref, *rest):
    if final:
        gf_ref, o_ref = rest
    else:
        (o_ref,) = rest
    x = x_ref[...]
    h = _rms(x, g_ref[...]).astype(BF16)
    width = FFN_HIDDEN // FFN_CHUNKS
    acc = x
    for c in range(FFN_CHUNKS):
        sl = slice(c * width, (c + 1) * width)
        gate = jnp.dot(h, wg_ref[:, sl], preferred_element_type=F32)
        up = jnp.dot(h, wu_ref[:, sl], preferred_element_type=F32)
        act = (gate * jax.nn.sigmoid(gate) * up).astype(BF16)
        acc = acc + jnp.dot(act, wd_ref[sl, :], preferred_element_type=F32)
    o_ref[...] = _rms(acc, gf_ref[...]) if final else acc


def _ffn(x2, g, w_gate, w_up, w_down, g_final=None):
    n = x2.shape[0]
    tm = ROW_TILE
    final = g_final is not None
    row = pl.BlockSpec((tm, D_MODEL), lambda i: (i, 0))

    def weight(shape):
        return pl.BlockSpec(shape, lambda i: (0, 0), pipeline_mode=pl.Buffered(1))

    in_specs = [row, _const_spec((1, D_MODEL)), weight((D_MODEL, FFN_HIDDEN)), weight((D_MODEL, FFN_HIDDEN)),
                weight((FFN_HIDDEN, D_MODEL))]
    args = [x2, g[None, :], w_gate.astype(BF16), w_up.astype(BF16), w_down.astype(BF16)]
    if final:
        in_specs.append(_const_spec((1, D_MODEL)))
        args.append(g_final[None, :])
    return pl.pallas_call(
        functools.partial(_ffn_kernel, final),
        grid=(n // tm,),
        in_specs=in_specs,
        out_specs=row,
        out_shape=jax.ShapeDtypeStruct((n, D_MODEL), F32),
        compiler_params=_cparams(("arbitrary",)),
        name="ffn",
    )(*args)


def kernel(x, mem, positions, l0_mix_norm, l0_w_in, l0_sinks, l0_q_norm, l0_w_uq, l0_kv_norm, l0_w_ukv, l0_w_out, l0_x_norm, l0_mem_norm, l0_w_xq, l0_w_xkv, l0_w_xo, l0_ffn_norm, l0_w_gate, l0_w_up, l0_w_down, l1_mix_norm, l1_w_qkv, l1_w_out, l1_x_norm, l1_mem_norm, l1_w_xq, l1_w_xkv, l1_w_xo, l1_ffn_norm, l1_w_gate, l1_w_up, l1_w_down, final_norm):
    batch, seq, _ = x.shape
    assert seq % DIL_SPAN == 0 and seq % MLA_TQ == 0 and seq % ROW_TILE == 0
    n = batch * seq
    x2 = x.reshape(n, D_MODEL)
    mem2 = mem.reshape(batch * MEM_LEN, D_MODEL)
    tables = _rope_tables(positions)

    qa, ka, va, qb, kb, vb = _l0_proj(x2, l0_mix_norm, l0_w_in, l0_q_norm, l0_w_uq, l0_kv_norm, l0_w_ukv, tables)
    oa = _banded_attention(qa, ka, va, batch, seq, SWA_HEADS // 2, lambda p: p // 2,
                           ((1, SWA_WINDOW - 1),), sinks=l0_sinks)
    ob = _mla_attention(qb, kb, vb, batch, seq)
    x2 = _mixout_xattn(x2, [oa, ob], [l0_w_out[:A_Q], l0_w_out[A_Q:]], l0_x_norm, l0_w_xq,
                       _memkv(mem2, l0_mem_norm, l0_w_xkv), l0_w_xo, seq)
    x2 = _ffn(x2, l0_ffn_norm, l0_w_gate, l0_w_up, l0_w_down)

    q, k, v = _l1_proj(x2, l1_mix_norm, l1_w_qkv, tables)
    od = _banded_attention(q, k, v, batch, seq, DIL_HEADS // 2, lambda p: p,
                           tuple((dil, window // dil) for window, dil in reversed(DIL_PATTERNS)))
    x2 = _mixout_xattn(x2, [od], [l1_w_out], l1_x_norm, l1_w_xq,
                       _memkv(mem2, l1_mem_norm, l1_w_xkv), l1_w_xo, seq)
    x2 = _ffn(x2, l1_ffn_norm, l1_w_gate, l1_w_up, l1_w_down, g_final=final_norm)
    return x2.reshape(batch, seq, D_MODEL)
```

```python
import functools

import jax
import jax.numpy as jnp
from jax import lax
from jax.experimental import pallas as pl
from jax.experimental.pallas import tpu as pltpu

D_MODEL = 1024
MEM_LEN = 256
HEAD_DIM = 64
ROPE_THETA = 10000.0
NORM_EPS = 1e-6
BLOCK = 128
SWA_HEADS = 8
SWA_KV_HEADS = 2
SWA_WINDOW = 128
MLA_HEADS = 8
MLA_Q_RANK = 384
MLA_KV_RANK = 256
MLA_NOPE_DIM = 64
MLA_ROPE_DIM = 32
MLA_V_DIM = 64
A_Q = SWA_HEADS * HEAD_DIM
A_KV = SWA_KV_HEADS * HEAD_DIM
DIL_HEADS = D_MODEL // HEAD_DIM
DIL_PATTERNS = ((128, 1), (512, 4), (2048, 16))
X_HEADS = 4
X_HEAD_DIM = 128
FFN_HIDDEN = -(-8 * D_MODEL // (3 * 256)) * 256

LANES = 128
V7X_VMEM_LIMIT = 56 * 1024 * 1024
NEG = -1e30
LOG2E = 1.4426950408889634

ROW_TILE = 512
DIL_SPAN = 2048
STAGE_DIL = 4
BANDED_GROUP = 4
MLA_TQ = 512
MLA_TK = 2048
MLA_PAIRS_PER_STEP = 2
FFN_ROW_TILE = 1024
FFN_CHUNKS = 11

F32 = jnp.float32
BF16 = jnp.bfloat16


def _cparams(sem, flags=None):
    return pltpu.CompilerParams(dimension_semantics=sem, vmem_limit_bytes=V7X_VMEM_LIMIT, flags=flags)


def _const_spec(shape):
    return pl.BlockSpec(shape, lambda *_: (0,) * len(shape))


def _rms(x, g):
    return x * lax.rsqrt(jnp.mean(x * x, axis=-1, keepdims=True) + NORM_EPS) * g


def _rope_chunk(xc, c, s, half):
    lane = lax.broadcasted_iota(jnp.int32, xc.shape, 1)
    up = pltpu.roll(xc, half, 1)
    down = pltpu.roll(xc, LANES - half, 1)
    partner = jnp.where((lane & (2 * half - 1)) < half, down, up)
    return xc * c + partner * s


def _tables_kernel(pos_ref, freq_ref, sg64_ref, sg32_ref, on32_ref, c64_ref, s64_ref, c32_ref, s32_ref):
    ang = pos_ref[...].astype(F32) * freq_ref[...]
    lane = lax.broadcasted_iota(jnp.int32, ang.shape, 1)
    half64, half32 = HEAD_DIM // 2, MLA_ROPE_DIM // 2
    on = on32_ref[...] > 0.5
    for t, sign64, sign32, off32, o64_ref, o32_ref in ((jnp.cos(ang), None, None, 1.0, c64_ref, c32_ref),
                                                       (jnp.sin(ang), sg64_ref, sg32_ref, 0.0, s64_ref, s32_ref)):
        b64 = jnp.where(lane < half64, t, 0.0)
        b64 = b64 + pltpu.roll(b64, half64, 1)
        b64 = b64 + pltpu.roll(b64, 2 * half64, 1)
        b32 = jnp.where((lane >= half64) & (lane < half64 + half32), t, 0.0)
        b32 = pltpu.roll(b32, MLA_NOPE_DIM - half64, 1) + pltpu.roll(b32, MLA_NOPE_DIM - half64 + half32, 1)
        if sign64 is not None:
            b64 = b64 * sign64[...]
            b32 = b32 * sign32[...]
        o64_ref[...] = b64
        o32_ref[...] = jnp.where(on, b32, off32)


def _rope_tables(positions):
    n = positions.size
    tm = 2048
    lane = jnp.arange(LANES)
    half64, half32 = HEAD_DIM // 2, MLA_ROPE_DIM // 2
    inv64 = ROPE_THETA ** (-jnp.arange(0, HEAD_DIM, 2, dtype=F32) / HEAD_DIM)
    inv32 = ROPE_THETA ** (-jnp.arange(0, MLA_ROPE_DIM, 2, dtype=F32) / MLA_ROPE_DIM)
    freq = jnp.zeros((LANES,), F32).at[:half64].set(inv64).at[half64:half64 + half32].set(inv32)[None, :]
    sg64 = jnp.where(lane % HEAD_DIM < half64, -1.0, 1.0).astype(F32)[None, :]
    rl = lane - MLA_NOPE_DIM
    on32 = ((rl >= 0) & (rl < MLA_ROPE_DIM)).astype(F32)[None, :]
    sg32 = jnp.where(rl % MLA_ROPE_DIM < half32, -1.0, 1.0).astype(F32)[None, :]
    row = pl.BlockSpec((tm, LANES), lambda i: (i, 0))
    out = jax.ShapeDtypeStruct((n, LANES), F32)
    return pl.pallas_call(
        _tables_kernel,
        grid=(n // tm,),
        in_specs=[pl.BlockSpec((tm, 1), lambda i: (i, 0))] + [_const_spec((1, LANES))] * 4,
        out_specs=[row] * 4,
        out_shape=[out] * 4,
        compiler_params=_cparams(("arbitrary",)),
        name="rope_tables",
    )(positions.reshape(n, 1), freq, sg64, sg32, on32)


L0_QA = 0
L0_KA = A_Q
L0_VA = L0_KA + 2 * A_KV
L0_CQ = L0_VA + 2 * A_KV
L0_CKV = L0_CQ + MLA_Q_RANK
L0_KR = L0_CKV + MLA_KV_RANK
L0_COLS = L0_KR + LANES
MLA_QK = MLA_HEADS * LANES
MLA_Q_LOG2_SCALE = (MLA_NOPE_DIM + MLA_ROPE_DIM) ** -0.5 * LOG2E
MLA_V = MLA_HEADS * MLA_V_DIM


def _l0_proj_kernel(x_ref, g_ref, win_ref, gq_ref, wuq_ref, gkv_ref, wukv_ref, c64_ref, s64_ref, c32_ref, s32_ref,
                    qa_ref, ka_ref, va_ref, qb_ref, kb_ref, vb_ref):
    h = _rms(x_ref[...], g_ref[...]).astype(BF16)
    z = jnp.dot(h, win_ref[...], preferred_element_type=F32)
    c64, s64, c32, s32 = c64_ref[...], s64_ref[...], c32_ref[...], s32_ref[...]
    qscale = HEAD_DIM ** -0.5 * LOG2E
    for c in range(A_Q // LANES):
        sl = slice(c * LANES, (c + 1) * LANES)
        qa_ref[:, sl] = (_rope_chunk(z[:, sl], c64, s64, HEAD_DIM // 2) * qscale).astype(qa_ref.dtype)
    for c in range(2 * A_KV // LANES):
        sl = slice(c * LANES, (c + 1) * LANES)
        ka_ref[:, sl] = _rope_chunk(z[:, L0_KA + c * LANES:L0_KA + (c + 1) * LANES], c64, s64,
                                    HEAD_DIM // 2).astype(ka_ref.dtype)
    va_ref[...] = z[:, L0_VA:L0_CQ].astype(va_ref.dtype)

    cq = _rms(z[:, L0_CQ:L0_CKV], gq_ref[...]).astype(BF16)
    qb = jnp.dot(cq, wuq_ref[...], preferred_element_type=F32)
    ckv = _rms(z[:, L0_CKV:L0_KR], gkv_ref[...]).astype(BF16)
    kv = jnp.dot(ckv, wukv_ref[...], preferred_element_type=F32)
    kr = _rope_chunk(z[:, L0_KR:L0_COLS], c32, s32, MLA_ROPE_DIM // 2)
    for hd in range(MLA_HEADS):
        sl = slice(hd * LANES, (hd + 1) * LANES)
        qb_ref[:, sl] = (_rope_chunk(qb[:, sl], c32, s32, MLA_ROPE_DIM // 2) * MLA_Q_LOG2_SCALE).astype(qb_ref.dtype)
        kb_ref[:, sl] = (kv[:, sl] + kr).astype(kb_ref.dtype)
    vb_ref[...] = kv[:, MLA_QK:].astype(vb_ref.dtype)


def _l0_proj(x2, g, w_in, g_q, w_uq, g_kv, w_ukv, tables):
    n = x2.shape[0]
    tm = ROW_TILE
    qa_w, ka_w, va_w, cq_w, ckv_w, kr_w = jnp.split(
        w_in, [A_Q, A_Q + A_KV, A_Q + 2 * A_KV, A_Q + 2 * A_KV + MLA_Q_RANK,
               A_Q + 2 * A_KV + MLA_Q_RANK + MLA_KV_RANK], axis=1)

    def dup(w):
        return jnp.repeat(w.reshape(D_MODEL, SWA_KV_HEADS, 1, HEAD_DIM), 2, axis=2).reshape(D_MODEL, 2 * A_KV)

    kr_pad = jnp.pad(kr_w, ((0, 0), (MLA_NOPE_DIM, LANES - MLA_NOPE_DIM - MLA_ROPE_DIM)))
    win_p = jnp.concatenate([qa_w, dup(ka_w), dup(va_w), cq_w, ckv_w, kr_pad], axis=1).astype(BF16)
    per_head_q = MLA_NOPE_DIM + MLA_ROPE_DIM
    wuq_p = jnp.pad(w_uq.reshape(MLA_Q_RANK, MLA_HEADS, per_head_q),
                    ((0, 0), (0, 0), (0, LANES - per_head_q))).reshape(MLA_Q_RANK, MLA_QK).astype(BF16)
    wukv3 = w_ukv.reshape(MLA_KV_RANK, MLA_HEADS, MLA_NOPE_DIM + MLA_V_DIM)
    wk_p = jnp.pad(wukv3[:, :, :MLA_NOPE_DIM], ((0, 0), (0, 0), (0, LANES - MLA_NOPE_DIM))).reshape(MLA_KV_RANK, MLA_QK)
    wv_p = wukv3[:, :, MLA_NOPE_DIM:].reshape(MLA_KV_RANK, MLA_V)
    wukv_p = jnp.concatenate([wk_p, wv_p], axis=1).astype(BF16)

    def row(w):
        return pl.BlockSpec((tm, w), lambda i: (i, 0))

    outs = [(A_Q, BF16), (2 * A_KV, BF16), (2 * A_KV, BF16), (MLA_QK, BF16), (MLA_QK, BF16), (MLA_V, BF16)]
    return pl.pallas_call(
        _l0_proj_kernel,
        grid=(n // tm,),
        in_specs=[row(D_MODEL), _const_spec((1, D_MODEL)), _const_spec(win_p.shape),
                  _const_spec((1, MLA_Q_RANK)), _const_spec(wuq_p.shape),
                  _const_spec((1, MLA_KV_RANK)), _const_spec(wukv_p.shape)] + [row(LANES)] * 4,
        out_specs=[row(w) for w, _ in outs],
        out_shape=[jax.ShapeDtypeStruct((n, w), dt) for w, dt in outs],
        compiler_params=_cparams(("arbitrary",)),
        name="l0_proj",
    )(x2, g[None, :], win_p, g_q[None, :], wuq_p, g_kv[None, :], wukv_p, *tables)


def _l1_proj_kernel(x_ref, g_ref, w_ref, c64_ref, s64_ref, q_ref, k_ref, v_ref):
    h = _rms(x_ref[...], g_ref[...]).astype(BF16)
    z = jnp.dot(h, w_ref[...], preferred_element_type=F32)
    c64, s64 = c64_ref[...], s64_ref[...]
    qscale = HEAD_DIM ** -0.5 * LOG2E
    for c in range(D_MODEL // LANES):
        sl = slice(c * LANES, (c + 1) * LANES)
        q_ref[:, sl] = _rope_chunk(z[:, sl], c64, s64, HEAD_DIM // 2) * qscale
        k_ref[:, sl] = _rope_chunk(z[:, D_MODEL + c * LANES:D_MODEL + (c + 1) * LANES], c64, s64, HEAD_DIM // 2)
    v_ref[...] = z[:, 2 * D_MODEL:]


def _l1_proj(x2, g, w_qkv, tables):
    n = x2.shape[0]
    tm = ROW_TILE
    row = pl.BlockSpec((tm, D_MODEL), lambda i: (i, 0))
    tab = pl.BlockSpec((tm, LANES), lambda i: (i, 0))
    out = jax.ShapeDtypeStruct((n, D_MODEL), F32)
    return pl.pallas_call(
        _l1_proj_kernel,
        grid=(n // tm,),
        in_specs=[row, _const_spec((1, D_MODEL)), _const_spec(w_qkv.shape), tab, tab],
        out_specs=[row] * 3,
        out_shape=[out] * 3,
        compiler_params=_cparams(("arbitrary",)),
        name="l1_proj",
    )(x2, g[None, :], w_qkv.astype(BF16), tables[0], tables[1])


def _banded_kernel(patterns, use_sink, span, *refs):
    refs = list(refs)
    sinks_ref = refs.pop(0) if use_sink else None
    q_ref, k_ref, v_ref, o_ref = refs[:4]
    kv_scr = refs[4:4 + 3 * len(patterns)]
    rest = refs[4 + 3 * len(patterns):]
    staged = any(dil > 1 for dil, _ in patterns)
    k4_s, v4_s = rest[:2] if staged else (None, None)
    stats = rest[2:] if staged else rest
    multi = len(patterns) > 1
    pair = pl.program_id(1)
    blk = pl.program_id(2)

    for first in (True, False):
        @pl.when(blk == 0 if first else blk > 0)
        def _(first=first):
            for pi, (dil, _) in enumerate(patterns):
                stream_len = BLOCK + span // dil
                for r in range(dil):
                    head = slice(r * stream_len, r * stream_len + BLOCK)
                    tail = slice((r + 1) * stream_len - BLOCK, (r + 1) * stream_len)
                    for ref in kv_scr[3 * pi:3 * pi + 3]:
                        ref[head, :] = jnp.zeros((BLOCK, LANES), BF16) if first else ref[tail, :]

    def put(pi, dst, count, k_rows, v_rows):
        k_s, vlo_s, vhi_s = kv_scr[3 * pi:3 * pi + 3]
        k_s[dst:dst + count, :] = k_rows.astype(BF16)
        v_rows = v_rows.astype(BF16)
        lo = lax.broadcasted_iota(jnp.int32, (count, LANES), 1) < HEAD_DIM
        vlo_s[dst:dst + count, :] = jnp.where(lo, v_rows, jnp.zeros_like(v_rows))
        vhi_s[dst:dst + count, :] = jnp.where(lo, jnp.zeros_like(v_rows), v_rows)

    quarter = span // STAGE_DIL
    if staged:
        for c in range(STAGE_DIL):
            k4_s[c * quarter:(c + 1) * quarter, :] = k_ref[pl.ds(c, quarter, stride=STAGE_DIL), :]
            v4_s[c * quarter:(c + 1) * quarter, :] = v_ref[pl.ds(c, quarter, stride=STAGE_DIL), :]
    for pi, (dil, _) in enumerate(patterns):
        cur_len = span // dil
        stream_len = BLOCK + cur_len
        if dil == 1:
            put(pi, BLOCK, span, k_ref[...], v_ref[...])
        elif dil == STAGE_DIL:
            for c in range(dil):
                src = slice(c * quarter, (c + 1) * quarter)
                put(pi, c * stream_len + BLOCK, cur_len, k4_s[src, :], v4_s[src, :])
        else:
            assert dil == STAGE_DIL * STAGE_DIL
            for r in range(dil):
                src = pl.ds((r % STAGE_DIL) * quarter + r // STAGE_DIL, cur_len, stride=STAGE_DIL)
                put(pi, r * stream_len + BLOCK, cur_len, k4_s[src, :], v4_s[src, :])

    lo_q = lax.broadcasted_iota(jnp.int32, (BLOCK, LANES), 1) < HEAD_DIM
    qi = lax.broadcasted_iota(jnp.int32, (2 * BLOCK, 2 * BLOCK), 0) & (BLOCK - 1)
    kj = lax.broadcasted_iota(jnp.int32, (2 * BLOCK, 2 * BLOCK), 1)
    dist = BLOCK + qi - kj
    in_cur = kj >= BLOCK
    if use_sink:
        top_rows = lax.broadcasted_iota(jnp.int32, (2 * BLOCK, 1), 0) < BLOCK
        sink2 = jnp.where(top_rows, sinks_ref[2 * pair], sinks_ref[2 * pair + 1]) * LOG2E

    def merge(rs):
        m_all = [stats[3 * pi + 1][rs, :] for pi in range(len(patterns))]
        m_top = functools.reduce(jnp.maximum, m_all)
        num = den = None
        for pi in range(len(patterns)):
            w = jnp.exp2(m_all[pi] - m_top)
            num = w * stats[3 * pi][rs, :] if num is None else num + w * stats[3 * pi][rs, :]
            den = w * stats[3 * pi + 2][rs, :] if den is None else den + w * stats[3 * pi + 2][rs, :]
        o_ref[rs, :] = (num / den).astype(o_ref.dtype)

    assert patterns[-1][0] == 1
    for pi, (dil, max_dist) in enumerate(patterns):
        per_stream = span // (BLOCK * dil)
        stream_len = BLOCK + span // dil
        band = (dist >= 0) & (dist <= max_dist)
        bias_full = jnp.where(band, 0.0, NEG)
        bias_first = jnp.where(blk > 0, bias_full, jnp.where(band & in_cur, 0.0, NEG))
        k_s, vlo_s, vhi_s = kv_scr[3 * pi:3 * pi + 3]

        def rows(start, count, dil=dil):
            return pl.ds(start, count) if dil == 1 else pl.ds(start, count, stride=dil)

        for gi in range(span // (BLOCK * BANDED_GROUP)):
            sub = range(BANDED_GROUP)
            idx = [gi * BANDED_GROUP + u for u in sub]
            j = [i % per_stream for i in idx]
            stream = [i // per_stream for i in idx]
            start = [r + dil * BLOCK * jj for r, jj in zip(stream, j)]
            krows = [pl.ds(r * stream_len + BLOCK * jj, 2 * BLOCK) for r, jj in zip(stream, j)]
            q = [q_ref[rows(st, BLOCK), :].astype(BF16) for st in start]
            q2 = [jnp.concatenate([jnp.where(lo_q, x, jnp.zeros_like(x)), jnp.where(lo_q, jnp.zeros_like(x), x)],
                                  axis=0) for x in q]
            s = [lax.dot_general(a, k_s[kr, :], (((1,), (1,)), ((), ())), preferred_element_type=F32)
                 for a, kr in zip(q2, krows)]
            s = [x + (bias_full if jj > 0 else bias_first) for x, jj in zip(s, j)]
            m = [jnp.max(x, axis=1, keepdims=True) for x in s]
            if use_sink:
                m = [jnp.maximum(x, sink2) for x in m]
            p = [jnp.exp2(x - mm) for x, mm in zip(s, m)]
            l = [jnp.sum(x, axis=1, keepdims=True) for x in p]
            if use_sink:
                l = [x + jnp.exp2(sink2 - mm) for x, mm in zip(l, m)]
            pb = [x.astype(BF16) for x in p]
            pcat = [jnp.concatenate([x[:BLOCK], x[BLOCK:]], axis=1) for x in pb]
            vcat = [jnp.concatenate([vlo_s[kr, :], vhi_s[kr, :]], axis=0) for kr in krows]
            acc = [jnp.dot(a, b, preferred_element_type=F32) for a, b in zip(pcat, vcat)]
            l_pair = [jnp.where(lo_q, x[:BLOCK], x[BLOCK:]) for x in l]
            for u in sub:
                dst = rows(start[u], BLOCK)
                if multi:
                    acc_s, m_s, l_s = stats[3 * pi:3 * pi + 3]
                    acc_s[dst, :] = acc[u]
                    m_s[dst, :] = jnp.where(lo_q, m[u][:BLOCK], m[u][BLOCK:])
                    l_s[dst, :] = l_pair[u]
                else:
                    o_ref[dst, :] = (acc[u] / l_pair[u]).astype(o_ref.dtype)
            if multi and pi == len(patterns) - 1:
                for u in sub:
                    merge(pl.ds(start[u], BLOCK))


def _banded_attention(q, k, v, batch, seq, n_pairs, kv_chunk_of_pair, patterns, sinks=None):
    n = q.shape[0]
    span = DIL_SPAN
    nblk = seq // span
    use_sink = sinks is not None

    def q_map(b, p, i):
        return (b * nblk + i, p)

    def kv_map(b, p, i):
        return (b * nblk + i, kv_chunk_of_pair(p))

    blk = (span, LANES)
    in_specs = [pl.BlockSpec(blk, q_map), pl.BlockSpec(blk, kv_map), pl.BlockSpec(blk, kv_map)]
    args = [q, k, v]
    if use_sink:
        in_specs = [pl.BlockSpec(memory_space=pltpu.SMEM)] + in_specs
        args = [sinks.astype(F32)] + args
    scratch = []
    for dil, _ in patterns:
        scratch += [pltpu.VMEM((dil * BLOCK + span, LANES), BF16)] * 3
    if any(dil > 1 for dil, _ in patterns):
        scratch += [pltpu.VMEM((span, LANES), k.dtype), pltpu.VMEM((span, LANES), v.dtype)]
    if len(patterns) > 1:
        scratch += [pltpu.VMEM((span, LANES), F32)] * (3 * len(patterns))
    return pl.pallas_call(
        functools.partial(_banded_kernel, patterns, use_sink, span),
        grid=(batch, n_pairs, nblk),
        in_specs=in_specs,
        out_specs=pl.BlockSpec(blk, q_map),
        out_shape=jax.ShapeDtypeStruct((n, n_pairs * LANES), BF16),
        scratch_shapes=scratch,
        compiler_params=_cparams(("arbitrary",) * 3),
        name="banded_attention_%d" % len(patterns),
    )(*args)


def _mla_kernel(tq, tk, q_ref, k_ref, v_ref, o_ref, *v_scr):
    qi = pl.program_id(2)
    n_heads = len(v_scr)
    lo_o = lax.broadcasted_iota(jnp.int32, (tq, LANES), 1) < MLA_V_DIM
    qs = [q_ref[:, e * LANES:(e + 1) * LANES] for e in range(n_heads)]

    @pl.when(qi == 0)
    def _():
        for pr in range(n_heads // 2):
            v = v_ref[:, pr * LANES:(pr + 1) * LANES]
            lo_v = lax.broadcasted_iota(jnp.int32, v.shape, 1) < MLA_V_DIM
            v_scr[2 * pr][...] = jnp.where(lo_v, v, jnp.ones_like(v))
            v_scr[2 * pr + 1][...] = jnp.where(lo_v, jnp.ones_like(v), v)

    def step(ks, width, carry, masked):
        out = []
        for e, v_s in enumerate(v_scr):
            m_old, acc = carry[2 * e:2 * e + 2]
            s = lax.dot_general(qs[e], k_ref[pl.ds(ks, width), e * LANES:(e + 1) * LANES],
                                (((1,), (1,)), ((), ())), preferred_element_type=F32)
            if masked:
                row_pos = qi * tq + lax.broadcasted_iota(jnp.int32, (tq, width), 0)
                col_pos = ks + lax.broadcasted_iota(jnp.int32, (tq, width), 1)
                s = jnp.where(row_pos >= col_pos, s, NEG)
            m_new = jnp.maximum(m_old, jnp.max(s, axis=1, keepdims=True))
            p = jnp.exp2(s - m_new).astype(BF16)
            pv = jnp.dot(p, v_s[pl.ds(ks, width), :], preferred_element_type=F32)
            out += [m_new, acc * jnp.exp2(m_old - m_new) + pv]
        return tuple(out)

    carry = (jnp.full((tq, 1), NEG, F32), jnp.zeros((tq, LANES), F32)) * n_heads
    q_start = qi * tq
    n_wide = q_start // tk
    carry = lax.fori_loop(0, n_wide, lambda kb, c: step(pl.multiple_of(kb * tk, tk), tk, c, False), carry)

    def tail(widths):
        def run(c):
            if widths:
                c = step(pl.multiple_of(n_wide * tk, tk), widths * tq, c, False)
            return step(pl.multiple_of(q_start, tq), tq, c, True)
        return run

    carry = lax.switch(qi % (tk // tq), [tail(w) for w in range(tk // tq)], carry)
    for pr in range(n_heads // 2):
        acc_lo, acc_hi = carry[4 * pr + 1], carry[4 * pr + 3]
        acc = jnp.where(lo_o, acc_lo, acc_hi)
        den = jnp.where(lo_o, pltpu.roll(acc_lo, MLA_V_DIM, 1), pltpu.roll(acc_hi, MLA_V_DIM, 1))
        o_ref[:, pr * LANES:(pr + 1) * LANES] = (acc / den).astype(o_ref.dtype)


def _mla_attention(qb, kb, vb, batch, seq):
    n = qb.shape[0]
    tq, tk = MLA_TQ, MLA_TK
    nq = seq // tq
    pairs = MLA_PAIRS_PER_STEP
    return pl.pallas_call(
        functools.partial(_mla_kernel, tq, tk),
        grid=(batch, MLA_HEADS // (2 * pairs), nq),
        in_specs=[pl.BlockSpec((tq, 2 * pairs * LANES), lambda b, p, i: (b * nq + i, p)),
                  pl.BlockSpec((seq, 2 * pairs * LANES), lambda b, p, i: (b, p)),
                  pl.BlockSpec((seq, pairs * LANES), lambda b, p, i: (b, p))],
        out_specs=pl.BlockSpec((tq, pairs * LANES), lambda b, p, i: (b * nq + i, p)),
        out_shape=jax.ShapeDtypeStruct((n, MLA_HEADS // 2 * LANES), BF16),
        scratch_shapes=[pltpu.VMEM((seq, LANES), BF16)] * (2 * pairs),
        compiler_params=_cparams(("arbitrary",) * 3),
        name="mla_attention",
    )(qb, kb, vb)


X_Q_LOG2_SCALE = X_HEAD_DIM ** -0.5 * LOG2E


def _mixout_xattn_kernel(n_in, *refs):
    x_ref = refs[0]
    a_refs = refs[1:1 + n_in]
    w_refs = refs[1 + n_in:1 + 2 * n_in]
    g_ref, wq_ref, kv_ref, wo_ref, o_ref = refs[1 + 2 * n_in:]
    x1 = x_ref[...]
    for a_ref, w_ref in zip(a_refs, w_refs):
        x1 = x1 + jnp.dot(a_ref[...], w_ref[...], preferred_element_type=F32)
    h = _rms(x1, g_ref[...]).astype(BF16)
    xq = (jnp.dot(h, wq_ref[...], preferred_element_type=F32) * X_Q_LOG2_SCALE).astype(BF16)
    heads = []
    for hd in range(X_HEADS):
        q = xq[:, hd * X_HEAD_DIM:(hd + 1) * X_HEAD_DIM]
        k = kv_ref[:, hd * X_HEAD_DIM:(hd + 1) * X_HEAD_DIM]
        v = kv_ref[:, (X_HEADS + hd) * X_HEAD_DIM:(X_HEADS + hd + 1) * X_HEAD_DIM]
        s = lax.dot_general(q, k, (((1,), (1,)), ((), ())), preferred_element_type=F32)
        p = jnp.exp2(s - jnp.max(s, axis=1, keepdims=True))
        l = jnp.sum(p, axis=1, keepdims=True)
        o = jnp.dot(p.astype(BF16), v, preferred_element_type=F32) / l
        heads.append(o.astype(BF16))
    o_all = jnp.concatenate(heads, axis=1)
    o_ref[...] = x1 + jnp.dot(o_all, wo_ref[...], preferred_element_type=F32)


def _mixout_xattn(x2, acts, weights, g_x, w_xq, kv, w_xo, seq):
    n = x2.shape[0]
    tm = ROW_TILE
    per_batch = seq // tm
    xq_w = X_HEADS * X_HEAD_DIM

    def row(w):
        return pl.BlockSpec((tm, w), lambda i: (i, 0))

    weights = [w.astype(BF16) for w in weights]
    return pl.pallas_call(
        functools.partial(_mixout_xattn_kernel, len(acts)),
        grid=(n // tm,),
        in_specs=[row(D_MODEL)] + [row(a.shape[1]) for a in acts] + [_const_spec(w.shape) for w in weights]
        + [_const_spec((1, D_MODEL)), _const_spec((D_MODEL, xq_w)),
           pl.BlockSpec((MEM_LEN, 2 * xq_w), lambda i: (i // per_batch, 0)), _const_spec((xq_w, D_MODEL))],
        out_specs=row(D_MODEL),
        out_shape=jax.ShapeDtypeStruct((n, D_MODEL), F32),
        compiler_params=_cparams(("arbitrary",)),
        name="mixout_xattn",
    )(x2, *acts, *weights, g_x[None, :], w_xq.astype(BF16), kv, w_xo.astype(BF16))


def _memkv_kernel(mem_ref, g_ref, w_ref, kv_ref):
    h = _rms(mem_ref[...], g_ref[...]).astype(BF16)
    kv_ref[...] = jnp.dot(h, w_ref[...], preferred_element_type=F32).astype(kv_ref.dtype)


def _memkv(mem2, g, w_xkv):
    n = mem2.shape[0]
    cols = 2 * X_HEADS * X_HEAD_DIM
    return pl.pallas_call(
        _memkv_kernel,
        grid=(n // MEM_LEN,),
        in_specs=[pl.BlockSpec((MEM_LEN, D_MODEL), lambda i: (i, 0)), _const_spec((1, D_MODEL)),
                  _const_spec((D_MODEL, cols))],
        out_specs=pl.BlockSpec((MEM_LEN, cols), lambda i: (i, 0)),
        out_shape=jax.ShapeDtypeStruct((n, cols), BF16),
        compiler_params=_cparams(("arbitrary",)),
        name="memkv",
    )(mem2, g[None, :], w_xkv.astype(BF16))


def _ffn_kernel(final, x_ref, g_ref, wg_ref, wu_ref, wd_ref, *rest):
    if final:
        gf_ref, o_ref = rest
    else:
        (o_ref,) = rest
    x = x_ref[...]
    h = _rms(x, g_ref[...]).astype(BF16)
    width = FFN_HIDDEN // FFN_CHUNKS
    acc = x
    for c in range(FFN_CHUNKS):
        sl = slice(c * width, (c + 1) * width)
        gate = jnp.dot(h, wg_ref[:, sl], preferred_element_type=F32)
        up = jnp.dot(h, wu_ref[:, sl], preferred_element_type=F32)
        act = (gate * jax.nn.sigmoid(gate) * up).astype(BF16)
        acc = acc + jnp.dot(act, wd_ref[sl, :], preferred_element_type=F32)
    o_ref[...] = _rms(acc, gf_ref[...]) if final else acc


def _ffn(x2, g, w_gate, w_up, w_down, g_final=None):
    n = x2.shape[0]
    tm = FFN_ROW_TILE
    final = g_final is not None
    row = pl.BlockSpec((tm, D_MODEL), lambda i: (i, 0))

    def weight(shape):
        return pl.BlockSpec(shape, lambda i: (0, 0), pipeline_mode=pl.Buffered(1))

    in_specs = [row, _const_spec((1, D_MODEL)), weight((D_MODEL, FFN_HIDDEN)), weight((D_MODEL, FFN_HIDDEN)),
                weight((FFN_HIDDEN, D_MODEL))]
    args = [x2, g[None, :], w_gate.astype(BF16), w_up.astype(BF16), w_down.astype(BF16)]
    if final:
        in_specs.append(_const_spec((1, D_MODEL)))
        args.append(g_final[None, :])
    return pl.pallas_call(
        functools.partial(_ffn_kernel, final),
        grid=(n // tm,),
        in_specs=in_specs,
        out_specs=row,
        out_shape=jax.ShapeDtypeStruct((n, D_MODEL), F32),
        compiler_params=_cparams(("arbitrary",)),
        name="ffn",
    )(*args)


def kernel(x, mem, positions, l0_mix_norm, l0_w_in, l0_sinks, l0_q_norm, l0_w_uq, l0_kv_norm, l0_w_ukv, l0_w_out, l0_x_norm, l0_mem_norm, l0_w_xq, l0_w_xkv, l0_w_xo, l0_ffn_norm, l0_w_gate, l0_w_up, l0_w_down, l1_mix_norm, l1_w_qkv, l1_w_out, l1_x_norm, l1_mem_norm, l1_w_xq, l1_w_xkv, l1_w_xo, l1_ffn_norm, l1_w_gate, l1_w_up, l1_w_down, final_norm):
    batch, seq, _ = x.shape
    assert seq % DIL_SPAN == 0 and seq % MLA_TQ == 0 and seq % ROW_TILE == 0
    n = batch * seq
    x2 = x.reshape(n, D_MODEL)
    mem2 = mem.reshape(batch * MEM_LEN, D_MODEL)
    tables = _rope_tables(positions)

    qa, ka, va, qb, kb, vb = _l0_proj(x2, l0_mix_norm, l0_w_in, l0_q_norm, l0_w_uq, l0_kv_norm, l0_w_ukv, tables)
    oa = _banded_attention(qa, ka, va, batch, seq, SWA_HEADS // 2, lambda p: p // 2,
                           ((1, SWA_WINDOW - 1),), sinks=l0_sinks)
    ob = _mla_attention(qb, kb, vb, batch, seq)
    x2 = _mixout_xattn(x2, [oa, ob], [l0_w_out[:A_Q], l0_w_out[A_Q:]], l0_x_norm, l0_w_xq,
                       _memkv(mem2, l0_mem_norm, l0_w_xkv), l0_w_xo, seq)
    x2 = _ffn(x2, l0_ffn_norm, l0_w_gate, l0_w_up, l0_w_down)

    q, k, v = _l1_proj(x2, l1_mix_norm, l1_w_qkv, tables)
    od = _banded_attention(q, k, v, batch, seq, DIL_HEADS // 2, lambda p: p,
                           tuple((dil, window // dil) for window, dil in reversed(DIL_PATTERNS)))
    x2 = _mixout_xattn(x2, [od], [l1_w_out], l1_x_norm, l1_w_xq,
                       _memkv(mem2, l1_mem_norm, l1_w_xkv), l1_w_xo, seq)
    x2 = _ffn(x2, l1_ffn_norm, l1_w_gate, l1_w_up, l1_w_down, g_final=final_norm)
    return x2.reshape(batch, seq, D_MODEL)
```

```python
import functools

import jax
import jax.numpy as jnp
from jax import lax
from jax.experimental import pallas as pl
from jax.experimental.pallas import tpu as pltpu

D_MODEL = 1024
MEM_LEN = 256
HEAD_DIM = 64
ROPE_THETA = 10000.0
NORM_EPS = 1e-6
BLOCK = 128
SWA_HEADS = 8
SWA_KV_HEADS = 2
SWA_WINDOW = 128
MLA_HEADS = 8
MLA_Q_RANK = 384
MLA_KV_RANK = 256
MLA_NOPE_DIM = 64
MLA_ROPE_DIM = 32
MLA_V_DIM = 64
A_Q = SWA_HEADS * HEAD_DIM
A_KV = SWA_KV_HEADS * HEAD_DIM
DIL_HEADS = D_MODEL // HEAD_DIM
DIL_PATTERNS = ((128, 1), (512, 4), (2048, 16))
X_HEADS = 4
X_HEAD_DIM = 128
FFN_HIDDEN = -(-8 * D_MODEL // (3 * 256)) * 256

LANES = 128
V7X_VMEM_LIMIT = 56 * 1024 * 1024
NEG = -1e30
LOG2E = 1.4426950408889634

ROW_TILE = 512
DIL_SPAN = 2048
STAGE_DIL = 4
BANDED_GROUP = 4
MLA_TQ = 512
MLA_TK = 2048
MLA_PAIRS_PER_STEP = 2
WIDE_ROW_TILE = 1024
FFN_CHUNKS = 11

F32 = jnp.float32
BF16 = jnp.bfloat16


def _cparams(sem, flags=None):
    return pltpu.CompilerParams(dimension_semantics=sem, vmem_limit_bytes=V7X_VMEM_LIMIT, flags=flags)


def _const_spec(shape):
    return pl.BlockSpec(shape, lambda *_: (0,) * len(shape))


def _rms(x, g):
    return x * lax.rsqrt(jnp.mean(x * x, axis=-1, keepdims=True) + NORM_EPS) * g


def _rope_chunk(xc, c, s, half):
    lane = lax.broadcasted_iota(jnp.int32, xc.shape, 1)
    up = pltpu.roll(xc, half, 1)
    down = pltpu.roll(xc, LANES - half, 1)
    partner = jnp.where((lane & (2 * half - 1)) < half, down, up)
    return xc * c + partner * s


def _tables_kernel(pos_ref, freq_ref, sg64_ref, sg32_ref, on32_ref, c64_ref, s64_ref, c32_ref, s32_ref):
    ang = pos_ref[...].astype(F32) * freq_ref[...]
    lane = lax.broadcasted_iota(jnp.int32, ang.shape, 1)
    half64, half32 = HEAD_DIM // 2, MLA_ROPE_DIM // 2
    on = on32_ref[...] > 0.5
    for t, sign64, sign32, off32, o64_ref, o32_ref in ((jnp.cos(ang), None, None, 1.0, c64_ref, c32_ref),
                                                       (jnp.sin(ang), sg64_ref, sg32_ref, 0.0, s64_ref, s32_ref)):
        b64 = jnp.where(lane < half64, t, 0.0)
        b64 = b64 + pltpu.roll(b64, half64, 1)
        b64 = b64 + pltpu.roll(b64, 2 * half64, 1)
        b32 = jnp.where((lane >= half64) & (lane < half64 + half32), t, 0.0)
        b32 = pltpu.roll(b32, MLA_NOPE_DIM - half64, 1) + pltpu.roll(b32, MLA_NOPE_DIM - half64 + half32, 1)
        if sign64 is not None:
            b64 = b64 * sign64[...]
            b32 = b32 * sign32[...]
        o64_ref[...] = b64
        o32_ref[...] = jnp.where(on, b32, off32)


def _rope_tables(positions):
    n = positions.size
    tm = 2048
    lane = jnp.arange(LANES)
    half64, half32 = HEAD_DIM // 2, MLA_ROPE_DIM // 2
    inv64 = ROPE_THETA ** (-jnp.arange(0, HEAD_DIM, 2, dtype=F32) / HEAD_DIM)
    inv32 = ROPE_THETA ** (-jnp.arange(0, MLA_ROPE_DIM, 2, dtype=F32) / MLA_ROPE_DIM)
    freq = jnp.zeros((LANES,), F32).at[:half64].set(inv64).at[half64:half64 + half32].set(inv32)[None, :]
    sg64 = jnp.where(lane % HEAD_DIM < half64, -1.0, 1.0).astype(F32)[None, :]
    rl = lane - MLA_NOPE_DIM
    on32 = ((rl >= 0) & (rl < MLA_ROPE_DIM)).astype(F32)[None, :]
    sg32 = jnp.where(rl % MLA_ROPE_DIM < half32, -1.0, 1.0).astype(F32)[None, :]
    row = pl.BlockSpec((tm, LANES), lambda i: (i, 0))
    out = jax.ShapeDtypeStruct((n, LANES), F32)
    return pl.pallas_call(
        _tables_kernel,
        grid=(n // tm,),
        in_specs=[pl.BlockSpec((tm, 1), lambda i: (i, 0))] + [_const_spec((1, LANES))] * 4,
        out_specs=[row] * 4,
        out_shape=[out] * 4,
        compiler_params=_cparams(("arbitrary",)),
        name="rope_tables",
    )(positions.reshape(n, 1), freq, sg64, sg32, on32)


L0_QA = 0
L0_KA = A_Q
L0_VA = L0_KA + 2 * A_KV
L0_CQ = L0_VA + 2 * A_KV
L0_CKV = L0_CQ + MLA_Q_RANK
L0_KR = L0_CKV + MLA_KV_RANK
L0_COLS = L0_KR + LANES
MLA_QK = MLA_HEADS * LANES
MLA_Q_LOG2_SCALE = (MLA_NOPE_DIM + MLA_ROPE_DIM) ** -0.5 * LOG2E
MLA_V = MLA_HEADS * MLA_V_DIM


def _l0_proj_kernel(x_ref, g_ref, win_ref, gq_ref, wuq_ref, gkv_ref, wukv_ref, c64_ref, s64_ref, c32_ref, s32_ref,
                    qa_ref, ka_ref, va_ref, qb_ref, kb_ref, vb_ref):
    h = _rms(x_ref[...], g_ref[...]).astype(BF16)
    z = jnp.dot(h, win_ref[...], preferred_element_type=F32)
    c64, s64, c32, s32 = c64_ref[...], s64_ref[...], c32_ref[...], s32_ref[...]
    qscale = HEAD_DIM ** -0.5 * LOG2E
    for c in range(A_Q // LANES):
        sl = slice(c * LANES, (c + 1) * LANES)
        qa_ref[:, sl] = (_rope_chunk(z[:, sl], c64, s64, HEAD_DIM // 2) * qscale).astype(qa_ref.dtype)
    for c in range(2 * A_KV // LANES):
        sl = slice(c * LANES, (c + 1) * LANES)
        ka_ref[:, sl] = _rope_chunk(z[:, L0_KA + c * LANES:L0_KA + (c + 1) * LANES], c64, s64,
                                    HEAD_DIM // 2).astype(ka_ref.dtype)
    va_ref[...] = z[:, L0_VA:L0_CQ].astype(va_ref.dtype)

    cq = _rms(z[:, L0_CQ:L0_CKV], gq_ref[...]).astype(BF16)
    qb = jnp.dot(cq, wuq_ref[...], preferred_element_type=F32)
    ckv = _rms(z[:, L0_CKV:L0_KR], gkv_ref[...]).astype(BF16)
    kv = jnp.dot(ckv, wukv_ref[...], preferred_element_type=F32)
    kr = _rope_chunk(z[:, L0_KR:L0_COLS], c32, s32, MLA_ROPE_DIM // 2)
    for hd in range(MLA_HEADS):
        sl = slice(hd * LANES, (hd + 1) * LANES)
        qb_ref[:, sl] = (_rope_chunk(qb[:, sl], c32, s32, MLA_ROPE_DIM // 2) * MLA_Q_LOG2_SCALE).astype(qb_ref.dtype)
        kb_ref[:, sl] = (kv[:, sl] + kr).astype(kb_ref.dtype)
    vb_ref[...] = kv[:, MLA_QK:].astype(vb_ref.dtype)


def _l0_proj(x2, g, w_in, g_q, w_uq, g_kv, w_ukv, tables):
    n = x2.shape[0]
    tm = WIDE_ROW_TILE
    qa_w, ka_w, va_w, cq_w, ckv_w, kr_w = jnp.split(
        w_in, [A_Q, A_Q + A_KV, A_Q + 2 * A_KV, A_Q + 2 * A_KV + MLA_Q_RANK,
               A_Q + 2 * A_KV + MLA_Q_RANK + MLA_KV_RANK], axis=1)

    def dup(w):
        return jnp.repeat(w.reshape(D_MODEL, SWA_KV_HEADS, 1, HEAD_DIM), 2, axis=2).reshape(D_MODEL, 2 * A_KV)

    kr_pad = jnp.pad(kr_w, ((0, 0), (MLA_NOPE_DIM, LANES - MLA_NOPE_DIM - MLA_ROPE_DIM)))
    win_p = jnp.concatenate([qa_w, dup(ka_w), dup(va_w), cq_w, ckv_w, kr_pad], axis=1).astype(BF16)
    per_head_q = MLA_NOPE_DIM + MLA_ROPE_DIM
    wuq_p = jnp.pad(w_uq.reshape(MLA_Q_RANK, MLA_HEADS, per_head_q),
                    ((0, 0), (0, 0), (0, LANES - per_head_q))).reshape(MLA_Q_RANK, MLA_QK).astype(BF16)
    wukv3 = w_ukv.reshape(MLA_KV_RANK, MLA_HEADS, MLA_NOPE_DIM + MLA_V_DIM)
    wk_p = jnp.pad(wukv3[:, :, :MLA_NOPE_DIM], ((0, 0), (0, 0), (0, LANES - MLA_NOPE_DIM))).reshape(MLA_KV_RANK, MLA_QK)
    wv_p = wukv3[:, :, MLA_NOPE_DIM:].reshape(MLA_KV_RANK, MLA_V)
    wukv_p = jnp.concatenate([wk_p, wv_p], axis=1).astype(BF16)

    def row(w):
        return pl.BlockSpec((tm, w), lambda i: (i, 0))

    outs = [(A_Q, BF16), (2 * A_KV, BF16), (2 * A_KV, BF16), (MLA_QK, BF16), (MLA_QK, BF16), (MLA_V, BF16)]
    return pl.pallas_call(
        _l0_proj_kernel,
        grid=(n // tm,),
        in_specs=[row(D_MODEL), _const_spec((1, D_MODEL)), _const_spec(win_p.shape),
                  _const_spec((1, MLA_Q_RANK)), _const_spec(wuq_p.shape),
                  _const_spec((1, MLA_KV_RANK)), _const_spec(wukv_p.shape)] + [row(LANES)] * 4,
        out_specs=[row(w) for w, _ in outs],
        out_shape=[jax.ShapeDtypeStruct((n, w), dt) for w, dt in outs],
        compiler_params=_cparams(("arbitrary",)),
        name="l0_proj",
    )(x2, g[None, :], win_p, g_q[None, :], wuq_p, g_kv[None, :], wukv_p, *tables)


def _l1_proj_kernel(x_ref, g_ref, w_ref, c64_ref, s64_ref, q_ref, k_ref, v_ref):
    h = _rms(x_ref[...], g_ref[...]).astype(BF16)
    z = jnp.dot(h, w_ref[...], preferred_element_type=F32)
    c64, s64 = c64_ref[...], s64_ref[...]
    qscale = HEAD_DIM ** -0.5 * LOG2E
    for c in range(D_MODEL // LANES):
        sl = slice(c * LANES, (c + 1) * LANES)
        q_ref[:, sl] = _rope_chunk(z[:, sl], c64, s64, HEAD_DIM // 2) * qscale
        k_ref[:, sl] = _rope_chunk(z[:, D_MODEL + c * LANES:D_MODEL + (c + 1) * LANES], c64, s64, HEAD_DIM // 2)
    v_ref[...] = z[:, 2 * D_MODEL:]


def _l1_proj(x2, g, w_qkv, tables):
    n = x2.shape[0]
    tm = ROW_TILE
    row = pl.BlockSpec((tm, D_MODEL), lambda i: (i, 0))
    tab = pl.BlockSpec((tm, LANES), lambda i: (i, 0))
    out = jax.ShapeDtypeStruct((n, D_MODEL), F32)
    return pl.pallas_call(
        _l1_proj_kernel,
        grid=(n // tm,),
        in_specs=[row, _const_spec((1, D_MODEL)), _const_spec(w_qkv.shape), tab, tab],
        out_specs=[row] * 3,
        out_shape=[out] * 3,
        compiler_params=_cparams(("arbitrary",)),
        name="l1_proj",
    )(x2, g[None, :], w_qkv.astype(BF16), tables[0], tables[1])


def _banded_kernel(patterns, use_sink, span, *refs):
    refs = list(refs)
    sinks_ref = refs.pop(0) if use_sink else None
    q_ref, k_ref, v_ref, o_ref = refs[:4]
    kv_scr = refs[4:4 + 3 * len(patterns)]
    rest = refs[4 + 3 * len(patterns):]
    staged = any(dil > 1 for dil, _ in patterns)
    k4_s, v4_s = rest[:2] if staged else (None, None)
    stats = rest[2:] if staged else rest
    multi = len(patterns) > 1
    pair = pl.program_id(1)
    blk = pl.program_id(2)

    for first in (True, False):
        @pl.when(blk == 0 if first else blk > 0)
        def _(first=first):
            for pi, (dil, _) in enumerate(patterns):
                stream_len = BLOCK + span // dil
                for r in range(dil):
                    head = slice(r * stream_len, r * stream_len + BLOCK)
                    tail = slice((r + 1) * stream_len - BLOCK, (r + 1) * stream_len)
                    for ref in kv_scr[3 * pi:3 * pi + 3]:
                        ref[head, :] = jnp.zeros((BLOCK, LANES), BF16) if first else ref[tail, :]

    def put(pi, dst, count, k_rows, v_rows):
        k_s, vlo_s, vhi_s = kv_scr[3 * pi:3 * pi + 3]
        k_s[dst:dst + count, :] = k_rows.astype(BF16)
        v_rows = v_rows.astype(BF16)
        lo = lax.broadcasted_iota(jnp.int32, (count, LANES), 1) < HEAD_DIM
        vlo_s[dst:dst + count, :] = jnp.where(lo, v_rows, jnp.zeros_like(v_rows))
        vhi_s[dst:dst + count, :] = jnp.where(lo, jnp.zeros_like(v_rows), v_rows)

    quarter = span // STAGE_DIL
    if staged:
        for c in range(STAGE_DIL):
            k4_s[c * quarter:(c + 1) * quarter, :] = k_ref[pl.ds(c, quarter, stride=STAGE_DIL), :]
            v4_s[c * quarter:(c + 1) * quarter, :] = v_ref[pl.ds(c, quarter, stride=STAGE_DIL), :]
    for pi, (dil, _) in enumerate(patterns):
        cur_len = span // dil
        stream_len = BLOCK + cur_len
        if dil == 1:
            put(pi, BLOCK, span, k_ref[...], v_ref[...])
        elif dil == STAGE_DIL:
            for c in range(dil):
                src = slice(c * quarter, (c + 1) * quarter)
                put(pi, c * stream_len + BLOCK, cur_len, k4_s[src, :], v4_s[src, :])
        else:
            assert dil == STAGE_DIL * STAGE_DIL
            for r in range(dil):
                src = pl.ds((r % STAGE_DIL) * quarter + r // STAGE_DIL, cur_len, stride=STAGE_DIL)
                put(pi, r * stream_len + BLOCK, cur_len, k4_s[src, :], v4_s[src, :])

    lo_q = lax.broadcasted_iota(jnp.int32, (BLOCK, LANES), 1) < HEAD_DIM
    qi = lax.broadcasted_iota(jnp.int32, (2 * BLOCK, 2 * BLOCK), 0) & (BLOCK - 1)
    kj = lax.broadcasted_iota(jnp.int32, (2 * BLOCK, 2 * BLOCK), 1)
    dist = BLOCK + qi - kj
    in_cur = kj >= BLOCK
    if use_sink:
        top_rows = lax.broadcasted_iota(jnp.int32, (2 * BLOCK, 1), 0) < BLOCK
        sink2 = jnp.where(top_rows, sinks_ref[2 * pair], sinks_ref[2 * pair + 1]) * LOG2E

    def merge(rs):
        m_all = [stats[3 * pi + 1][rs, :] for pi in range(len(patterns))]
        m_top = functools.reduce(jnp.maximum, m_all)
        num = den = None
        for pi in range(len(patterns)):
            w = jnp.exp2(m_all[pi] - m_top)
            num = w * stats[3 * pi][rs, :] if num is None else num + w * stats[3 * pi][rs, :]
            den = w * stats[3 * pi + 2][rs, :] if den is None else den + w * stats[3 * pi + 2][rs, :]
        o_ref[rs, :] = (num / den).astype(o_ref.dtype)

    assert patterns[-1][0] == 1
    for pi, (dil, max_dist) in enumerate(patterns):
        per_stream = span // (BLOCK * dil)
        stream_len = BLOCK + span // dil
        band = (dist >= 0) & (dist <= max_dist)
        bias_full = jnp.where(band, 0.0, NEG)
        bias_first = jnp.where(blk > 0, bias_full, jnp.where(band & in_cur, 0.0, NEG))
        k_s, vlo_s, vhi_s = kv_scr[3 * pi:3 * pi + 3]

        def rows(start, count, dil=dil):
            return pl.ds(start, count) if dil == 1 else pl.ds(start, count, stride=dil)

        for gi in range(span // (BLOCK * BANDED_GROUP)):
            sub = range(BANDED_GROUP)
            idx = [gi * BANDED_GROUP + u for u in sub]
            j = [i % per_stream for i in idx]
            stream = [i // per_stream for i in idx]
            start = [r + dil * BLOCK * jj for r, jj in zip(stream, j)]
            krows = [pl.ds(r * stream_len + BLOCK * jj, 2 * BLOCK) for r, jj in zip(stream, j)]
            q = [q_ref[rows(st, BLOCK), :].astype(BF16) for st in start]
            q2 = [jnp.concatenate([jnp.where(lo_q, x, jnp.zeros_like(x)), jnp.where(lo_q, jnp.zeros_like(x), x)],
                                  axis=0) for x in q]
            s = [lax.dot_general(a, k_s[kr, :], (((1,), (1,)), ((), ())), preferred_element_type=F32)
                 for a, kr in zip(q2, krows)]
            s = [x + (bias_full if jj > 0 else bias_first) for x, jj in zip(s, j)]
            m = [jnp.max(x, axis=1, keepdims=True) for x in s]
            if use_sink:
                m = [jnp.maximum(x, sink2) for x in m]
            p = [jnp.exp2(x - mm) for x, mm in zip(s, m)]
            l = [jnp.sum(x, axis=1, keepdims=True) for x in p]
            if use_sink:
                l = [x + jnp.exp2(sink2 - mm) for x, mm in zip(l, m)]
            pb = [x.astype(BF16) for x in p]
            pcat = [jnp.concatenate([x[:BLOCK], x[BLOCK:]], axis=1) for x in pb]
            vcat = [jnp.concatenate([vlo_s[kr, :], vhi_s[kr, :]], axis=0) for kr in krows]
            acc = [jnp.dot(a, b, preferred_element_type=F32) for a, b in zip(pcat, vcat)]
            l_pair = [jnp.where(lo_q, x[:BLOCK], x[BLOCK:]) for x in l]
            for u in sub:
                dst = rows(start[u], BLOCK)
                if multi:
                    acc_s, m_s, l_s = stats[3 * pi:3 * pi + 3]
                    acc_s[dst, :] = acc[u]
                    m_s[dst, :] = jnp.where(lo_q, m[u][:BLOCK], m[u][BLOCK:])
                    l_s[dst, :] = l_pair[u]
                else:
                    o_ref[dst, :] = (acc[u] / l_pair[u]).astype(o_ref.dtype)
            if multi and pi == len(patterns) - 1:
                for u in sub:
                    merge(pl.ds(start[u], BLOCK))


def _banded_attention(q, k, v, batch, seq, n_pairs, kv_chunk_of_pair, patterns, sinks=None):
    n = q.shape[0]
    span = DIL_SPAN
    nblk = seq // span
    use_sink = sinks is not None

    def q_map(b, p, i):
        return (b * nblk + i, p)

    def kv_map(b, p, i):
        return (b * nblk + i, kv_chunk_of_pair(p))

    blk = (span, LANES)
    in_specs = [pl.BlockSpec(blk, q_map), pl.BlockSpec(blk, kv_map), pl.BlockSpec(blk, kv_map)]
    args = [q, k, v]
    if use_sink:
        in_specs = [pl.BlockSpec(memory_space=pltpu.SMEM)] + in_specs
        args = [sinks.astype(F32)] + args
    scratch = []
    for dil, _ in patterns:
        scratch += [pltpu.VMEM((dil * BLOCK + span, LANES), BF16)] * 3
    if any(dil > 1 for dil, _ in patterns):
        scratch += [pltpu.VMEM((span, LANES), k.dtype), pltpu.VMEM((span, LANES), v.dtype)]
    if len(patterns) > 1:
        scratch += [pltpu.VMEM((span, LANES), F32)] * (3 * len(patterns))
    return pl.pallas_call(
        functools.partial(_banded_kernel, patterns, use_sink, span),
        grid=(batch, n_pairs, nblk),
        in_specs=in_specs,
        out_specs=pl.BlockSpec(blk, q_map),
        out_shape=jax.ShapeDtypeStruct((n, n_pairs * LANES), BF16),
        scratch_shapes=scratch,
        compiler_params=_cparams(("arbitrary",) * 3),
        name="banded_attention_%d" % len(patterns),
    )(*args)


def _mla_kernel(tq, tk, q_ref, k_ref, v_ref, o_ref, *v_scr):
    qi = pl.program_id(2)
    n_heads = len(v_scr)
    lo_o = lax.broadcasted_iota(jnp.int32, (tq, LANES), 1) < MLA_V_DIM
    qs = [q_ref[:, e * LANES:(e + 1) * LANES] for e in range(n_heads)]

    @pl.when(qi == 0)
    def _():
        for pr in range(n_heads // 2):
            v = v_ref[:, pr * LANES:(pr + 1) * LANES]
            lo_v = lax.broadcasted_iota(jnp.int32, v.shape, 1) < MLA_V_DIM
            v_scr[2 * pr][...] = jnp.where(lo_v, v, jnp.ones_like(v))
            v_scr[2 * pr + 1][...] = jnp.where(lo_v, jnp.ones_like(v), v)

    def step(ks, width, carry, masked):
        out = []
        for e, v_s in enumerate(v_scr):
            m_old, acc = carry[2 * e:2 * e + 2]
            s = lax.dot_general(qs[e], k_ref[pl.ds(ks, width), e * LANES:(e + 1) * LANES],
                                (((1,), (1,)), ((), ())), preferred_element_type=F32)
            if masked:
                row_pos = qi * tq + lax.broadcasted_iota(jnp.int32, (tq, width), 0)
                col_pos = ks + lax.broadcasted_iota(jnp.int32, (tq, width), 1)
                s = jnp.where(row_pos >= col_pos, s, NEG)
            m_new = jnp.maximum(m_old, jnp.max(s, axis=1, keepdims=True))
            p = jnp.exp2(s - m_new).astype(BF16)
            pv = jnp.dot(p, v_s[pl.ds(ks, width), :], preferred_element_type=F32)
            out += [m_new, acc * jnp.exp2(m_old - m_new) + pv]
        return tuple(out)

    carry = (jnp.full((tq, 1), NEG, F32), jnp.zeros((tq, LANES), F32)) * n_heads
    q_start = qi * tq
    n_wide = q_start // tk
    carry = lax.fori_loop(0, n_wide, lambda kb, c: step(pl.multiple_of(kb * tk, tk), tk, c, False), carry)

    def tail(widths):
        def run(c):
            if widths:
                c = step(pl.multiple_of(n_wide * tk, tk), widths * tq, c, False)
            return step(pl.multiple_of(q_start, tq), tq, c, True)
        return run

    carry = lax.switch(qi % (tk // tq), [tail(w) for w in range(tk // tq)], carry)
    for pr in range(n_heads // 2):
        acc_lo, acc_hi = carry[4 * pr + 1], carry[4 * pr + 3]
        acc = jnp.where(lo_o, acc_lo, acc_hi)
        den = jnp.where(lo_o, pltpu.roll(acc_lo, MLA_V_DIM, 1), pltpu.roll(acc_hi, MLA_V_DIM, 1))
        o_ref[:, pr * LANES:(pr + 1) * LANES] = (acc / den).astype(o_ref.dtype)


def _mla_attention(qb, kb, vb, batch, seq):
    n = qb.shape[0]
    tq, tk = MLA_TQ, MLA_TK
    nq = seq // tq
    pairs = MLA_PAIRS_PER_STEP
    return pl.pallas_call(
        functools.partial(_mla_kernel, tq, tk),
        grid=(batch, MLA_HEADS // (2 * pairs), nq),
        in_specs=[pl.BlockSpec((tq, 2 * pairs * LANES), lambda b, p, i: (b * nq + i, p)),
                  pl.BlockSpec((seq, 2 * pairs * LANES), lambda b, p, i: (b, p)),
                  pl.BlockSpec((seq, pairs * LANES), lambda b, p, i: (b, p))],
        out_specs=pl.BlockSpec((tq, pairs * LANES), lambda b, p, i: (b * nq + i, p)),
        out_shape=jax.ShapeDtypeStruct((n, MLA_HEADS // 2 * LANES), BF16),
        scratch_shapes=[pltpu.VMEM((seq, LANES), BF16)] * (2 * pairs),
        compiler_params=_cparams(("arbitrary",) * 3),
        name="mla_attention",
    )(qb, kb, vb)


X_Q_LOG2_SCALE = X_HEAD_DIM ** -0.5 * LOG2E


def _mixout_xattn_kernel(n_in, *refs):
    x_ref = refs[0]
    a_refs = refs[1:1 + n_in]
    w_refs = refs[1 + n_in:1 + 2 * n_in]
    g_ref, wq_ref, kv_ref, wo_ref, o_ref = refs[1 + 2 * n_in:]
    x1 = x_ref[...]
    for a_ref, w_ref in zip(a_refs, w_refs):
        x1 = x1 + jnp.dot(a_ref[...], w_ref[...], preferred_element_type=F32)
    h = _rms(x1, g_ref[...]).astype(BF16)
    xq = (jnp.dot(h, wq_ref[...], preferred_element_type=F32) * X_Q_LOG2_SCALE).astype(BF16)
    heads = []
    for hd in range(X_HEADS):
        q = xq[:, hd * X_HEAD_DIM:(hd + 1) * X_HEAD_DIM]
        k = kv_ref[:, hd * X_HEAD_DIM:(hd + 1) * X_HEAD_DIM]
        v = kv_ref[:, (X_HEADS + hd) * X_HEAD_DIM:(X_HEADS + hd + 1) * X_HEAD_DIM]
        s = lax.dot_general(q, k, (((1,), (1,)), ((), ())), preferred_element_type=F32)
        p = jnp.exp2(s - jnp.max(s, axis=1, keepdims=True))
        l = jnp.sum(p, axis=1, keepdims=True)
        o = jnp.dot(p.astype(BF16), v, preferred_element_type=F32) / l
        heads.append(o.astype(BF16))
    o_all = jnp.concatenate(heads, axis=1)
    o_ref[...] = x1 + jnp.dot(o_all, wo_ref[...], preferred_element_type=F32)


def _mixout_xattn(x2, acts, weights, g_x, w_xq, kv, w_xo, seq):
    n = x2.shape[0]
    tm = WIDE_ROW_TILE
    per_batch = seq // tm
    xq_w = X_HEADS * X_HEAD_DIM

    def row(w):
        return pl.BlockSpec((tm, w), lambda i: (i, 0))

    weights = [w.astype(BF16) for w in weights]
    return pl.pallas_call(
        functools.partial(_mixout_xattn_kernel, len(acts)),
        grid=(n // tm,),
        in_specs=[row(D_MODEL)] + [row(a.shape[1]) for a in acts] + [_const_spec(w.shape) for w in weights]
        + [_const_spec((1, D_MODEL)), _const_spec((D_MODEL, xq_w)),
           pl.BlockSpec((MEM_LEN, 2 * xq_w), lambda i: (i // per_batch, 0)), _const_spec((xq_w, D_MODEL))],
        out_specs=row(D_MODEL),
        out_shape=jax.ShapeDtypeStruct((n, D_MODEL), F32),
        compiler_params=_cparams(("arbitrary",)),
        name="mixout_xattn",
    )(x2, *acts, *weights, g_x[None, :], w_xq.astype(BF16), kv, w_xo.astype(BF16))


def _memkv_kernel(mem_ref, g_ref, w_ref, kv_ref):
    h = _rms(mem_ref[...], g_ref[...]).astype(BF16)
    kv_ref[...] = jnp.dot(h, w_ref[...], preferred_element_type=F32).astype(kv_ref.dtype)


def _memkv(mem2, g, w_xkv):
    n = mem2.shape[0]
    cols = 2 * X_HEADS * X_HEAD_DIM
    return pl.pallas_call(
        _memkv_kernel,
        grid=(n // MEM_LEN,),
        in_specs=[pl.BlockSpec((MEM_LEN, D_MODEL), lambda i: (i, 0)), _const_spec((1, D_MODEL)),
                  _const_spec((D_MODEL, cols))],
        out_specs=pl.BlockSpec((MEM_LEN, cols), lambda i: (i, 0)),
        out_shape=jax.ShapeDtypeStruct((n, cols), BF16),
        compiler_params=_cparams(("arbitrary",)),
        name="memkv",
    )(mem2, g[None, :], w_xkv.astype(BF16))


def _ffn_kernel(final, x_ref, g_ref, wg_ref, wu_ref, wd_ref, *rest):
    if final:
        gf_ref, o_ref = rest
    else:
        (o_ref,) = rest
    x = x_ref[...]
    h = _rms(x, g_ref[...]).astype(BF16)
    width = FFN_HIDDEN // FFN_CHUNKS
    acc = x
    for c in range(FFN_CHUNKS):
        sl = slice(c * width, (c + 1) * width)
        gate = jnp.dot(h, wg_ref[:, sl], preferred_element_type=F32)
        up = jnp.dot(h, wu_ref[:, sl], preferred_element_type=F32)
        act = (gate * jax.nn.sigmoid(gate) * up).astype(BF16)
        acc = acc + jnp.dot(act, wd_ref[sl, :], preferred_element_type=F32)
    o_ref[...] = _rms(acc, gf_ref[...]) if final else acc


def _ffn(x2, g, w_gate, w_up, w_down, g_final=None):
    n = x2.shape[0]
    tm = WIDE_ROW_TILE
    final = g_final is not None
    row = pl.BlockSpec((tm, D_MODEL), lambda i: (i, 0))

    def weight(shape):
        return pl.BlockSpec(shape, lambda i: (0, 0), pipeline_mode=pl.Buffered(1))

    in_specs = [row, _const_spec((1, D_MODEL)), weight((D_MODEL, FFN_HIDDEN)), weight((D_MODEL, FFN_HIDDEN)),
                weight((FFN_HIDDEN, D_MODEL))]
    args = [x2, g[None, :], w_gate.astype(BF16), w_up.astype(BF16), w_down.astype(BF16)]
    if final:
        in_specs.append(_const_spec((1, D_MODEL)))
        args.append(g_final[None, :])
    return pl.pallas_call(
        functools.partial(_ffn_kernel, final),
        grid=(n // tm,),
        in_specs=in_specs,
        out_specs=row,
        out_shape=jax.ShapeDtypeStruct((n, D_MODEL), F32),
        compiler_params=_cparams(("arbitrary",)),
        name="ffn",
    )(*args)


def kernel(x, mem, positions, l0_mix_norm, l0_w_in, l0_sinks, l0_q_norm, l0_w_uq, l0_kv_norm, l0_w_ukv, l0_w_out, l0_x_norm, l0_mem_norm, l0_w_xq, l0_w_xkv, l0_w_xo, l0_ffn_norm, l0_w_gate, l0_w_up, l0_w_down, l1_mix_norm, l1_w_qkv, l1_w_out, l1_x_norm, l1_mem_norm, l1_w_xq, l1_w_xkv, l1_w_xo, l1_ffn_norm, l1_w_gate, l1_w_up, l1_w_down, final_norm):
    batch, seq, _ = x.shape
    assert seq % DIL_SPAN == 0 and seq % MLA_TK == 0 and seq % WIDE_ROW_TILE == 0
    n = batch * seq
    x2 = x.reshape(n, D_MODEL)
    mem2 = mem.reshape(batch * MEM_LEN, D_MODEL)
    tables = _rope_tables(positions)

    qa, ka, va, qb, kb, vb = _l0_proj(x2, l0_mix_norm, l0_w_in, l0_q_norm, l0_w_uq, l0_kv_norm, l0_w_ukv, tables)
    oa = _banded_attention(qa, ka, va, batch, seq, SWA_HEADS // 2, lambda p: p // 2,
                           ((1, SWA_WINDOW - 1),), sinks=l0_sinks)
    ob = _mla_attention(qb, kb, vb, batch, seq)
    x2 = _mixout_xattn(x2, [oa, ob], [l0_w_out[:A_Q], l0_w_out[A_Q:]], l0_x_norm, l0_w_xq,
                       _memkv(mem2, l0_mem_norm, l0_w_xkv), l0_w_xo, seq)
    x2 = _ffn(x2, l0_ffn_norm, l0_w_gate, l0_w_up, l0_w_down)

    q, k, v = _l1_proj(x2, l1_mix_norm, l1_w_qkv, tables)
    od = _banded_attention(q, k, v, batch, seq, DIL_HEADS // 2, lambda p: p,
                           tuple((dil, window // dil) for window, dil in reversed(DIL_PATTERNS)))
    x2 = _mixout_xattn(x2, [od], [l1_w_out], l1_x_norm, l1_w_xq,
                       _memkv(mem2, l1_mem_norm, l1_w_xkv), l1_w_xo, seq)
    x2 = _ffn(x2, l1_ffn_norm, l1_w_gate, l1_w_up, l1_w_down, g_final=final_norm)
    return x2.reshape(batch, seq, D_MODEL)
```

```python
import functools

import jax
import jax.numpy as jnp
from jax import lax
from jax.experimental import pallas as pl
from jax.experimental.pallas import tpu as pltpu

D_MODEL = 1024
MEM_LEN = 256
HEAD_DIM = 64
ROPE_THETA = 10000.0
NORM_EPS = 1e-6
BLOCK = 128
SWA_HEADS = 8
SWA_KV_HEADS = 2
SWA_WINDOW = 128
MLA_HEADS = 8
MLA_Q_RANK = 384
MLA_KV_RANK = 256
MLA_NOPE_DIM = 64
MLA_ROPE_DIM = 32
MLA_V_DIM = 64
A_Q = SWA_HEADS * HEAD_DIM
A_KV = SWA_KV_HEADS * HEAD_DIM
DIL_HEADS = D_MODEL // HEAD_DIM
DIL_PATTERNS = ((128, 1), (512, 4), (2048, 16))
X_HEADS = 4
X_HEAD_DIM = 128
FFN_HIDDEN = -(-8 * D_MODEL // (3 * 256)) * 256

LANES = 128
V7X_VMEM_LIMIT = 56 * 1024 * 1024
NEG = -1e30
LOG2E = 1.4426950408889634

ROW_TILE = 512
DIL_SPAN = 2048
STAGE_DIL = 4
BANDED_GROUP = 4
MLA_TQ = 512
MLA_TK = 2048
MLA_PAIRS_PER_STEP = 2
WIDE_ROW_TILE = 1024
FFN_CHUNKS = 11

F32 = jnp.float32
BF16 = jnp.bfloat16


def _cparams(sem, flags=None):
    return pltpu.CompilerParams(dimension_semantics=sem, vmem_limit_bytes=V7X_VMEM_LIMIT, flags=flags)


def _const_spec(shape):
    return pl.BlockSpec(shape, lambda *_: (0,) * len(shape))


def _rms(x, g):
    return x * lax.rsqrt(jnp.mean(x * x, axis=-1, keepdims=True) + NORM_EPS) * g


def _rope_chunk(xc, c, s, half):
    lane = lax.broadcasted_iota(jnp.int32, xc.shape, 1)
    up = pltpu.roll(xc, half, 1)
    down = pltpu.roll(xc, LANES - half, 1)
    partner = jnp.where((lane & (2 * half - 1)) < half, down, up)
    return xc * c + partner * s


def _tables_kernel(pos_ref, freq_ref, sg64_ref, sg32_ref, on32_ref, c64_ref, s64_ref, c32_ref, s32_ref):
    ang = pos_ref[...].astype(F32) * freq_ref[...]
    on = on32_ref[...] > 0.5
    for t, sign64, sign32, off32, o64_ref, o32_ref in ((jnp.cos(ang), None, None, 1.0, c64_ref, c32_ref),
                                                       (jnp.sin(ang), sg64_ref, sg32_ref, 0.0, s64_ref, s32_ref)):
        b64 = jnp.where(on, pltpu.roll(t, MLA_NOPE_DIM, 1), t)
        b32 = t
        if sign64 is not None:
            b64 = b64 * sign64[...]
            b32 = b32 * sign32[...]
        o64_ref[...] = b64
        o32_ref[...] = jnp.where(on, b32, off32)


def _rope_tables(positions):
    n = positions.size
    tm = 2048
    lane = jnp.arange(LANES)
    half64, half32 = HEAD_DIM // 2, MLA_ROPE_DIM // 2
    inv64 = ROPE_THETA ** (-jnp.arange(0, HEAD_DIM, 2, dtype=F32) / HEAD_DIM)
    inv32 = ROPE_THETA ** (-jnp.arange(0, MLA_ROPE_DIM, 2, dtype=F32) / MLA_ROPE_DIM)
    sg64 = jnp.where(lane % HEAD_DIM < half64, -1.0, 1.0).astype(F32)[None, :]
    rl = lane - MLA_NOPE_DIM
    is_rope = (rl >= 0) & (rl < MLA_ROPE_DIM)
    on32 = is_rope.astype(F32)[None, :]
    freq = jnp.where(is_rope, inv32[rl % half32], inv64[lane % half64])[None, :]
    sg32 = jnp.where(rl % MLA_ROPE_DIM < half32, -1.0, 1.0).astype(F32)[None, :]
    row = pl.BlockSpec((tm, LANES), lambda i: (i, 0))
    out = jax.ShapeDtypeStruct((n, LANES), F32)
    return pl.pallas_call(
        _tables_kernel,
        grid=(n // tm,),
        in_specs=[pl.BlockSpec((tm, 1), lambda i: (i, 0))] + [_const_spec((1, LANES))] * 4,
        out_specs=[row] * 4,
        out_shape=[out] * 4,
        compiler_params=_cparams(("arbitrary",)),
        name="rope_tables",
    )(positions.reshape(n, 1), freq, sg64, sg32, on32)


L0_QA = 0
L0_KA = A_Q
L0_VA = L0_KA + 2 * A_KV
L0_CQ = L0_VA + 2 * A_KV
L0_CKV = L0_CQ + MLA_Q_RANK
L0_KR = L0_CKV + MLA_KV_RANK
L0_COLS = L0_KR + LANES
MLA_QK = MLA_HEADS * LANES
MLA_Q_LOG2_SCALE = (MLA_NOPE_DIM + MLA_ROPE_DIM) ** -0.5 * LOG2E
MLA_V = MLA_HEADS * MLA_V_DIM


def _l0_proj_kernel(x_ref, g_ref, win_ref, gq_ref, wuq_ref, gkv_ref, wukv_ref, c64_ref, s64_ref, c32_ref, s32_ref,
                    qa_ref, ka_ref, va_ref, qb_ref, kb_ref, vb_ref):
    h = _rms(x_ref[...], g_ref[...]).astype(BF16)
    z = jnp.dot(h, win_ref[...], preferred_element_type=F32)
    c64, s64, c32, s32 = c64_ref[...], s64_ref[...], c32_ref[...], s32_ref[...]
    qscale = HEAD_DIM ** -0.5 * LOG2E
    for c in range(A_Q // LANES):
        sl = slice(c * LANES, (c + 1) * LANES)
        qa_ref[:, sl] = (_rope_chunk(z[:, sl], c64, s64, HEAD_DIM // 2) * qscale).astype(qa_ref.dtype)
    for c in range(2 * A_KV // LANES):
        sl = slice(c * LANES, (c + 1) * LANES)
        ka_ref[:, sl] = _rope_chunk(z[:, L0_KA + c * LANES:L0_KA + (c + 1) * LANES], c64, s64,
                                    HEAD_DIM // 2).astype(ka_ref.dtype)
    va_ref[...] = z[:, L0_VA:L0_CQ].astype(va_ref.dtype)

    cq = _rms(z[:, L0_CQ:L0_CKV], gq_ref[...]).astype(BF16)
    qb = jnp.dot(cq, wuq_ref[...], preferred_element_type=F32)
    ckv = _rms(z[:, L0_CKV:L0_KR], gkv_ref[...]).astype(BF16)
    kv = jnp.dot(ckv, wukv_ref[...], preferred_element_type=F32)
    kr = _rope_chunk(z[:, L0_KR:L0_COLS], c32, s32, MLA_ROPE_DIM // 2)
    for hd in range(MLA_HEADS):
        sl = slice(hd * LANES, (hd + 1) * LANES)
        qb_ref[:, sl] = (_rope_chunk(qb[:, sl], c32, s32, MLA_ROPE_DIM // 2) * MLA_Q_LOG2_SCALE).astype(qb_ref.dtype)
        kb_ref[:, sl] = (kv[:, sl] + kr).astype(kb_ref.dtype)
    vb_ref[...] = kv[:, MLA_QK:].astype(vb_ref.dtype)


def _l0_proj(x2, g, w_in, g_q, w_uq, g_kv, w_ukv, tables):
    n = x2.shape[0]
    tm = WIDE_ROW_TILE
    qa_w, ka_w, va_w, cq_w, ckv_w, kr_w = jnp.split(
        w_in, [A_Q, A_Q + A_KV, A_Q + 2 * A_KV, A_Q + 2 * A_KV + MLA_Q_RANK,
               A_Q + 2 * A_KV + MLA_Q_RANK + MLA_KV_RANK], axis=1)

    def dup(w):
        return jnp.repeat(w.reshape(D_MODEL, SWA_KV_HEADS, 1, HEAD_DIM), 2, axis=2).reshape(D_MODEL, 2 * A_KV)

    kr_pad = jnp.pad(kr_w, ((0, 0), (MLA_NOPE_DIM, LANES - MLA_NOPE_DIM - MLA_ROPE_DIM)))
    win_p = jnp.concatenate([qa_w, dup(ka_w), dup(va_w), cq_w, ckv_w, kr_pad], axis=1).astype(BF16)
    per_head_q = MLA_NOPE_DIM + MLA_ROPE_DIM
    wuq_p = jnp.pad(w_uq.reshape(MLA_Q_RANK, MLA_HEADS, per_head_q),
                    ((0, 0), (0, 0), (0, LANES - per_head_q))).reshape(MLA_Q_RANK, MLA_QK).astype(BF16)
    wukv3 = w_ukv.reshape(MLA_KV_RANK, MLA_HEADS, MLA_NOPE_DIM + MLA_V_DIM)
    wk_p = jnp.pad(wukv3[:, :, :MLA_NOPE_DIM], ((0, 0), (0, 0), (0, LANES - MLA_NOPE_DIM))).reshape(MLA_KV_RANK, MLA_QK)
    wv_p = wukv3[:, :, MLA_NOPE_DIM:].reshape(MLA_KV_RANK, MLA_V)
    wukv_p = jnp.concatenate([wk_p, wv_p], axis=1).astype(BF16)

    def row(w):
        return pl.BlockSpec((tm, w), lambda i: (i, 0))

    outs = [(A_Q, BF16), (2 * A_KV, BF16), (2 * A_KV, BF16), (MLA_QK, BF16), (MLA_QK, BF16), (MLA_V, BF16)]
    return pl.pallas_call(
        _l0_proj_kernel,
        grid=(n // tm,),
        in_specs=[row(D_MODEL), _const_spec((1, D_MODEL)), _const_spec(win_p.shape),
                  _const_spec((1, MLA_Q_RANK)), _const_spec(wuq_p.shape),
                  _const_spec((1, MLA_KV_RANK)), _const_spec(wukv_p.shape)] + [row(LANES)] * 4,
        out_specs=[row(w) for w, _ in outs],
        out_shape=[jax.ShapeDtypeStruct((n, w), dt) for w, dt in outs],
        compiler_params=_cparams(("arbitrary",)),
        name="l0_proj",
    )(x2, g[None, :], win_p, g_q[None, :], wuq_p, g_kv[None, :], wukv_p, *tables)


def _l1_proj_kernel(x_ref, g_ref, w_ref, c64_ref, s64_ref, q_ref, k_ref, v_ref):
    h = _rms(x_ref[...], g_ref[...]).astype(BF16)
    z = jnp.dot(h, w_ref[...], preferred_element_type=F32)
    c64, s64 = c64_ref[...], s64_ref[...]
    qscale = HEAD_DIM ** -0.5 * LOG2E
    for c in range(D_MODEL // LANES):
        sl = slice(c * LANES, (c + 1) * LANES)
        q_ref[:, sl] = _rope_chunk(z[:, sl], c64, s64, HEAD_DIM // 2) * qscale
        k_ref[:, sl] = _rope_chunk(z[:, D_MODEL + c * LANES:D_MODEL + (c + 1) * LANES], c64, s64, HEAD_DIM // 2)
    v_ref[...] = z[:, 2 * D_MODEL:]


def _l1_proj(x2, g, w_qkv, tables):
    n = x2.shape[0]
    tm = ROW_TILE
    row = pl.BlockSpec((tm, D_MODEL), lambda i: (i, 0))
    tab = pl.BlockSpec((tm, LANES), lambda i: (i, 0))
    out = jax.ShapeDtypeStruct((n, D_MODEL), F32)
    return pl.pallas_call(
        _l1_proj_kernel,
        grid=(n // tm,),
        in_specs=[row, _const_spec((1, D_MODEL)), _const_spec(w_qkv.shape), tab, tab],
        out_specs=[row] * 3,
        out_shape=[out] * 3,
        compiler_params=_cparams(("arbitrary",)),
        name="l1_proj",
    )(x2, g[None, :], w_qkv.astype(BF16), tables[0], tables[1])


def _banded_kernel(patterns, use_sink, span, *refs):
    refs = list(refs)
    sinks_ref = refs.pop(0) if use_sink else None
    q_ref, k_ref, v_ref, o_ref = refs[:4]
    kv_scr = refs[4:4 + 3 * len(patterns)]
    rest = refs[4 + 3 * len(patterns):]
    staged = any(dil > 1 for dil, _ in patterns)
    k4_s, v4_s = rest[:2] if staged else (None, None)
    stats = rest[2:] if staged else rest
    multi = len(patterns) > 1
    pair = pl.program_id(1)
    blk = pl.program_id(2)

    for first in (True, False):
        @pl.when(blk == 0 if first else blk > 0)
        def _(first=first):
            for pi, (dil, _) in enumerate(patterns):
                stream_len = BLOCK + span // dil
                for r in range(dil):
                    head = slice(r * stream_len, r * stream_len + BLOCK)
                    tail = slice((r + 1) * stream_len - BLOCK, (r + 1) * stream_len)
                    for ref in kv_scr[3 * pi:3 * pi + 3]:
                        ref[head, :] = jnp.zeros((BLOCK, LANES), BF16) if first else ref[tail, :]

    def put(pi, dst, count, k_rows, v_rows):
        k_s, vlo_s, vhi_s = kv_scr[3 * pi:3 * pi + 3]
        k_s[dst:dst + count, :] = k_rows.astype(BF16)
        v_rows = v_rows.astype(BF16)
        lo = lax.broadcasted_iota(jnp.int32, (count, LANES), 1) < HEAD_DIM
        vlo_s[dst:dst + count, :] = jnp.where(lo, v_rows, jnp.zeros_like(v_rows))
        vhi_s[dst:dst + count, :] = jnp.where(lo, jnp.zeros_like(v_rows), v_rows)

    quarter = span // STAGE_DIL
    if staged:
        for c in range(STAGE_DIL):
            k4_s[c * quarter:(c + 1) * quarter, :] = k_ref[pl.ds(c, quarter, stride=STAGE_DIL), :]
            v4_s[c * quarter:(c + 1) * quarter, :] = v_ref[pl.ds(c, quarter, stride=STAGE_DIL), :]
    for pi, (dil, _) in enumerate(patterns):
        cur_len = span // dil
        stream_len = BLOCK + cur_len
        if dil == 1:
            put(pi, BLOCK, span, k_ref[...], v_ref[...])
        elif dil == STAGE_DIL:
            for c in range(dil):
                src = slice(c * quarter, (c + 1) * quarter)
                put(pi, c * stream_len + BLOCK, cur_len, k4_s[src, :], v4_s[src, :])
        else:
            assert dil == STAGE_DIL * STAGE_DIL
            for r in range(dil):
                src = pl.ds((r % STAGE_DIL) * quarter + r // STAGE_DIL, cur_len, stride=STAGE_DIL)
                put(pi, r * stream_len + BLOCK, cur_len, k4_s[src, :], v4_s[src, :])

    lo_q = lax.broadcasted_iota(jnp.int32, (BLOCK, LANES), 1) < HEAD_DIM
    qi = lax.broadcasted_iota(jnp.int32, (2 * BLOCK, 2 * BLOCK), 0) & (BLOCK - 1)
    kj = lax.broadcasted_iota(jnp.int32, (2 * BLOCK, 2 * BLOCK), 1)
    dist = BLOCK + qi - kj
    in_cur = kj >= BLOCK
    if use_sink:
        top_rows = lax.broadcasted_iota(jnp.int32, (2 * BLOCK, 1), 0) < BLOCK
        sink2 = jnp.where(top_rows, sinks_ref[2 * pair], sinks_ref[2 * pair + 1]) * LOG2E

    def merge(rs):
        m_all = [stats[3 * pi + 1][rs, :] for pi in range(len(patterns))]
        m_top = functools.reduce(jnp.maximum, m_all)
        num = den = None
        for pi in range(len(patterns)):
            w = jnp.exp2(m_all[pi] - m_top)
            num = w * stats[3 * pi][rs, :] if num is None else num + w * stats[3 * pi][rs, :]
            den = w * stats[3 * pi + 2][rs, :] if den is None else den + w * stats[3 * pi + 2][rs, :]
        o_ref[rs, :] = (num / den).astype(o_ref.dtype)

    assert patterns[-1][0] == 1
    for pi, (dil, max_dist) in enumerate(patterns):
        per_stream = span // (BLOCK * dil)
        stream_len = BLOCK + span // dil
        band = (dist >= 0) & (dist <= max_dist)
        bias_full = jnp.where(band, 0.0, NEG)
        bias_first = jnp.where(blk > 0, bias_full, jnp.where(band & in_cur, 0.0, NEG))
        k_s, vlo_s, vhi_s = kv_scr[3 * pi:3 * pi + 3]

        def rows(start, count, dil=dil):
            return pl.ds(start, count) if dil == 1 else pl.ds(start, count, stride=dil)

        for gi in range(span // (BLOCK * BANDED_GROUP)):
            sub = range(BANDED_GROUP)
            idx = [gi * BANDED_GROUP + u for u in sub]
            j = [i % per_stream for i in idx]
            stream = [i // per_stream for i in idx]
            start = [r + dil * BLOCK * jj for r, jj in zip(stream, j)]
            krows = [pl.ds(r * stream_len + BLOCK * jj, 2 * BLOCK) for r, jj in zip(stream, j)]
            q = [q_ref[rows(st, BLOCK), :].astype(BF16) for st in start]
            q2 = [jnp.concatenate([jnp.where(lo_q, x, jnp.zeros_like(x)), jnp.where(lo_q, jnp.zeros_like(x), x)],
                                  axis=0) for x in q]
            s = [lax.dot_general(a, k_s[kr, :], (((1,), (1,)), ((), ())), preferred_element_type=F32)
                 for a, kr in zip(q2, krows)]
            s = [x + (bias_full if jj > 0 else bias_first) for x, jj in zip(s, j)]
            m = [jnp.max(x, axis=1, keepdims=True) for x in s]
            if use_sink:
                m = [jnp.maximum(x, sink2) for x in m]
            p = [jnp.exp2(x - mm) for x, mm in zip(s, m)]
            l = [jnp.sum(x, axis=1, keepdims=True) for x in p]
            if use_sink:
                l = [x + jnp.exp2(sink2 - mm) for x, mm in zip(l, m)]
            pb = [x.astype(BF16) for x in p]
            pcat = [jnp.concatenate([x[:BLOCK], x[BLOCK:]], axis=1) for x in pb]
            vcat = [jnp.concatenate([vlo_s[kr, :], vhi_s[kr, :]], axis=0) for kr in krows]
            acc = [jnp.dot(a, b, preferred_element_type=F32) for a, b in zip(pcat, vcat)]
            l_pair = [jnp.where(lo_q, x[:BLOCK], x[BLOCK:]) for x in l]
            for u in sub:
                dst = rows(start[u], BLOCK)
                if multi:
                    acc_s, m_s, l_s = stats[3 * pi:3 * pi + 3]
                    acc_s[dst, :] = acc[u]
                    m_s[dst, :] = jnp.where(lo_q, m[u][:BLOCK], m[u][BLOCK:])
                    l_s[dst, :] = l_pair[u]
                else:
                    o_ref[dst, :] = (acc[u] / l_pair[u]).astype(o_ref.dtype)
            if multi and pi == len(patterns) - 1:
                for u in sub:
                    merge(pl.ds(start[u], BLOCK))


def _banded_attention(q, k, v, batch, seq, n_pairs, kv_chunk_of_pair, patterns, sinks=None):
    n = q.shape[0]
    span = DIL_SPAN
    nblk = seq // span
    use_sink = sinks is not None

    def q_map(b, p, i):
        return (b * nblk + i, p)

    def kv_map(b, p, i):
        return (b * nblk + i, kv_chunk_of_pair(p))

    blk = (span, LANES)
    in_specs = [pl.BlockSpec(blk, q_map), pl.BlockSpec(blk, kv_map), pl.BlockSpec(blk, kv_map)]
    args = [q, k, v]
    if use_sink:
        in_specs = [pl.BlockSpec(memory_space=pltpu.SMEM)] + in_specs
        args = [sinks.astype(F32)] + args
    scratch = []
    for dil, _ in patterns:
        scratch += [pltpu.VMEM((dil * BLOCK + span, LANES), BF16)] * 3
    if any(dil > 1 for dil, _ in patterns):
        scratch += [pltpu.VMEM((span, LANES), k.dtype), pltpu.VMEM((span, LANES), v.dtype)]
    if len(patterns) > 1:
        scratch += [pltpu.VMEM((span, LANES), F32)] * (3 * len(patterns))
    return pl.pallas_call(
        functools.partial(_banded_kernel, patterns, use_sink, span),
        grid=(batch, n_pairs, nblk),
        in_specs=in_specs,
        out_specs=pl.BlockSpec(blk, q_map),
        out_shape=jax.ShapeDtypeStruct((n, n_pairs * LANES), BF16),
        scratch_shapes=scratch,
        compiler_params=_cparams(("arbitrary",) * 3),
        name="banded_attention_%d" % len(patterns),
    )(*args)


def _mla_kernel(tq, tk, q_ref, k_ref, v_ref, o_ref, *v_scr):
    qi = pl.program_id(2)
    n_heads = len(v_scr)
    lo_o = lax.broadcasted_iota(jnp.int32, (tq, LANES), 1) < MLA_V_DIM
    qs = [q_ref[:, e * LANES:(e + 1) * LANES] for e in range(n_heads)]

    @pl.when(qi == 0)
    def _():
        for pr in range(n_heads // 2):
            v = v_ref[:, pr * LANES:(pr + 1) * LANES]
            lo_v = lax.broadcasted_iota(jnp.int32, v.shape, 1) < MLA_V_DIM
            v_scr[2 * pr][...] = jnp.where(lo_v, v, jnp.ones_like(v))
            v_scr[2 * pr + 1][...] = jnp.where(lo_v, jnp.ones_like(v), v)

    def step(ks, width, carry, masked):
        out = []
        for e, v_s in enumerate(v_scr):
            m_old, acc = carry[2 * e:2 * e + 2]
            s = lax.dot_general(qs[e], k_ref[pl.ds(ks, width), e * LANES:(e + 1) * LANES],
                                (((1,), (1,)), ((), ())), preferred_element_type=F32)
            if masked:
                row_pos = qi * tq + lax.broadcasted_iota(jnp.int32, (tq, width), 0)
                col_pos = ks + lax.broadcasted_iota(jnp.int32, (tq, width), 1)
                s = jnp.where(row_pos >= col_pos, s, NEG)
            m_new = jnp.maximum(m_old, jnp.max(s, axis=1, keepdims=True))
            p = jnp.exp2(s - m_new).astype(BF16)
            pv = jnp.dot(p, v_s[pl.ds(ks, width), :], preferred_element_type=F32)
            out += [m_new, acc * jnp.exp2(m_old - m_new) + pv]
        return tuple(out)

    carry = (jnp.full((tq, 1), NEG, F32), jnp.zeros((tq, LANES), F32)) * n_heads
    q_start = qi * tq
    n_wide = q_start // tk
    carry = lax.fori_loop(0, n_wide, lambda kb, c: step(pl.multiple_of(kb * tk, tk), tk, c, False), carry)

    def tail(widths):
        def run(c):
            if widths:
                c = step(pl.multiple_of(n_wide * tk, tk), widths * tq, c, False)
            return step(pl.multiple_of(q_start, tq), tq, c, True)
        return run

    carry = lax.switch(qi % (tk // tq), [tail(w) for w in range(tk // tq)], carry)
    for pr in range(n_heads // 2):
        acc_lo, acc_hi = carry[4 * pr + 1], carry[4 * pr + 3]
        acc = jnp.where(lo_o, acc_lo, acc_hi)
        den = jnp.where(lo_o, pltpu.roll(acc_lo, MLA_V_DIM, 1), pltpu.roll(acc_hi, MLA_V_DIM, 1))
        o_ref[:, pr * LANES:(pr + 1) * LANES] = (acc / den).astype(o_ref.dtype)


def _mla_attention(qb, kb, vb, batch, seq):
    n = qb.shape[0]
    tq, tk = MLA_TQ, MLA_TK
    nq = seq // tq
    pairs = MLA_PAIRS_PER_STEP
    return pl.pallas_call(
        functools.partial(_mla_kernel, tq, tk),
        grid=(batch, MLA_HEADS // (2 * pairs), nq),
        in_specs=[pl.BlockSpec((tq, 2 * pairs * LANES), lambda b, p, i: (b * nq + i, p)),
                  pl.BlockSpec((seq, 2 * pairs * LANES), lambda b, p, i: (b, p)),
                  pl.BlockSpec((seq, pairs * LANES), lambda b, p, i: (b, p))],
        out_specs=pl.BlockSpec((tq, pairs * LANES), lambda b, p, i: (b * nq + i, p)),
        out_shape=jax.ShapeDtypeStruct((n, MLA_HEADS // 2 * LANES), BF16),
        scratch_shapes=[pltpu.VMEM((seq, LANES), BF16)] * (2 * pairs),
        compiler_params=_cparams(("arbitrary",) * 3),
        name="mla_attention",
    )(qb, kb, vb)


X_Q_LOG2_SCALE = X_HEAD_DIM ** -0.5 * LOG2E


def _mixout_xattn_kernel(n_in, *refs):
    x_ref = refs[0]
    a_refs = refs[1:1 + n_in]
    w_refs = refs[1 + n_in:1 + 2 * n_in]
    g_ref, wq_ref, kv_ref, wo_ref, o_ref = refs[1 + 2 * n_in:]
    x1 = x_ref[...]
    for a_ref, w_ref in zip(a_refs, w_refs):
        x1 = x1 + jnp.dot(a_ref[...], w_ref[...], preferred_element_type=F32)
    h = _rms(x1, g_ref[...]).astype(BF16)
    xq = (jnp.dot(h, wq_ref[...], preferred_element_type=F32) * X_Q_LOG2_SCALE).astype(BF16)
    heads = []
    for hd in range(X_HEADS):
        q = xq[:, hd * X_HEAD_DIM:(hd + 1) * X_HEAD_DIM]
        k = kv_ref[:, hd * X_HEAD_DIM:(hd + 1) * X_HEAD_DIM]
        v = kv_ref[:, (X_HEADS + hd) * X_HEAD_DIM:(X_HEADS + hd + 1) * X_HEAD_DIM]
        s = lax.dot_general(q, k, (((1,), (1,)), ((), ())), preferred_element_type=F32)
        p = jnp.exp2(s - jnp.max(s, axis=1, keepdims=True))
        l = jnp.sum(p, axis=1, keepdims=True)
        o = jnp.dot(p.astype(BF16), v, preferred_element_type=F32) / l
        heads.append(o.astype(BF16))
    o_all = jnp.concatenate(heads, axis=1)
    o_ref[...] = x1 + jnp.dot(o_all, wo_ref[...], preferred_element_type=F32)


def _mixout_xattn(x2, acts, weights, g_x, w_xq, kv, w_xo, seq):
    n = x2.shape[0]
    tm = WIDE_ROW_TILE
    per_batch = seq // tm
    xq_w = X_HEADS * X_HEAD_DIM

    def row(w):
        return pl.BlockSpec((tm, w), lambda i: (i, 0))

    weights = [w.astype(BF16) for w in weights]
    return pl.pallas_call(
        functools.partial(_mixout_xattn_kernel, len(acts)),
        grid=(n // tm,),
        in_specs=[row(D_MODEL)] + [row(a.shape[1]) for a in acts] + [_const_spec(w.shape) for w in weights]
        + [_const_spec((1, D_MODEL)), _const_spec((D_MODEL, xq_w)),
           pl.BlockSpec((MEM_LEN, 2 * xq_w), lambda i: (i // per_batch, 0)), _const_spec((xq_w, D_MODEL))],
        out_specs=row(D_MODEL),
        out_shape=jax.ShapeDtypeStruct((n, D_MODEL), F32),
        compiler_params=_cparams(("arbitrary",)),
        name="mixout_xattn",
    )(x2, *acts, *weights, g_x[None, :], w_xq.astype(BF16), kv, w_xo.astype(BF16))


def _memkv_kernel(mem_ref, g_ref, w_ref, kv_ref):
    h = _rms(mem_ref[...], g_ref[...]).astype(BF16)
    kv_ref[...] = jnp.dot(h, w_ref[...], preferred_element_type=F32).astype(kv_ref.dtype)


def _memkv(mem2, g, w_xkv):
    n = mem2.shape[0]
    cols = 2 * X_HEADS * X_HEAD_DIM
    return pl.pallas_call(
        _memkv_kernel,
        grid=(n // MEM_LEN,),
        in_specs=[pl.BlockSpec((MEM_LEN, D_MODEL), lambda i: (i, 0)), _const_spec((1, D_MODEL)),
                  _const_spec((D_MODEL, cols))],
        out_specs=pl.BlockSpec((MEM_LEN, cols), lambda i: (i, 0)),
        out_shape=jax.ShapeDtypeStruct((n, cols), BF16),
        compiler_params=_cparams(("arbitrary",)),
        name="memkv",
    )(mem2, g[None, :], w_xkv.astype(BF16))


def _ffn_kernel(final, x_ref, g_ref, wg_ref, wu_ref, wd_ref, *rest):
    if final:
        gf_ref, o_ref = rest
    else:
        (o_ref,) = rest
    x = x_ref[...]
    h = _rms(x, g_ref[...]).astype(BF16)
    width = FFN_HIDDEN // FFN_CHUNKS
    acc = x
    for c in range(FFN_CHUNKS):
        sl = slice(c * width, (c + 1) * width)
        gate = jnp.dot(h, wg_ref[:, sl], preferred_element_type=F32)
        up = jnp.dot(h, wu_ref[:, sl], preferred_element_type=F32)
        act = (gate * jax.nn.sigmoid(gate) * up).astype(BF16)
        acc = acc + jnp.dot(act, wd_ref[sl, :], preferred_element_type=F32)
    o_ref[...] = _rms(acc, gf_ref[...]) if final else acc


def _ffn(x2, g, w_gate, w_up, w_down, g_final=None):
    n = x2.shape[0]
    tm = WIDE_ROW_TILE
    final = g_final is not None
    row = pl.BlockSpec((tm, D_MODEL), lambda i: (i, 0))

    def weight(shape):
        return pl.BlockSpec(shape, lambda i: (0, 0), pipeline_mode=pl.Buffered(1))

    in_specs = [row, _const_spec((1, D_MODEL)), weight((D_MODEL, FFN_HIDDEN)), weight((D_MODEL, FFN_HIDDEN)),
                weight((FFN_HIDDEN, D_MODEL))]
    args = [x2, g[None, :], w_gate.astype(BF16), w_up.astype(BF16), w_down.astype(BF16)]
    if final:
        in_specs.append(_const_spec((1, D_MODEL)))
        args.append(g_final[None, :])
    return pl.pallas_call(
        functools.partial(_ffn_kernel, final),
        grid=(n // tm,),
        in_specs=in_specs,
        out_specs=row,
        out_shape=jax.ShapeDtypeStruct((n, D_MODEL), F32),
        compiler_params=_cparams(("arbitrary",)),
        name="ffn",
    )(*args)


def kernel(x, mem, positions, l0_mix_norm, l0_w_in, l0_sinks, l0_q_norm, l0_w_uq, l0_kv_norm, l0_w_ukv, l0_w_out, l0_x_norm, l0_mem_norm, l0_w_xq, l0_w_xkv, l0_w_xo, l0_ffn_norm, l0_w_gate, l0_w_up, l0_w_down, l1_mix_norm, l1_w_qkv, l1_w_out, l1_x_norm, l1_mem_norm, l1_w_xq, l1_w_xkv, l1_w_xo, l1_ffn_norm, l1_w_gate, l1_w_up, l1_w_down, final_norm):
    batch, seq, _ = x.shape
    assert seq % DIL_SPAN == 0 and seq % MLA_TK == 0 and seq % WIDE_ROW_TILE == 0
    n = batch * seq
    x2 = x.reshape(n, D_MODEL)
    mem2 = mem.reshape(batch * MEM_LEN, D_MODEL)
    tables = _rope_tables(positions)

    qa, ka, va, qb, kb, vb = _l0_proj(x2, l0_mix_norm, l0_w_in, l0_q_norm, l0_w_uq, l0_kv_norm, l0_w_ukv, tables)
    oa = _banded_attention(qa, ka, va, batch, seq, SWA_HEADS // 2, lambda p: p // 2,
                           ((1, SWA_WINDOW - 1),), sinks=l0_sinks)
    ob = _mla_attention(qb, kb, vb, batch, seq)
    x2 = _mixout_xattn(x2, [oa, ob], [l0_w_out[:A_Q], l0_w_out[A_Q:]], l0_x_norm, l0_w_xq,
                       _memkv(mem2, l0_mem_norm, l0_w_xkv), l0_w_xo, seq)
    x2 = _ffn(x2, l0_ffn_norm, l0_w_gate, l0_w_up, l0_w_down)

    q, k, v = _l1_proj(x2, l1_mix_norm, l1_w_qkv, tables)
    od = _banded_attention(q, k, v, batch, seq, DIL_HEADS // 2, lambda p: p,
                           tuple((dil, window // dil) for window, dil in reversed(DIL_PATTERNS)))
    x2 = _mixout_xattn(x2, [od], [l1_w_out], l1_x_norm, l1_w_xq,
                       _memkv(mem2, l1_mem_norm, l1_w_xkv), l1_w_xo, seq)
    x2 = _ffn(x2, l1_ffn_norm, l1_w_gate, l1_w_up, l1_w_down, g_final=final_norm)
    return x2.reshape(batch, seq, D_MODEL)
```

```python
import functools

import jax
import jax.numpy as jnp
from jax import lax
from jax.experimental import pallas as pl
from jax.experimental.pallas import tpu as pltpu

D_MODEL = 1024
MEM_LEN = 256
HEAD_DIM = 64
ROPE_THETA = 10000.0
NORM_EPS = 1e-6
BLOCK = 128
SWA_HEADS = 8
SWA_KV_HEADS = 2
SWA_WINDOW = 128
MLA_HEADS = 8
MLA_Q_RANK = 384
MLA_KV_RANK = 256
MLA_NOPE_DIM = 64
MLA_ROPE_DIM = 32
MLA_V_DIM = 64
A_Q = SWA_HEADS * HEAD_DIM
A_KV = SWA_KV_HEADS * HEAD_DIM
DIL_HEADS = D_MODEL // HEAD_DIM
DIL_PATTERNS = ((128, 1), (512, 4), (2048, 16))
X_HEADS = 4
X_HEAD_DIM = 128
FFN_HIDDEN = -(-8 * D_MODEL // (3 * 256)) * 256

LANES = 128
V7X_VMEM_LIMIT = 56 * 1024 * 1024
NEG = -1e30
LOG2E = 1.4426950408889634

ROW_TILE = 512
DIL_SPAN = 2048
STAGE_DIL = 4
BANDED_GROUP = 4
MLA_TQ = 512
MLA_TK = 2048
MLA_PAIRS_PER_STEP = 2
WIDE_ROW_TILE = 1024
FFN_CHUNKS = 11

F32 = jnp.float32
BF16 = jnp.bfloat16


def _cparams(sem, flags=None):
    return pltpu.CompilerParams(dimension_semantics=sem, vmem_limit_bytes=V7X_VMEM_LIMIT, flags=flags)


def _const_spec(shape):
    return pl.BlockSpec(shape, lambda *_: (0,) * len(shape))


def _rms(x, g):
    return x * lax.rsqrt(jnp.mean(x * x, axis=-1, keepdims=True) + NORM_EPS) * g


def _rope_chunk(xc, c, s, half):
    lane = lax.broadcasted_iota(jnp.int32, xc.shape, 1)
    up = pltpu.roll(xc, half, 1)
    down = pltpu.roll(xc, LANES - half, 1)
    partner = jnp.where((lane & (2 * half - 1)) < half, down, up)
    return xc * c + partner * s


def _tables_kernel(pos_ref, freq_ref, sg64_ref, sg32_ref, on32_ref, c64_ref, s64_ref, c32_ref, s32_ref):
    ang = pos_ref[...].astype(F32) * freq_ref[...]
    on = on32_ref[...] > 0.5
    for t, sign64, sign32, off32, o64_ref, o32_ref in ((jnp.cos(ang), None, None, 1.0, c64_ref, c32_ref),
                                                       (jnp.sin(ang), sg64_ref, sg32_ref, 0.0, s64_ref, s32_ref)):
        b64 = jnp.where(on, pltpu.roll(t, MLA_NOPE_DIM, 1), t)
        b32 = t
        if sign64 is not None:
            b64 = b64 * sign64[...]
            b32 = b32 * sign32[...]
        o64_ref[...] = b64
        o32_ref[...] = jnp.where(on, b32, off32)


def _rope_tables(positions):
    n = positions.size
    tm = 2048
    lane = jnp.arange(LANES)
    half64, half32 = HEAD_DIM // 2, MLA_ROPE_DIM // 2
    inv64 = ROPE_THETA ** (-jnp.arange(0, HEAD_DIM, 2, dtype=F32) / HEAD_DIM)
    inv32 = ROPE_THETA ** (-jnp.arange(0, MLA_ROPE_DIM, 2, dtype=F32) / MLA_ROPE_DIM)
    sg64 = jnp.where(lane % HEAD_DIM < half64, -1.0, 1.0).astype(F32)[None, :]
    rl = lane - MLA_NOPE_DIM
    is_rope = (rl >= 0) & (rl < MLA_ROPE_DIM)
    on32 = is_rope.astype(F32)[None, :]
    freq = jnp.where(is_rope, inv32[rl % half32], inv64[lane % half64])[None, :]
    sg32 = jnp.where(rl % MLA_ROPE_DIM < half32, -1.0, 1.0).astype(F32)[None, :]
    row = pl.BlockSpec((tm, LANES), lambda i: (i, 0))
    out = jax.ShapeDtypeStruct((n, LANES), F32)
    return pl.pallas_call(
        _tables_kernel,
        grid=(n // tm,),
        in_specs=[pl.BlockSpec((tm, 1), lambda i: (i, 0))] + [_const_spec((1, LANES))] * 4,
        out_specs=[row] * 4,
        out_shape=[out] * 4,
        compiler_params=_cparams(("arbitrary",)),
        name="rope_tables",
    )(positions.reshape(n, 1), freq, sg64, sg32, on32)


L0_QA = 0
L0_KA = A_Q
L0_VA = L0_KA + 2 * A_KV
L0_CQ = L0_VA + 2 * A_KV
L0_CKV = L0_CQ + MLA_Q_RANK
L0_KR = L0_CKV + MLA_KV_RANK
L0_COLS = L0_KR + LANES
MLA_QK = MLA_HEADS * LANES
MLA_Q_LOG2_SCALE = (MLA_NOPE_DIM + MLA_ROPE_DIM) ** -0.5 * LOG2E
MLA_V = MLA_HEADS * MLA_V_DIM


def _l0_proj_kernel(x_ref, g_ref, win_ref, gq_ref, wuq_ref, gkv_ref, wukv_ref, c64_ref, s64_ref, c32_ref, s32_ref,
                    qa_ref, ka_ref, va_ref, qb_ref, kb_ref, vb_ref):
    h = _rms(x_ref[...], g_ref[...]).astype(BF16)
    z = jnp.dot(h, win_ref[...], preferred_element_type=F32)
    c64, s64, c32, s32 = c64_ref[...], s64_ref[...], c32_ref[...], s32_ref[...]
    qscale = HEAD_DIM ** -0.5 * LOG2E
    for c in range(A_Q // LANES):
        sl = slice(c * LANES, (c + 1) * LANES)
        qa_ref[:, sl] = (_rope_chunk(z[:, sl], c64, s64, HEAD_DIM // 2) * qscale).astype(qa_ref.dtype)
    for c in range(2 * A_KV // LANES):
        sl = slice(c * LANES, (c + 1) * LANES)
        ka_ref[:, sl] = _rope_chunk(z[:, L0_KA + c * LANES:L0_KA + (c + 1) * LANES], c64, s64,
                                    HEAD_DIM // 2).astype(ka_ref.dtype)
    va_ref[...] = z[:, L0_VA:L0_CQ].astype(va_ref.dtype)

    cq = _rms(z[:, L0_CQ:L0_CKV], gq_ref[...]).astype(BF16)
    qb = jnp.dot(cq, wuq_ref[...], preferred_element_type=F32)
    ckv = _rms(z[:, L0_CKV:L0_KR], gkv_ref[...]).astype(BF16)
    kv = jnp.dot(ckv, wukv_ref[...], preferred_element_type=F32)
    kr = _rope_chunk(z[:, L0_KR:L0_COLS], c32, s32, MLA_ROPE_DIM // 2)
    for hd in range(MLA_HEADS):
        sl = slice(hd * LANES, (hd + 1) * LANES)
        qb_ref[:, sl] = (_rope_chunk(qb[:, sl], c32, s32, MLA_ROPE_DIM // 2) * MLA_Q_LOG2_SCALE).astype(qb_ref.dtype)
        kb_ref[:, sl] = (kv[:, sl] + kr).astype(kb_ref.dtype)
    vb_ref[...] = kv[:, MLA_QK:].astype(vb_ref.dtype)


def _l0_proj(x2, g, w_in, g_q, w_uq, g_kv, w_ukv, tables):
    n = x2.shape[0]
    tm = WIDE_ROW_TILE
    qa_w, ka_w, va_w, cq_w, ckv_w, kr_w = jnp.split(
        w_in, [A_Q, A_Q + A_KV, A_Q + 2 * A_KV, A_Q + 2 * A_KV + MLA_Q_RANK,
               A_Q + 2 * A_KV + MLA_Q_RANK + MLA_KV_RANK], axis=1)

    def dup(w):
        return jnp.repeat(w.reshape(D_MODEL, SWA_KV_HEADS, 1, HEAD_DIM), 2, axis=2).reshape(D_MODEL, 2 * A_KV)

    kr_pad = jnp.pad(kr_w, ((0, 0), (MLA_NOPE_DIM, LANES - MLA_NOPE_DIM - MLA_ROPE_DIM)))
    win_p = jnp.concatenate([qa_w, dup(ka_w), dup(va_w), cq_w, ckv_w, kr_pad], axis=1).astype(BF16)
    per_head_q = MLA_NOPE_DIM + MLA_ROPE_DIM
    wuq_p = jnp.pad(w_uq.reshape(MLA_Q_RANK, MLA_HEADS, per_head_q),
                    ((0, 0), (0, 0), (0, LANES - per_head_q))).reshape(MLA_Q_RANK, MLA_QK).astype(BF16)
    wukv3 = w_ukv.reshape(MLA_KV_RANK, MLA_HEADS, MLA_NOPE_DIM + MLA_V_DIM)
    wk_p = jnp.pad(wukv3[:, :, :MLA_NOPE_DIM], ((0, 0), (0, 0), (0, LANES - MLA_NOPE_DIM))).reshape(MLA_KV_RANK, MLA_QK)
    wv_p = wukv3[:, :, MLA_NOPE_DIM:].reshape(MLA_KV_RANK, MLA_V)
    wukv_p = jnp.concatenate([wk_p, wv_p], axis=1).astype(BF16)

    def row(w):
        return pl.BlockSpec((tm, w), lambda i: (i, 0))

    outs = [(A_Q, BF16), (2 * A_KV, BF16), (2 * A_KV, BF16), (MLA_QK, BF16), (MLA_QK, BF16), (MLA_V, BF16)]
    return pl.pallas_call(
        _l0_proj_kernel,
        grid=(n // tm,),
        in_specs=[row(D_MODEL), _const_spec((1, D_MODEL)), _const_spec(win_p.shape),
                  _const_spec((1, MLA_Q_RANK)), _const_spec(wuq_p.shape),
                  _const_spec((1, MLA_KV_RANK)), _const_spec(wukv_p.shape)] + [row(LANES)] * 4,
        out_specs=[row(w) for w, _ in outs],
        out_shape=[jax.ShapeDtypeStruct((n, w), dt) for w, dt in outs],
        compiler_params=_cparams(("arbitrary",)),
        name="l0_proj",
    )(x2, g[None, :], win_p, g_q[None, :], wuq_p, g_kv[None, :], wukv_p, *tables)


def _l1_proj_kernel(x_ref, g_ref, w_ref, c64_ref, s64_ref, q_ref, k_ref, v_ref):
    h = _rms(x_ref[...], g_ref[...]).astype(BF16)
    z = jnp.dot(h, w_ref[...], preferred_element_type=F32)
    c64, s64 = c64_ref[...], s64_ref[...]
    qscale = HEAD_DIM ** -0.5 * LOG2E
    for c in range(D_MODEL // LANES):
        sl = slice(c * LANES, (c + 1) * LANES)
        q_ref[:, sl] = _rope_chunk(z[:, sl], c64, s64, HEAD_DIM // 2) * qscale
        k_ref[:, sl] = _rope_chunk(z[:, D_MODEL + c * LANES:D_MODEL + (c + 1) * LANES], c64, s64, HEAD_DIM // 2)
    v_ref[...] = z[:, 2 * D_MODEL:]


def _l1_proj(x2, g, w_qkv, tables):
    n = x2.shape[0]
    tm = WIDE_ROW_TILE
    row = pl.BlockSpec((tm, D_MODEL), lambda i: (i, 0))
    tab = pl.BlockSpec((tm, LANES), lambda i: (i, 0))
    out = jax.ShapeDtypeStruct((n, D_MODEL), F32)
    return pl.pallas_call(
        _l1_proj_kernel,
        grid=(n // tm,),
        in_specs=[row, _const_spec((1, D_MODEL)),
                  pl.BlockSpec(w_qkv.shape, lambda i: (0, 0), pipeline_mode=pl.Buffered(1)), tab, tab],
        out_specs=[row] * 3,
        out_shape=[out] * 3,
        compiler_params=_cparams(("arbitrary",)),
        name="l1_proj",
    )(x2, g[None, :], w_qkv.astype(BF16), tables[0], tables[1])


def _banded_kernel(patterns, use_sink, span, *refs):
    refs = list(refs)
    sinks_ref = refs.pop(0) if use_sink else None
    q_ref, k_ref, v_ref, o_ref = refs[:4]
    kv_scr = refs[4:4 + 3 * len(patterns)]
    rest = refs[4 + 3 * len(patterns):]
    staged = any(dil > 1 for dil, _ in patterns)
    k4_s, v4_s = rest[:2] if staged else (None, None)
    stats = rest[2:] if staged else rest
    multi = len(patterns) > 1
    pair = pl.program_id(1)
    blk = pl.program_id(2)

    for first in (True, False):
        @pl.when(blk == 0 if first else blk > 0)
        def _(first=first):
            for pi, (dil, _) in enumerate(patterns):
                stream_len = BLOCK + span // dil
                for r in range(dil):
                    head = slice(r * stream_len, r * stream_len + BLOCK)
                    tail = slice((r + 1) * stream_len - BLOCK, (r + 1) * stream_len)
                    for ref in kv_scr[3 * pi:3 * pi + 3]:
                        ref[head, :] = jnp.zeros((BLOCK, LANES), BF16) if first else ref[tail, :]

    def put(pi, dst, count, k_rows, v_rows):
        k_s, vlo_s, vhi_s = kv_scr[3 * pi:3 * pi + 3]
        k_s[dst:dst + count, :] = k_rows.astype(BF16)
        v_rows = v_rows.astype(BF16)
        lo = lax.broadcasted_iota(jnp.int32, (count, LANES), 1) < HEAD_DIM
        vlo_s[dst:dst + count, :] = jnp.where(lo, v_rows, jnp.zeros_like(v_rows))
        vhi_s[dst:dst + count, :] = jnp.where(lo, jnp.zeros_like(v_rows), v_rows)

    quarter = span // STAGE_DIL
    if staged:
        for c in range(STAGE_DIL):
            k4_s[c * quarter:(c + 1) * quarter, :] = k_ref[pl.ds(c, quarter, stride=STAGE_DIL), :]
            v4_s[c * quarter:(c + 1) * quarter, :] = v_ref[pl.ds(c, quarter, stride=STAGE_DIL), :]
    for pi, (dil, _) in enumerate(patterns):
        cur_len = span // dil
        stream_len = BLOCK + cur_len
        if dil == 1:
            put(pi, BLOCK, span, k_ref[...], v_ref[...])
        elif dil == STAGE_DIL:
            for c in range(dil):
                src = slice(c * quarter, (c + 1) * quarter)
                put(pi, c * stream_len + BLOCK, cur_len, k4_s[src, :], v4_s[src, :])
        else:
            assert dil == STAGE_DIL * STAGE_DIL
            for r in range(dil):
                src = pl.ds((r % STAGE_DIL) * quarter + r // STAGE_DIL, cur_len, stride=STAGE_DIL)
                put(pi, r * stream_len + BLOCK, cur_len, k4_s[src, :], v4_s[src, :])

    lo_q = lax.broadcasted_iota(jnp.int32, (BLOCK, LANES), 1) < HEAD_DIM
    qi = lax.broadcasted_iota(jnp.int32, (2 * BLOCK, 2 * BLOCK), 0) & (BLOCK - 1)
    kj = lax.broadcasted_iota(jnp.int32, (2 * BLOCK, 2 * BLOCK), 1)
    dist = BLOCK + qi - kj
    in_cur = kj >= BLOCK
    if use_sink:
        top_rows = lax.broadcasted_iota(jnp.int32, (2 * BLOCK, 1), 0) < BLOCK
        sink2 = jnp.where(top_rows, sinks_ref[2 * pair], sinks_ref[2 * pair + 1]) * LOG2E

    def merge(rs):
        m_all = [stats[3 * pi + 1][rs, :] for pi in range(len(patterns))]
        m_top = functools.reduce(jnp.maximum, m_all)
        num = den = None
        for pi in range(len(patterns)):
            w = jnp.exp2(m_all[pi] - m_top)
            num = w * stats[3 * pi][rs, :] if num is None else num + w * stats[3 * pi][rs, :]
            den = w * stats[3 * pi + 2][rs, :] if den is None else den + w * stats[3 * pi + 2][rs, :]
        o_ref[rs, :] = (num / den).astype(o_ref.dtype)

    assert patterns[-1][0] == 1
    for pi, (dil, max_dist) in enumerate(patterns):
        per_stream = span // (BLOCK * dil)
        stream_len = BLOCK + span // dil
        band = (dist >= 0) & (dist <= max_dist)
        bias_full = jnp.where(band, 0.0, NEG)
        bias_first = jnp.where(blk > 0, bias_full, jnp.where(band & in_cur, 0.0, NEG))
        k_s, vlo_s, vhi_s = kv_scr[3 * pi:3 * pi + 3]

        def rows(start, count, dil=dil):
            return pl.ds(start, count) if dil == 1 else pl.ds(start, count, stride=dil)

        for gi in range(span // (BLOCK * BANDED_GROUP)):
            sub = range(BANDED_GROUP)
            idx = [gi * BANDED_GROUP + u for u in sub]
            j = [i % per_stream for i in idx]
            stream = [i // per_stream for i in idx]
            start = [r + dil * BLOCK * jj for r, jj in zip(stream, j)]
            krows = [pl.ds(r * stream_len + BLOCK * jj, 2 * BLOCK) for r, jj in zip(stream, j)]
            q = [q_ref[rows(st, BLOCK), :].astype(BF16) for st in start]
            q2 = [jnp.concatenate([jnp.where(lo_q, x, jnp.zeros_like(x)), jnp.where(lo_q, jnp.zeros_like(x), x)],
                                  axis=0) for x in q]
            s = [lax.dot_general(a, k_s[kr, :], (((1,), (1,)), ((), ())), preferred_element_type=F32)
                 for a, kr in zip(q2, krows)]
            s = [x + (bias_full if jj > 0 else bias_first) for x, jj in zip(s, j)]
            m = [jnp.max(x, axis=1, keepdims=True) for x in s]
            if use_sink:
                m = [jnp.maximum(x, sink2) for x in m]
            p = [jnp.exp2(x - mm) for x, mm in zip(s, m)]
            l = [jnp.sum(x, axis=1, keepdims=True) for x in p]
            if use_sink:
                l = [x + jnp.exp2(sink2 - mm) for x, mm in zip(l, m)]
            pb = [x.astype(BF16) for x in p]
            pcat = [jnp.concatenate([x[:BLOCK], x[BLOCK:]], axis=1) for x in pb]
            vcat = [jnp.concatenate([vlo_s[kr, :], vhi_s[kr, :]], axis=0) for kr in krows]
            acc = [jnp.dot(a, b, preferred_element_type=F32) for a, b in zip(pcat, vcat)]
            l_pair = [jnp.where(lo_q, x[:BLOCK], x[BLOCK:]) for x in l]
            for u in sub:
                dst = rows(start[u], BLOCK)
                if multi:
                    acc_s, m_s, l_s = stats[3 * pi:3 * pi + 3]
                    acc_s[dst, :] = acc[u]
                    m_s[dst, :] = jnp.where(lo_q, m[u][:BLOCK], m[u][BLOCK:])
                    l_s[dst, :] = l_pair[u]
                else:
                    o_ref[dst, :] = (acc[u] / l_pair[u]).astype(o_ref.dtype)
            if multi and pi == len(patterns) - 1:
                for u in sub:
                    merge(pl.ds(start[u], BLOCK))


def _banded_attention(q, k, v, batch, seq, n_pairs, kv_chunk_of_pair, patterns, sinks=None):
    n = q.shape[0]
    span = DIL_SPAN
    nblk = seq // span
    use_sink = sinks is not None

    def q_map(b, p, i):
        return (b * nblk + i, p)

    def kv_map(b, p, i):
        return (b * nblk + i, kv_chunk_of_pair(p))

    blk = (span, LANES)
    in_specs = [pl.BlockSpec(blk, q_map), pl.BlockSpec(blk, kv_map), pl.BlockSpec(blk, kv_map)]
    args = [q, k, v]
    if use_sink:
        in_specs = [pl.BlockSpec(memory_space=pltpu.SMEM)] + in_specs
        args = [sinks.astype(F32)] + args
    scratch = []
    for dil, _ in patterns:
        scratch += [pltpu.VMEM((dil * BLOCK + span, LANES), BF16)] * 3
    if any(dil > 1 for dil, _ in patterns):
        scratch += [pltpu.VMEM((span, LANES), k.dtype), pltpu.VMEM((span, LANES), v.dtype)]
    if len(patterns) > 1:
        scratch += [pltpu.VMEM((span, LANES), F32)] * (3 * len(patterns))
    return pl.pallas_call(
        functools.partial(_banded_kernel, patterns, use_sink, span),
        grid=(batch, n_pairs, nblk),
        in_specs=in_specs,
        out_specs=pl.BlockSpec(blk, q_map),
        out_shape=jax.ShapeDtypeStruct((n, n_pairs * LANES), BF16),
        scratch_shapes=scratch,
        compiler_params=_cparams(("arbitrary",) * 3),
        name="banded_attention_%d" % len(patterns),
    )(*args)


def _mla_kernel(tq, tk, q_ref, k_ref, v_ref, o_ref, *v_scr):
    qi = pl.program_id(2)
    n_heads = len(v_scr)
    lo_o = lax.broadcasted_iota(jnp.int32, (tq, LANES), 1) < MLA_V_DIM
    qs = [q_ref[:, e * LANES:(e + 1) * LANES] for e in range(n_heads)]

    @pl.when(qi == 0)
    def _():
        for pr in range(n_heads // 2):
            v = v_ref[:, pr * LANES:(pr + 1) * LANES]
            lo_v = lax.broadcasted_iota(jnp.int32, v.shape, 1) < MLA_V_DIM
            v_scr[2 * pr][...] = jnp.where(lo_v, v, jnp.ones_like(v))
            v_scr[2 * pr + 1][...] = jnp.where(lo_v, jnp.ones_like(v), v)

    def step(ks, width, carry, masked):
        out = []
        for e, v_s in enumerate(v_scr):
            m_old, acc = carry[2 * e:2 * e + 2]
            s = lax.dot_general(qs[e], k_ref[pl.ds(ks, width), e * LANES:(e + 1) * LANES],
                                (((1,), (1,)), ((), ())), preferred_element_type=F32)
            if masked:
                row_pos = qi * tq + lax.broadcasted_iota(jnp.int32, (tq, width), 0)
                col_pos = ks + lax.broadcasted_iota(jnp.int32, (tq, width), 1)
                s = jnp.where(row_pos >= col_pos, s, NEG)
            m_new = jnp.maximum(m_old, jnp.max(s, axis=1, keepdims=True))
            p = jnp.exp2(s - m_new).astype(BF16)
            pv = jnp.dot(p, v_s[pl.ds(ks, width), :], preferred_element_type=F32)
            out += [m_new, acc * jnp.exp2(m_old - m_new) + pv]
        return tuple(out)

    carry = (jnp.full((tq, 1), NEG, F32), jnp.zeros((tq, LANES), F32)) * n_heads
    q_start = qi * tq
    n_wide = q_start // tk
    carry = lax.fori_loop(0, n_wide, lambda kb, c: step(pl.multiple_of(kb * tk, tk), tk, c, False), carry)

    def tail(widths):
        def run(c):
            if widths:
                c = step(pl.multiple_of(n_wide * tk, tk), widths * tq, c, False)
            return step(pl.multiple_of(q_start, tq), tq, c, True)
        return run

    carry = lax.switch(qi % (tk // tq), [tail(w) for w in range(tk // tq)], carry)
    for pr in range(n_heads // 2):
        acc_lo, acc_hi = carry[4 * pr + 1], carry[4 * pr + 3]
        acc = jnp.where(lo_o, acc_lo, acc_hi)
        den = jnp.where(lo_o, pltpu.roll(acc_lo, MLA_V_DIM, 1), pltpu.roll(acc_hi, MLA_V_DIM, 1))
        o_ref[:, pr * LANES:(pr + 1) * LANES] = (acc / den).astype(o_ref.dtype)


def _mla_attention(qb, kb, vb, batch, seq):
    n = qb.shape[0]
    tq, tk = MLA_TQ, MLA_TK
    nq = seq // tq
    pairs = MLA_PAIRS_PER_STEP
    return pl.pallas_call(
        functools.partial(_mla_kernel, tq, tk),
        grid=(batch, MLA_HEADS // (2 * pairs), nq),
        in_specs=[pl.BlockSpec((tq, 2 * pairs * LANES), lambda b, p, i: (b * nq + i, p)),
                  pl.BlockSpec((seq, 2 * pairs * LANES), lambda b, p, i: (b, p)),
                  pl.BlockSpec((seq, pairs * LANES), lambda b, p, i: (b, p))],
        out_specs=pl.BlockSpec((tq, pairs * LANES), lambda b, p, i: (b * nq + i, p)),
        out_shape=jax.ShapeDtypeStruct((n, MLA_HEADS // 2 * LANES), BF16),
        scratch_shapes=[pltpu.VMEM((seq, LANES), BF16)] * (2 * pairs),
        compiler_params=_cparams(("arbitrary",) * 3),
        name="mla_attention",
    )(qb, kb, vb)


X_Q_LOG2_SCALE = X_HEAD_DIM ** -0.5 * LOG2E


def _mixout_xattn_kernel(n_in, *refs):
    x_ref = refs[0]
    a_refs = refs[1:1 + n_in]
    w_refs = refs[1 + n_in:1 + 2 * n_in]
    g_ref, wq_ref, kv_ref, wo_ref, o_ref = refs[1 + 2 * n_in:]
    x1 = x_ref[...]
    for a_ref, w_ref in zip(a_refs, w_refs):
        x1 = x1 + jnp.dot(a_ref[...], w_ref[...], preferred_element_type=F32)
    h = _rms(x1, g_ref[...]).astype(BF16)
    xq = (jnp.dot(h, wq_ref[...], preferred_element_type=F32) * X_Q_LOG2_SCALE).astype(BF16)
    heads = []
    for hd in range(X_HEADS):
        q = xq[:, hd * X_HEAD_DIM:(hd + 1) * X_HEAD_DIM]
        k = kv_ref[:, hd * X_HEAD_DIM:(hd + 1) * X_HEAD_DIM]
        v = kv_ref[:, (X_HEADS + hd) * X_HEAD_DIM:(X_HEADS + hd + 1) * X_HEAD_DIM]
        s = lax.dot_general(q, k, (((1,), (1,)), ((), ())), preferred_element_type=F32)
        p = jnp.exp2(s - jnp.max(s, axis=1, keepdims=True))
        l = jnp.sum(p, axis=1, keepdims=True)
        o = jnp.dot(p.astype(BF16), v, preferred_element_type=F32) / l
        heads.append(o.astype(BF16))
    o_all = jnp.concatenate(heads, axis=1)
    o_ref[...] = x1 + jnp.dot(o_all, wo_ref[...], preferred_element_type=F32)


def _mixout_xattn(x2, acts, weights, g_x, w_xq, kv, w_xo, seq):
    n = x2.shape[0]
    tm = WIDE_ROW_TILE
    per_batch = seq // tm
    xq_w = X_HEADS * X_HEAD_DIM

    def row(w):
        return pl.BlockSpec((tm, w), lambda i: (i, 0))

    weights = [w.astype(BF16) for w in weights]
    return pl.pallas_call(
        functools.partial(_mixout_xattn_kernel, len(acts)),
        grid=(n // tm,),
        in_specs=[row(D_MODEL)] + [row(a.shape[1]) for a in acts] + [_const_spec(w.shape) for w in weights]
        + [_const_spec((1, D_MODEL)), _const_spec((D_MODEL, xq_w)),
           pl.BlockSpec((MEM_LEN, 2 * xq_w), lambda i: (i // per_batch, 0)), _const_spec((xq_w, D_MODEL))],
        out_specs=row(D_MODEL),
        out_shape=jax.ShapeDtypeStruct((n, D_MODEL), F32),
        compiler_params=_cparams(("arbitrary",)),
        name="mixout_xattn",
    )(x2, *acts, *weights, g_x[None, :], w_xq.astype(BF16), kv, w_xo.astype(BF16))


def _memkv_kernel(mem_ref, g_ref, w_ref, kv_ref):
    h = _rms(mem_ref[...], g_ref[...]).astype(BF16)
    kv_ref[...] = jnp.dot(h, w_ref[...], preferred_element_type=F32).astype(kv_ref.dtype)


def _memkv(mem2, g, w_xkv):
    n = mem2.shape[0]
    cols = 2 * X_HEADS * X_HEAD_DIM
    return pl.pallas_call(
        _memkv_kernel,
        grid=(n // MEM_LEN,),
        in_specs=[pl.BlockSpec((MEM_LEN, D_MODEL), lambda i: (i, 0)), _const_spec((1, D_MODEL)),
                  _const_spec((D_MODEL, cols))],
        out_specs=pl.BlockSpec((MEM_LEN, cols), lambda i: (i, 0)),
        out_shape=jax.ShapeDtypeStruct((n, cols), BF16),
        compiler_params=_cparams(("arbitrary",)),
        name="memkv",
    )(mem2, g[None, :], w_xkv.astype(BF16))


def _ffn_kernel(final, x_ref, g_ref, wg_ref, wu_ref, wd_ref, *rest):
    if final:
        gf_ref, o_ref = rest
    else:
        (o_ref,) = rest
    x = x_ref[...]
    h = _rms(x, g_ref[...]).astype(BF16)
    width = FFN_HIDDEN // FFN_CHUNKS
    acc = x
    for c in range(FFN_CHUNKS):
        sl = slice(c * width, (c + 1) * width)
        gate = jnp.dot(h, wg_ref[:, sl], preferred_element_type=F32)
        up = jnp.dot(h, wu_ref[:, sl], preferred_element_type=F32)
        act = (gate * jax.nn.sigmoid(gate) * up).astype(BF16)
        acc = acc + jnp.dot(act, wd_ref[sl, :], preferred_element_type=F32)
    o_ref[...] = _rms(acc, gf_ref[...]) if final else acc


def _ffn(x2, g, w_gate, w_up, w_down, g_final=None):
    n = x2.shape[0]
    tm = WIDE_ROW_TILE
    final = g_final is not None
    row = pl.BlockSpec((tm, D_MODEL), lambda i: (i, 0))

    def weight(shape):
        return pl.BlockSpec(shape, lambda i: (0, 0), pipeline_mode=pl.Buffered(1))

    in_specs = [row, _const_spec((1, D_MODEL)), weight((D_MODEL, FFN_HIDDEN)), weight((D_MODEL, FFN_HIDDEN)),
                weight((FFN_HIDDEN, D_MODEL))]
    args = [x2, g[None, :], w_gate.astype(BF16), w_up.astype(BF16), w_down.astype(BF16)]
    if final:
        in_specs.append(_const_spec((1, D_MODEL)))
        args.append(g_final[None, :])
    return pl.pallas_call(
        functools.partial(_ffn_kernel, final),
        grid=(n // tm,),
        in_specs=in_specs,
        out_specs=row,
        out_shape=jax.ShapeDtypeStruct((n, D_MODEL), F32),
        compiler_params=_cparams(("arbitrary",)),
        name="ffn",
    )(*args)


def kernel(x, mem, positions, l0_mix_norm, l0_w_in, l0_sinks, l0_q_norm, l0_w_uq, l0_kv_norm, l0_w_ukv, l0_w_out, l0_x_norm, l0_mem_norm, l0_w_xq, l0_w_xkv, l0_w_xo, l0_ffn_norm, l0_w_gate, l0_w_up, l0_w_down, l1_mix_norm, l1_w_qkv, l1_w_out, l1_x_norm, l1_mem_norm, l1_w_xq, l1_w_xkv, l1_w_xo, l1_ffn_norm, l1_w_gate, l1_w_up, l1_w_down, final_norm):
    batch, seq, _ = x.shape
    assert seq % DIL_SPAN == 0 and seq % MLA_TK == 0 and seq % WIDE_ROW_TILE == 0
    n = batch * seq
    x2 = x.reshape(n, D_MODEL)
    mem2 = mem.reshape(batch * MEM_LEN, D_MODEL)
    tables = _rope_tables(positions)

    qa, ka, va, qb, kb, vb = _l0_proj(x2, l0_mix_norm, l0_w_in, l0_q_norm, l0_w_uq, l0_kv_norm, l0_w_ukv, tables)
    oa = _banded_attention(qa, ka, va, batch, seq, SWA_HEADS // 2, lambda p: p // 2,
                           ((1, SWA_WINDOW - 1),), sinks=l0_sinks)
    ob = _mla_attention(qb, kb, vb, batch, seq)
    x2 = _mixout_xattn(x2, [oa, ob], [l0_w_out[:A_Q], l0_w_out[A_Q:]], l0_x_norm, l0_w_xq,
                       _memkv(mem2, l0_mem_norm, l0_w_xkv), l0_w_xo, seq)
    x2 = _ffn(x2, l0_ffn_norm, l0_w_gate, l0_w_up, l0_w_down)

    q, k, v = _l1_proj(x2, l1_mix_norm, l1_w_qkv, tables)
    od = _banded_attention(q, k, v, batch, seq, DIL_HEADS // 2, lambda p: p,
                           tuple((dil, window // dil) for window, dil in reversed(DIL_PATTERNS)))
    x2 = _mixout_xattn(x2, [od], [l1_w_out], l1_x_norm, l1_w_xq,
                       _memkv(mem2, l1_mem_norm, l1_w_xkv), l1_w_xo, seq)
    x2 = _ffn(x2, l1_ffn_norm, l1_w_gate, l1_w_up, l1_w_down, g_final=final_norm)
    return x2.reshape(batch, seq, D_MODEL)
```

```python
import functools

import jax
import jax.numpy as jnp
from jax import lax
from jax.experimental import pallas as pl
from jax.experimental.pallas import tpu as pltpu

D_MODEL = 1024
MEM_LEN = 256
HEAD_DIM = 64
ROPE_THETA = 10000.0
NORM_EPS = 1e-6
BLOCK = 128
SWA_HEADS = 8
SWA_KV_HEADS = 2
SWA_WINDOW = 128
MLA_HEADS = 8
MLA_Q_RANK = 384
MLA_KV_RANK = 256
MLA_NOPE_DIM = 64
MLA_ROPE_DIM = 32
MLA_V_DIM = 64
A_Q = SWA_HEADS * HEAD_DIM
A_KV = SWA_KV_HEADS * HEAD_DIM
DIL_HEADS = D_MODEL // HEAD_DIM
DIL_PATTERNS = ((128, 1), (512, 4), (2048, 16))
X_HEADS = 4
X_HEAD_DIM = 128
FFN_HIDDEN = -(-8 * D_MODEL // (3 * 256)) * 256

LANES = 128
V7X_VMEM_LIMIT = 56 * 1024 * 1024
NEG = -1e30
LOG2E = 1.4426950408889634

DIL_SPAN = 2048
STAGE_DIL = 4
BANDED_GROUP = 4
MLA_TQ = 512
MLA_TK = 2048
MLA_PAIRS_PER_STEP = 2
WIDE_ROW_TILE = 1024
FFN_CHUNKS = 11

F32 = jnp.float32
BF16 = jnp.bfloat16


def _cparams(sem, flags=None):
    return pltpu.CompilerParams(dimension_semantics=sem, vmem_limit_bytes=V7X_VMEM_LIMIT, flags=flags)


def _const_spec(shape):
    return pl.BlockSpec(shape, lambda *_: (0,) * len(shape))


def _rms(x, g):
    return x * lax.rsqrt(jnp.mean(x * x, axis=-1, keepdims=True) + NORM_EPS) * g


def _rope_chunk(xc, c, s, half):
    lane = lax.broadcasted_iota(jnp.int32, xc.shape, 1)
    up = pltpu.roll(xc, half, 1)
    down = pltpu.roll(xc, LANES - half, 1)
    partner = jnp.where((lane & (2 * half - 1)) < half, down, up)
    return xc * c + partner * s


def _tables_kernel(pos_ref, freq_ref, sg64_ref, sg32_ref, on32_ref, c64_ref, s64_ref, c32_ref, s32_ref):
    ang = pos_ref[...].astype(F32) * freq_ref[...]
    on = on32_ref[...] > 0.5
    for t, sign64, sign32, off32, o64_ref, o32_ref in ((jnp.cos(ang), None, None, 1.0, c64_ref, c32_ref),
                                                       (jnp.sin(ang), sg64_ref, sg32_ref, 0.0, s64_ref, s32_ref)):
        b64 = jnp.where(on, pltpu.roll(t, MLA_NOPE_DIM, 1), t)
        b32 = t
        if sign64 is not None:
            b64 = b64 * sign64[...]
            b32 = b32 * sign32[...]
        o64_ref[...] = b64
        o32_ref[...] = jnp.where(on, b32, off32)


def _rope_tables(positions):
    n = positions.size
    tm = 2048
    lane = jnp.arange(LANES)
    half64, half32 = HEAD_DIM // 2, MLA_ROPE_DIM // 2
    inv64 = ROPE_THETA ** (-jnp.arange(0, HEAD_DIM, 2, dtype=F32) / HEAD_DIM)
    inv32 = ROPE_THETA ** (-jnp.arange(0, MLA_ROPE_DIM, 2, dtype=F32) / MLA_ROPE_DIM)
    sg64 = jnp.where(lane % HEAD_DIM < half64, -1.0, 1.0).astype(F32)[None, :]
    rl = lane - MLA_NOPE_DIM
    is_rope = (rl >= 0) & (rl < MLA_ROPE_DIM)
    on32 = is_rope.astype(F32)[None, :]
    freq = jnp.where(is_rope, inv32[rl % half32], inv64[lane % half64])[None, :]
    sg32 = jnp.where(rl % MLA_ROPE_DIM < half32, -1.0, 1.0).astype(F32)[None, :]
    row = pl.BlockSpec((tm, LANES), lambda i: (i, 0))
    out = jax.ShapeDtypeStruct((n, LANES), F32)
    return pl.pallas_call(
        _tables_kernel,
        grid=(n // tm,),
        in_specs=[pl.BlockSpec((tm, 1), lambda i: (i, 0))] + [_const_spec((1, LANES))] * 4,
        out_specs=[row] * 4,
        out_shape=[out] * 4,
        compiler_params=_cparams(("arbitrary",)),
        name="rope_tables",
    )(positions.reshape(n, 1), freq, sg64, sg32, on32)


L0_QA = 0
L0_KA = A_Q
L0_VA = L0_KA + 2 * A_KV
L0_CQ = L0_VA + 2 * A_KV
L0_CKV = L0_CQ + MLA_Q_RANK
L0_KR = L0_CKV + MLA_KV_RANK
L0_COLS = L0_KR + LANES
MLA_QK = MLA_HEADS * LANES
MLA_Q_LOG2_SCALE = (MLA_NOPE_DIM + MLA_ROPE_DIM) ** -0.5 * LOG2E
MLA_V = MLA_HEADS * MLA_V_DIM


def _l0_proj_kernel(x_ref, g_ref, win_ref, gq_ref, wuq_ref, gkv_ref, wukv_ref, c64_ref, s64_ref, c32_ref, s32_ref,
                    qa_ref, ka_ref, va_ref, qb_ref, kb_ref, vb_ref):
    h = _rms(x_ref[...], g_ref[...]).astype(BF16)
    z = jnp.dot(h, win_ref[...], preferred_element_type=F32)
    c64, s64, c32, s32 = c64_ref[...], s64_ref[...], c32_ref[...], s32_ref[...]
    qscale = HEAD_DIM ** -0.5 * LOG2E
    for c in range(A_Q // LANES):
        sl = slice(c * LANES, (c + 1) * LANES)
        qa_ref[:, sl] = (_rope_chunk(z[:, sl], c64, s64, HEAD_DIM // 2) * qscale).astype(qa_ref.dtype)
    for c in range(2 * A_KV // LANES):
        sl = slice(c * LANES, (c + 1) * LANES)
        ka_ref[:, sl] = _rope_chunk(z[:, L0_KA + c * LANES:L0_KA + (c + 1) * LANES], c64, s64,
                                    HEAD_DIM // 2).astype(ka_ref.dtype)
    va_ref[...] = z[:, L0_VA:L0_CQ].astype(va_ref.dtype)

    cq = _rms(z[:, L0_CQ:L0_CKV], gq_ref[...]).astype(BF16)
    qb = jnp.dot(cq, wuq_ref[...], preferred_element_type=F32)
    ckv = _rms(z[:, L0_CKV:L0_KR], gkv_ref[...]).astype(BF16)
    kv = jnp.dot(ckv, wukv_ref[...], preferred_element_type=F32)
    kr = _rope_chunk(z[:, L0_KR:L0_COLS], c32, s32, MLA_ROPE_DIM // 2)
    for hd in range(MLA_HEADS):
        sl = slice(hd * LANES, (hd + 1) * LANES)
        qb_ref[:, sl] = (_rope_chunk(qb[:, sl], c32, s32, MLA_ROPE_DIM // 2) * MLA_Q_LOG2_SCALE).astype(qb_ref.dtype)
        kb_ref[:, sl] = (kv[:, sl] + kr).astype(kb_ref.dtype)
    vb_ref[...] = kv[:, MLA_QK:].astype(vb_ref.dtype)


def _l0_proj(x2, g, w_in, g_q, w_uq, g_kv, w_ukv, tables):
    n = x2.shape[0]
    tm = WIDE_ROW_TILE
    qa_w, ka_w, va_w, cq_w, ckv_w, kr_w = jnp.split(
        w_in, [A_Q, A_Q + A_KV, A_Q + 2 * A_KV, A_Q + 2 * A_KV + MLA_Q_RANK,
               A_Q + 2 * A_KV + MLA_Q_RANK + MLA_KV_RANK], axis=1)

    def dup(w):
        return jnp.repeat(w.reshape(D_MODEL, SWA_KV_HEADS, 1, HEAD_DIM), 2, axis=2).reshape(D_MODEL, 2 * A_KV)

    kr_pad = jnp.pad(kr_w, ((0, 0), (MLA_NOPE_DIM, LANES - MLA_NOPE_DIM - MLA_ROPE_DIM)))
    win_p = jnp.concatenate([qa_w, dup(ka_w), dup(va_w), cq_w, ckv_w, kr_pad], axis=1).astype(BF16)
    per_head_q = MLA_NOPE_DIM + MLA_ROPE_DIM
    wuq_p = jnp.pad(w_uq.reshape(MLA_Q_RANK, MLA_HEADS, per_head_q),
                    ((0, 0), (0, 0), (0, LANES - per_head_q))).reshape(MLA_Q_RANK, MLA_QK).astype(BF16)
    wukv3 = w_ukv.reshape(MLA_KV_RANK, MLA_HEADS, MLA_NOPE_DIM + MLA_V_DIM)
    wk_p = jnp.pad(wukv3[:, :, :MLA_NOPE_DIM], ((0, 0), (0, 0), (0, LANES - MLA_NOPE_DIM))).reshape(MLA_KV_RANK, MLA_QK)
    wv_p = wukv3[:, :, MLA_NOPE_DIM:].reshape(MLA_KV_RANK, MLA_V)
    wukv_p = jnp.concatenate([wk_p, wv_p], axis=1).astype(BF16)

    def row(w):
        return pl.BlockSpec((tm, w), lambda i: (i, 0))

    outs = [(A_Q, BF16), (2 * A_KV, BF16), (2 * A_KV, BF16), (MLA_QK, BF16), (MLA_QK, BF16), (MLA_V, BF16)]
    return pl.pallas_call(
        _l0_proj_kernel,
        grid=(n // tm,),
        in_specs=[row(D_MODEL), _const_spec((1, D_MODEL)), _const_spec(win_p.shape),
                  _const_spec((1, MLA_Q_RANK)), _const_spec(wuq_p.shape),
                  _const_spec((1, MLA_KV_RANK)), _const_spec(wukv_p.shape)] + [row(LANES)] * 4,
        out_specs=[row(w) for w, _ in outs],
        out_shape=[jax.ShapeDtypeStruct((n, w), dt) for w, dt in outs],
        compiler_params=_cparams(("arbitrary",)),
        name="l0_proj",
    )(x2, g[None, :], win_p, g_q[None, :], wuq_p, g_kv[None, :], wukv_p, *tables)


def _l1_proj_kernel(x_ref, g_ref, w_ref, c64_ref, s64_ref, q_ref, k_ref, v_ref):
    h = _rms(x_ref[...], g_ref[...]).astype(BF16)
    z = jnp.dot(h, w_ref[...], preferred_element_type=F32)
    c64, s64 = c64_ref[...], s64_ref[...]
    qscale = HEAD_DIM ** -0.5 * LOG2E
    for c in range(D_MODEL // LANES):
        sl = slice(c * LANES, (c + 1) * LANES)
        q_ref[:, sl] = _rope_chunk(z[:, sl], c64, s64, HEAD_DIM // 2) * qscale
        k_ref[:, sl] = _rope_chunk(z[:, D_MODEL + c * LANES:D_MODEL + (c + 1) * LANES], c64, s64, HEAD_DIM // 2)
    v_ref[...] = z[:, 2 * D_MODEL:]


def _l1_proj(x2, g, w_qkv, tables):
    n = x2.shape[0]
    tm = WIDE_ROW_TILE
    row = pl.BlockSpec((tm, D_MODEL), lambda i: (i, 0))
    tab = pl.BlockSpec((tm, LANES), lambda i: (i, 0))
    out = jax.ShapeDtypeStruct((n, D_MODEL), F32)
    return pl.pallas_call(
        _l1_proj_kernel,
        grid=(n // tm,),
        in_specs=[row, _const_spec((1, D_MODEL)),
                  pl.BlockSpec(w_qkv.shape, lambda i: (0, 0), pipeline_mode=pl.Buffered(1)), tab, tab],
        out_specs=[row] * 3,
        out_shape=[out] * 3,
        compiler_params=_cparams(("arbitrary",)),
        name="l1_proj",
    )(x2, g[None, :], w_qkv.astype(BF16), tables[0], tables[1])


def _banded_kernel(patterns, use_sink, span, *refs):
    refs = list(refs)
    sinks_ref = refs.pop(0) if use_sink else None
    q_ref, k_ref, v_ref, o_ref = refs[:4]
    kv_scr = refs[4:4 + 3 * len(patterns)]
    rest = refs[4 + 3 * len(patterns):]
    staged = any(dil > 1 for dil, _ in patterns)
    k4_s, v4_s = rest[:2] if staged else (None, None)
    stats = rest[2:] if staged else rest
    multi = len(patterns) > 1
    pair = pl.program_id(1)
    blk = pl.program_id(2)

    for first in (True, False):
        @pl.when(blk == 0 if first else blk > 0)
        def _(first=first):
            for pi, (dil, _) in enumerate(patterns):
                stream_len = BLOCK + span // dil
                for r in range(dil):
                    head = slice(r * stream_len, r * stream_len + BLOCK)
                    tail = slice((r + 1) * stream_len - BLOCK, (r + 1) * stream_len)
                    for ref in kv_scr[3 * pi:3 * pi + 3]:
                        ref[head, :] = jnp.zeros((BLOCK, LANES), BF16) if first else ref[tail, :]

    def put(pi, dst, count, k_rows, v_rows):
        k_s, vlo_s, vhi_s = kv_scr[3 * pi:3 * pi + 3]
        k_s[dst:dst + count, :] = k_rows.astype(BF16)
        v_rows = v_rows.astype(BF16)
        lo = lax.broadcasted_iota(jnp.int32, (count, LANES), 1) < HEAD_DIM
        vlo_s[dst:dst + count, :] = jnp.where(lo, v_rows, jnp.zeros_like(v_rows))
        vhi_s[dst:dst + count, :] = jnp.where(lo, jnp.zeros_like(v_rows), v_rows)

    quarter = span // STAGE_DIL
    if staged:
        for c in range(STAGE_DIL):
            k4_s[c * quarter:(c + 1) * quarter, :] = k_ref[pl.ds(c, quarter, stride=STAGE_DIL), :]
            v4_s[c * quarter:(c + 1) * quarter, :] = v_ref[pl.ds(c, quarter, stride=STAGE_DIL), :]
    for pi, (dil, _) in enumerate(patterns):
        cur_len = span // dil
        stream_len = BLOCK + cur_len
        if dil == 1:
            put(pi, BLOCK, span, k_ref[...], v_ref[...])
        elif dil == STAGE_DIL:
            for c in range(dil):
                src = slice(c * quarter, (c + 1) * quarter)
                put(pi, c * stream_len + BLOCK, cur_len, k4_s[src, :], v4_s[src, :])
        else:
            assert dil == STAGE_DIL * STAGE_DIL
            for r in range(dil):
                src = pl.ds((r % STAGE_DIL) * quarter + r // STAGE_DIL, cur_len, stride=STAGE_DIL)
                put(pi, r * stream_len + BLOCK, cur_len, k4_s[src, :], v4_s[src, :])

    lo_q = lax.broadcasted_iota(jnp.int32, (BLOCK, LANES), 1) < HEAD_DIM
    qi = lax.broadcasted_iota(jnp.int32, (2 * BLOCK, 2 * BLOCK), 0) & (BLOCK - 1)
    kj = lax.broadcasted_iota(jnp.int32, (2 * BLOCK, 2 * BLOCK), 1)
    dist = BLOCK + qi - kj
    in_cur = kj >= BLOCK
    if use_sink:
        top_rows = lax.broadcasted_iota(jnp.int32, (2 * BLOCK, 1), 0) < BLOCK
        sink2 = jnp.where(top_rows, sinks_ref[2 * pair], sinks_ref[2 * pair + 1]) * LOG2E

    def merge(rs):
        m_all = [stats[3 * pi + 1][rs, :] for pi in range(len(patterns))]
        m_top = functools.reduce(jnp.maximum, m_all)
        num = den = None
        for pi in range(len(patterns)):
            w = jnp.exp2(m_all[pi] - m_top)
            num = w * stats[3 * pi][rs, :] if num is None else num + w * stats[3 * pi][rs, :]
            den = w * stats[3 * pi + 2][rs, :] if den is None else den + w * stats[3 * pi + 2][rs, :]
        o_ref[rs, :] = (num / den).astype(o_ref.dtype)

    assert patterns[-1][0] == 1
    for pi, (dil, max_dist) in enumerate(patterns):
        per_stream = span // (BLOCK * dil)
        stream_len = BLOCK + span // dil
        band = (dist >= 0) & (dist <= max_dist)
        bias_full = jnp.where(band, 0.0, NEG)
        bias_first = jnp.where(blk > 0, bias_full, jnp.where(band & in_cur, 0.0, NEG))
        k_s, vlo_s, vhi_s = kv_scr[3 * pi:3 * pi + 3]

        def rows(start, count, dil=dil):
            return pl.ds(start, count) if dil == 1 else pl.ds(start, count, stride=dil)

        for gi in range(span // (BLOCK * BANDED_GROUP)):
            sub = range(BANDED_GROUP)
            idx = [gi * BANDED_GROUP + u for u in sub]
            j = [i % per_stream for i in idx]
            stream = [i // per_stream for i in idx]
            start = [r + dil * BLOCK * jj for r, jj in zip(stream, j)]
            krows = [pl.ds(r * stream_len + BLOCK * jj, 2 * BLOCK) for r, jj in zip(stream, j)]
            q = [q_ref[rows(st, BLOCK), :].astype(BF16) for st in start]
            q2 = [jnp.concatenate([jnp.where(lo_q, x, jnp.zeros_like(x)), jnp.where(lo_q, jnp.zeros_like(x), x)],
                                  axis=0) for x in q]
            s = [lax.dot_general(a, k_s[kr, :], (((1,), (1,)), ((), ())), preferred_element_type=F32)
                 for a, kr in zip(q2, krows)]
            s = [x + (bias_full if jj > 0 else bias_first) for x, jj in zip(s, j)]
            m = [jnp.max(x, axis=1, keepdims=True) for x in s]
            if use_sink:
                m = [jnp.maximum(x, sink2) for x in m]
            p = [jnp.exp2(x - mm) for x, mm in zip(s, m)]
            l = [jnp.sum(x, axis=1, keepdims=True) for x in p]
            if use_sink:
                l = [x + jnp.exp2(sink2 - mm) for x, mm in zip(l, m)]
            pb = [x.astype(BF16) for x in p]
            pcat = [jnp.concatenate([x[:BLOCK], x[BLOCK:]], axis=1) for x in pb]
            vcat = [jnp.concatenate([vlo_s[kr, :], vhi_s[kr, :]], axis=0) for kr in krows]
            acc = [jnp.dot(a, b, preferred_element_type=F32) for a, b in zip(pcat, vcat)]
            l_pair = [jnp.where(lo_q, x[:BLOCK], x[BLOCK:]) for x in l]
            for u in sub:
                dst = rows(start[u], BLOCK)
                if multi:
                    acc_s, m_s, l_s = stats[3 * pi:3 * pi + 3]
                    acc_s[dst, :] = acc[u]
                    m_s[dst, :] = jnp.where(lo_q, m[u][:BLOCK], m[u][BLOCK:])
                    l_s[dst, :] = l_pair[u]
                else:
                    o_ref[dst, :] = (acc[u] / l_pair[u]).astype(o_ref.dtype)
            if multi and pi == len(patterns) - 1:
                for u in sub:
                    merge(pl.ds(start[u], BLOCK))


def _banded_attention(q, k, v, batch, seq, n_pairs, kv_chunk_of_pair, patterns, sinks=None):
    n = q.shape[0]
    span = DIL_SPAN
    nblk = seq // span
    use_sink = sinks is not None

    def q_map(b, p, i):
        return (b * nblk + i, p)

    def kv_map(b, p, i):
        return (b * nblk + i, kv_chunk_of_pair(p))

    blk = (span, LANES)
    in_specs = [pl.BlockSpec(blk, q_map), pl.BlockSpec(blk, kv_map), pl.BlockSpec(blk, kv_map)]
    args = [q, k, v]
    if use_sink:
        in_specs = [pl.BlockSpec(memory_space=pltpu.SMEM)] + in_specs
        args = [sinks.astype(F32)] + args
    scratch = []
    for dil, _ in patterns:
        scratch += [pltpu.VMEM((dil * BLOCK + span, LANES), BF16)] * 3
    if any(dil > 1 for dil, _ in patterns):
        scratch += [pltpu.VMEM((span, LANES), k.dtype), pltpu.VMEM((span, LANES), v.dtype)]
    if len(patterns) > 1:
        scratch += [pltpu.VMEM((span, LANES), F32)] * (3 * len(patterns))
    return pl.pallas_call(
        functools.partial(_banded_kernel, patterns, use_sink, span),
        grid=(batch, n_pairs, nblk),
        in_specs=in_specs,
        out_specs=pl.BlockSpec(blk, q_map),
        out_shape=jax.ShapeDtypeStruct((n, n_pairs * LANES), BF16),
        scratch_shapes=scratch,
        compiler_params=_cparams(("arbitrary",) * 3),
        name="banded_attention_%d" % len(patterns),
    )(*args)


def _mla_kernel(tq, tk, q_ref, k_ref, v_ref, o_ref, *v_scr):
    qi = pl.program_id(2)
    n_heads = len(v_scr)
    lo_o = lax.broadcasted_iota(jnp.int32, (tq, LANES), 1) < MLA_V_DIM
    qs = [q_ref[:, e * LANES:(e + 1) * LANES] for e in range(n_heads)]

    @pl.when(qi == 0)
    def _():
        for pr in range(n_heads // 2):
            v = v_ref[:, pr * LANES:(pr + 1) * LANES]
            lo_v = lax.broadcasted_iota(jnp.int32, v.shape, 1) < MLA_V_DIM
            v_scr[2 * pr][...] = jnp.where(lo_v, v, jnp.ones_like(v))
            v_scr[2 * pr + 1][...] = jnp.where(lo_v, jnp.ones_like(v), v)

    def step(ks, width, carry, masked):
        out = []
        for e, v_s in enumerate(v_scr):
            m_old, acc = carry[2 * e:2 * e + 2]
            s = lax.dot_general(qs[e], k_ref[pl.ds(ks, width), e * LANES:(e + 1) * LANES],
                                (((1,), (1,)), ((), ())), preferred_element_type=F32)
            if masked:
                row_pos = qi * tq + lax.broadcasted_iota(jnp.int32, (tq, width), 0)
                col_pos = ks + lax.broadcasted_iota(jnp.int32, (tq, width), 1)
                s = jnp.where(row_pos >= col_pos, s, NEG)
            m_new = jnp.maximum(m_old, jnp.max(s, axis=1, keepdims=True))
            p = jnp.exp2(s - m_new).astype(BF16)
            pv = jnp.dot(p, v_s[pl.ds(ks, width), :], preferred_element_type=F32)
            out += [m_new, acc * jnp.exp2(m_old - m_new) + pv]
        return tuple(out)

    carry = (jnp.full((tq, 1), NEG, F32), jnp.zeros((tq, LANES), F32)) * n_heads
    q_start = qi * tq
    n_wide = q_start // tk
    carry = lax.fori_loop(0, n_wide, lambda kb, c: step(pl.multiple_of(kb * tk, tk), tk, c, False), carry)

    def tail(widths):
        def run(c):
            if widths:
                c = step(pl.multiple_of(n_wide * tk, tk), widths * tq, c, False)
            return step(pl.multiple_of(q_start, tq), tq, c, True)
        return run

    carry = lax.switch(qi % (tk // tq), [tail(w) for w in range(tk // tq)], carry)
    for pr in range(n_heads // 2):
        acc_lo, acc_hi = carry[4 * pr + 1], carry[4 * pr + 3]
        acc = jnp.where(lo_o, acc_lo, acc_hi)
        den = jnp.where(lo_o, pltpu.roll(acc_lo, MLA_V_DIM, 1), pltpu.roll(acc_hi, MLA_V_DIM, 1))
        o_ref[:, pr * LANES:(pr + 1) * LANES] = (acc / den).astype(o_ref.dtype)


def _mla_attention(qb, kb, vb, batch, seq):
    n = qb.shape[0]
    tq, tk = MLA_TQ, MLA_TK
    nq = seq // tq
    pairs = MLA_PAIRS_PER_STEP
    return pl.pallas_call(
        functools.partial(_mla_kernel, tq, tk),
        grid=(batch, MLA_HEADS // (2 * pairs), nq),
        in_specs=[pl.BlockSpec((tq, 2 * pairs * LANES), lambda b, p, i: (b * nq + i, p)),
                  pl.BlockSpec((seq, 2 * pairs * LANES), lambda b, p, i: (b, p)),
                  pl.BlockSpec((seq, pairs * LANES), lambda b, p, i: (b, p))],
        out_specs=pl.BlockSpec((tq, pairs * LANES), lambda b, p, i: (b * nq + i, p)),
        out_shape=jax.ShapeDtypeStruct((n, MLA_HEADS // 2 * LANES), BF16),
        scratch_shapes=[pltpu.VMEM((seq, LANES), BF16)] * (2 * pairs),
        compiler_params=_cparams(("arbitrary",) * 3),
        name="mla_attention",
    )(qb, kb, vb)


X_Q_LOG2_SCALE = X_HEAD_DIM ** -0.5 * LOG2E


def _mixout_xattn_kernel(n_in, *refs):
    x_ref = refs[0]
    a_refs = refs[1:1 + n_in]
    w_refs = refs[1 + n_in:1 + 2 * n_in]
    g_ref, wq_ref, kv_ref, wo_ref, o_ref = refs[1 + 2 * n_in:]
    x1 = x_ref[...]
    for a_ref, w_ref in zip(a_refs, w_refs):
        x1 = x1 + jnp.dot(a_ref[...], w_ref[...], preferred_element_type=F32)
    h = _rms(x1, g_ref[...]).astype(BF16)
    xq = (jnp.dot(h, wq_ref[...], preferred_element_type=F32) * X_Q_LOG2_SCALE).astype(BF16)
    heads = []
    for hd in range(X_HEADS):
        q = xq[:, hd * X_HEAD_DIM:(hd + 1) * X_HEAD_DIM]
        k = kv_ref[:, hd * X_HEAD_DIM:(hd + 1) * X_HEAD_DIM]
        v = kv_ref[:, (X_HEADS + hd) * X_HEAD_DIM:(X_HEADS + hd + 1) * X_HEAD_DIM]
        s = lax.dot_general(q, k, (((1,), (1,)), ((), ())), preferred_element_type=F32)
        p = jnp.exp2(s - jnp.max(s, axis=1, keepdims=True))
        l = jnp.sum(p, axis=1, keepdims=True)
        o = jnp.dot(p.astype(BF16), v, preferred_element_type=F32) / l
        heads.append(o.astype(BF16))
    o_all = jnp.concatenate(heads, axis=1)
    o_ref[...] = x1 + jnp.dot(o_all, wo_ref[...], preferred_element_type=F32)


def _mixout_xattn(x2, acts, weights, g_x, w_xq, kv, w_xo, seq):
    n = x2.shape[0]
    tm = WIDE_ROW_TILE
    per_batch = seq // tm
    xq_w = X_HEADS * X_HEAD_DIM

    def row(w):
        return pl.BlockSpec((tm, w), lambda i: (i, 0))

    weights = [w.astype(BF16) for w in weights]
    return pl.pallas_call(
        functools.partial(_mixout_xattn_kernel, len(acts)),
        grid=(n // tm,),
        in_specs=[row(D_MODEL)] + [row(a.shape[1]) for a in acts] + [_const_spec(w.shape) for w in weights]
        + [_const_spec((1, D_MODEL)), _const_spec((D_MODEL, xq_w)),
           pl.BlockSpec((MEM_LEN, 2 * xq_w), lambda i: (i // per_batch, 0)), _const_spec((xq_w, D_MODEL))],
        out_specs=row(D_MODEL),
        out_shape=jax.ShapeDtypeStruct((n, D_MODEL), F32),
        compiler_params=_cparams(("arbitrary",)),
        name="mixout_xattn",
    )(x2, *acts, *weights, g_x[None, :], w_xq.astype(BF16), kv, w_xo.astype(BF16))


def _memkv_kernel(mem_ref, g_ref, w_ref, kv_ref):
    h = _rms(mem_ref[...], g_ref[...]).astype(BF16)
    kv_ref[...] = jnp.dot(h, w_ref[...], preferred_element_type=F32).astype(kv_ref.dtype)


def _memkv(mem2, g, w_xkv):
    n = mem2.shape[0]
    cols = 2 * X_HEADS * X_HEAD_DIM
    return pl.pallas_call(
        _memkv_kernel,
        grid=(n // MEM_LEN,),
        in_specs=[pl.BlockSpec((MEM_LEN, D_MODEL), lambda i: (i, 0)), _const_spec((1, D_MODEL)),
                  _const_spec((D_MODEL, cols))],
        out_specs=pl.BlockSpec((MEM_LEN, cols), lambda i: (i, 0)),
        out_shape=jax.ShapeDtypeStruct((n, cols), BF16),
        compiler_params=_cparams(("arbitrary",)),
        name="memkv",
    )(mem2, g[None, :], w_xkv.astype(BF16))


def _ffn_kernel(final, x_ref, g_ref, wg_ref, wu_ref, wd_ref, *rest):
    if final:
        gf_ref, o_ref = rest
    else:
        (o_ref,) = rest
    x = x_ref[...]
    h = _rms(x, g_ref[...]).astype(BF16)
    width = FFN_HIDDEN // FFN_CHUNKS
    acc = x
    for c in range(FFN_CHUNKS):
        sl = slice(c * width, (c + 1) * width)
        gate = jnp.dot(h, wg_ref[:, sl], preferred_element_type=F32)
        up = jnp.dot(h, wu_ref[:, sl], preferred_element_type=F32)
        act = (gate * jax.nn.sigmoid(gate) * up).astype(BF16)
        acc = acc + jnp.dot(act, wd_ref[sl, :], preferred_element_type=F32)
    o_ref[...] = _rms(acc, gf_ref[...]) if final else acc


def _ffn(x2, g, w_gate, w_up, w_down, g_final=None):
    n = x2.shape[0]
    tm = WIDE_ROW_TILE
    final = g_final is not None
    row = pl.BlockSpec((tm, D_MODEL), lambda i: (i, 0))

    def weight(shape):
        return pl.BlockSpec(shape, lambda i: (0, 0), pipeline_mode=pl.Buffered(1))

    in_specs = [row, _const_spec((1, D_MODEL)), weight((D_MODEL, FFN_HIDDEN)), weight((D_MODEL, FFN_HIDDEN)),
                weight((FFN_HIDDEN, D_MODEL))]
    args = [x2, g[None, :], w_gate.astype(BF16), w_up.astype(BF16), w_down.astype(BF16)]
    if final:
        in_specs.append(_const_spec((1, D_MODEL)))
        args.append(g_final[None, :])
    return pl.pallas_call(
        functools.partial(_ffn_kernel, final),
        grid=(n // tm,),
        in_specs=in_specs,
        out_specs=row,
        out_shape=jax.ShapeDtypeStruct((n, D_MODEL), F32),
        compiler_params=_cparams(("arbitrary",)),
        name="ffn",
    )(*args)


def kernel(x, mem, positions, l0_mix_norm, l0_w_in, l0_sinks, l0_q_norm, l0_w_uq, l0_kv_norm, l0_w_ukv, l0_w_out, l0_x_norm, l0_mem_norm, l0_w_xq, l0_w_xkv, l0_w_xo, l0_ffn_norm, l0_w_gate, l0_w_up, l0_w_down, l1_mix_norm, l1_w_qkv, l1_w_out, l1_x_norm, l1_mem_norm, l1_w_xq, l1_w_xkv, l1_w_xo, l1_ffn_norm, l1_w_gate, l1_w_up, l1_w_down, final_norm):
    batch, seq, _ = x.shape
    assert seq % DIL_SPAN == 0 and seq % MLA_TK == 0 and seq % WIDE_ROW_TILE == 0
    n = batch * seq
    x2 = x.reshape(n, D_MODEL)
    mem2 = mem.reshape(batch * MEM_LEN, D_MODEL)
    tables = _rope_tables(positions)

    qa, ka, va, qb, kb, vb = _l0_proj(x2, l0_mix_norm, l0_w_in, l0_q_norm, l0_w_uq, l0_kv_norm, l0_w_ukv, tables)
    oa = _banded_attention(qa, ka, va, batch, seq, SWA_HEADS // 2, lambda p: p // 2,
                           ((1, SWA_WINDOW - 1),), sinks=l0_sinks)
    ob = _mla_attention(qb, kb, vb, batch, seq)
    x2 = _mixout_xattn(x2, [oa, ob], [l0_w_out[:A_Q], l0_w_out[A_Q:]], l0_x_norm, l0_w_xq,
                       _memkv(mem2, l0_mem_norm, l0_w_xkv), l0_w_xo, seq)
    x2 = _ffn(x2, l0_ffn_norm, l0_w_gate, l0_w_up, l0_w_down)

    q, k, v = _l1_proj(x2, l1_mix_norm, l1_w_qkv, tables)
    od = _banded_attention(q, k, v, batch, seq, DIL_HEADS // 2, lambda p: p,
                           tuple((dil, window // dil) for window, dil in reversed(DIL_PATTERNS)))
    x2 = _mixout_xattn(x2, [od], [l1_w_out], l1_x_norm, l1_w_xq,
                       _memkv(mem2, l1_mem_norm, l1_w_xkv), l1_w_xo, seq)
    x2 = _ffn(x2, l1_ffn_norm, l1_w_gate, l1_w_up, l1_w_down, g_final=final_norm)
    return x2.reshape(batch, seq, D_MODEL)
```

```python
import functools

import jax
import jax.numpy as jnp
from jax import lax
from jax.experimental import pallas as pl
from jax.experimental.pallas import tpu as pltpu

D_MODEL = 1024
MEM_LEN = 256
HEAD_DIM = 64
ROPE_THETA = 10000.0
NORM_EPS = 1e-6
BLOCK = 128
SWA_HEADS = 8
SWA_KV_HEADS = 2
SWA_WINDOW = 128
MLA_HEADS = 8
MLA_Q_RANK = 384
MLA_KV_RANK = 256
MLA_NOPE_DIM = 64
MLA_ROPE_DIM = 32
MLA_V_DIM = 64
A_Q = SWA_HEADS * HEAD_DIM
A_KV = SWA_KV_HEADS * HEAD_DIM
DIL_HEADS = D_MODEL // HEAD_DIM
DIL_PATTERNS = ((128, 1), (512, 4), (2048, 16))
X_HEADS = 4
X_HEAD_DIM = 128
FFN_HIDDEN = -(-8 * D_MODEL // (3 * 256)) * 256

LANES = 128
V7X_VMEM_LIMIT = 56 * 1024 * 1024
NEG = -1e30
LOG2E = 1.4426950408889634

DIL_SPAN = 2048
STAGE_DIL = 4
BANDED_GROUP = 4
MLA_TQ = 512
MLA_TK = 2048
MLA_PAIRS_PER_STEP = 2
WIDE_ROW_TILE = 1024
FFN_CHUNKS = 11

F32 = jnp.float32
BF16 = jnp.bfloat16


def _cparams(sem, flags=None):
    return pltpu.CompilerParams(dimension_semantics=sem, vmem_limit_bytes=V7X_VMEM_LIMIT, flags=flags)


def _const_spec(shape):
    return pl.BlockSpec(shape, lambda *_: (0,) * len(shape))


def _rms(x, g):
    return x * lax.rsqrt(jnp.mean(x * x, axis=-1, keepdims=True) + NORM_EPS) * g


def _rope_chunk(xc, c, s, half):
    lane = lax.broadcasted_iota(jnp.int32, xc.shape, 1)
    up = pltpu.roll(xc, half, 1)
    down = pltpu.roll(xc, LANES - half, 1)
    partner = jnp.where((lane & (2 * half - 1)) < half, down, up)
    return xc * c + partner * s


def _tables_kernel(pos_ref, freq_ref, sg64_ref, sg32_ref, on32_ref, c64_ref, s64_ref, c32_ref, s32_ref):
    ang = pos_ref[...].astype(F32) * freq_ref[...]
    on = on32_ref[...] > 0.5
    for t, sign64, sign32, off32, o64_ref, o32_ref in ((jnp.cos(ang), None, None, 1.0, c64_ref, c32_ref),
                                                       (jnp.sin(ang), sg64_ref, sg32_ref, 0.0, s64_ref, s32_ref)):
        b64 = jnp.where(on, pltpu.roll(t, MLA_NOPE_DIM, 1), t)
        b32 = t
        if sign64 is not None:
            b64 = b64 * sign64[...]
            b32 = b32 * sign32[...]
        o64_ref[...] = b64
        o32_ref[...] = jnp.where(on, b32, off32)


def _rope_tables(positions):
    n = positions.size
    tm = 2048
    lane = jnp.arange(LANES)
    half64, half32 = HEAD_DIM // 2, MLA_ROPE_DIM // 2
    inv64 = ROPE_THETA ** (-jnp.arange(0, HEAD_DIM, 2, dtype=F32) / HEAD_DIM)
    inv32 = ROPE_THETA ** (-jnp.arange(0, MLA_ROPE_DIM, 2, dtype=F32) / MLA_ROPE_DIM)
    sg64 = jnp.where(lane % HEAD_DIM < half64, -1.0, 1.0).astype(F32)[None, :]
    rl = lane - MLA_NOPE_DIM
    is_rope = (rl >= 0) & (rl < MLA_ROPE_DIM)
    on32 = is_rope.astype(F32)[None, :]
    freq = jnp.where(is_rope, inv32[rl % half32], inv64[lane % half64])[None, :]
    sg32 = jnp.where(rl % MLA_ROPE_DIM < half32, -1.0, 1.0).astype(F32)[None, :]
    row = pl.BlockSpec((tm, LANES), lambda i: (i, 0))
    out = jax.ShapeDtypeStruct((n, LANES), F32)
    return pl.pallas_call(
        _tables_kernel,
        grid=(n // tm,),
        in_specs=[pl.BlockSpec((tm, 1), lambda i: (i, 0))] + [_const_spec((1, LANES))] * 4,
        out_specs=[row] * 4,
        out_shape=[out] * 4,
        compiler_params=_cparams(("arbitrary",)),
        name="rope_tables",
    )(positions.reshape(n, 1), freq, sg64, sg32, on32)


L0_QA = 0
L0_KA = A_Q
L0_VA = L0_KA + 2 * A_KV
L0_CQ = L0_VA + 2 * A_KV
L0_CKV = L0_CQ + MLA_Q_RANK
L0_KR = L0_CKV + MLA_KV_RANK
L0_COLS = L0_KR + LANES
MLA_QK = MLA_HEADS * LANES
MLA_Q_LOG2_SCALE = (MLA_NOPE_DIM + MLA_ROPE_DIM) ** -0.5 * LOG2E
MLA_V = MLA_HEADS * MLA_V_DIM


def _l0_proj_kernel(x_ref, g_ref, win_ref, gq_ref, wuq_ref, gkv_ref, wukv_ref, c64_ref, s64_ref, c32_ref, s32_ref,
                    qa_ref, ka_ref, va_ref, qb_ref, kb_ref, vb_ref):
    h = _rms(x_ref[...], g_ref[...]).astype(BF16)
    z = jnp.dot(h, win_ref[...], preferred_element_type=F32)
    c64, s64, c32, s32 = c64_ref[...], s64_ref[...], c32_ref[...], s32_ref[...]
    qscale = HEAD_DIM ** -0.5 * LOG2E
    for c in range(A_Q // LANES):
        sl = slice(c * LANES, (c + 1) * LANES)
        qa_ref[:, sl] = (_rope_chunk(z[:, sl], c64, s64, HEAD_DIM // 2) * qscale).astype(qa_ref.dtype)
    for c in range(2 * A_KV // LANES):
        sl = slice(c * LANES, (c + 1) * LANES)
        ka_ref[:, sl] = _rope_chunk(z[:, L0_KA + c * LANES:L0_KA + (c + 1) * LANES], c64, s64,
                                    HEAD_DIM // 2).astype(ka_ref.dtype)
    va_ref[...] = z[:, L0_VA:L0_CQ].astype(va_ref.dtype)

    cq = _rms(z[:, L0_CQ:L0_CKV], gq_ref[...]).astype(BF16)
    qb = jnp.dot(cq, wuq_ref[...], preferred_element_type=F32)
    ckv = _rms(z[:, L0_CKV:L0_KR], gkv_ref[...]).astype(BF16)
    kv = jnp.dot(ckv, wukv_ref[...], preferred_element_type=F32)
    kr = _rope_chunk(z[:, L0_KR:L0_COLS], c32, s32, MLA_ROPE_DIM // 2)
    for hd in range(MLA_HEADS):
        sl = slice(hd * LANES, (hd + 1) * LANES)
        qb_ref[:, sl] = (_rope_chunk(qb[:, sl], c32, s32, MLA_ROPE_DIM // 2) * MLA_Q_LOG2_SCALE).astype(qb_ref.dtype)
        kb_ref[:, sl] = (kv[:, sl] + kr).astype(kb_ref.dtype)
    vb_ref[...] = kv[:, MLA_QK:].astype(vb_ref.dtype)


def _l0_proj(x2, g, w_in, g_q, w_uq, g_kv, w_ukv, tables):
    n = x2.shape[0]
    tm = WIDE_ROW_TILE
    qa_w, ka_w, va_w, cq_w, ckv_w, kr_w = jnp.split(
        w_in, [A_Q, A_Q + A_KV, A_Q + 2 * A_KV, A_Q + 2 * A_KV + MLA_Q_RANK,
               A_Q + 2 * A_KV + MLA_Q_RANK + MLA_KV_RANK], axis=1)

    def dup(w):
        return jnp.repeat(w.reshape(D_MODEL, SWA_KV_HEADS, 1, HEAD_DIM), 2, axis=2).reshape(D_MODEL, 2 * A_KV)

    kr_pad = jnp.pad(kr_w, ((0, 0), (MLA_NOPE_DIM, LANES - MLA_NOPE_DIM - MLA_ROPE_DIM)))
    win_p = jnp.concatenate([qa_w, dup(ka_w), dup(va_w), cq_w, ckv_w, kr_pad], axis=1).astype(BF16)
    per_head_q = MLA_NOPE_DIM + MLA_ROPE_DIM
    wuq_p = jnp.pad(w_uq.reshape(MLA_Q_RANK, MLA_HEADS, per_head_q),
                    ((0, 0), (0, 0), (0, LANES - per_head_q))).reshape(MLA_Q_RANK, MLA_QK).astype(BF16)
    wukv3 = w_ukv.reshape(MLA_KV_RANK, MLA_HEADS, MLA_NOPE_DIM + MLA_V_DIM)
    wk_p = jnp.pad(wukv3[:, :, :MLA_NOPE_DIM], ((0, 0), (0, 0), (0, LANES - MLA_NOPE_DIM))).reshape(MLA_KV_RANK, MLA_QK)
    wv_p = wukv3[:, :, MLA_NOPE_DIM:].reshape(MLA_KV_RANK, MLA_V)
    wukv_p = jnp.concatenate([wk_p, wv_p], axis=1).astype(BF16)

    def row(w):
        return pl.BlockSpec((tm, w), lambda i: (i, 0))

    outs = [(A_Q, BF16), (2 * A_KV, BF16), (2 * A_KV, BF16), (MLA_QK, BF16), (MLA_QK, BF16), (MLA_V, BF16)]
    return pl.pallas_call(
        _l0_proj_kernel,
        grid=(n // tm,),
        in_specs=[row(D_MODEL), _const_spec((1, D_MODEL)), _const_spec(win_p.shape),
                  _const_spec((1, MLA_Q_RANK)), _const_spec(wuq_p.shape),
                  _const_spec((1, MLA_KV_RANK)), _const_spec(wukv_p.shape)] + [row(LANES)] * 4,
        out_specs=[row(w) for w, _ in outs],
        out_shape=[jax.ShapeDtypeStruct((n, w), dt) for w, dt in outs],
        compiler_params=_cparams(("arbitrary",)),
        name="l0_proj",
    )(x2, g[None, :], win_p, g_q[None, :], wuq_p, g_kv[None, :], wukv_p, *tables)


def _l1_proj_kernel(x_ref, g_ref, w_ref, c64_ref, s64_ref, q_ref, k_ref, v_ref):
    h = _rms(x_ref[...], g_ref[...]).astype(BF16)
    z = jnp.dot(h, w_ref[...], preferred_element_type=F32)
    c64, s64 = c64_ref[...], s64_ref[...]
    qscale = HEAD_DIM ** -0.5 * LOG2E
    for c in range(D_MODEL // LANES):
        sl = slice(c * LANES, (c + 1) * LANES)
        q_ref[:, sl] = _rope_chunk(z[:, sl], c64, s64, HEAD_DIM // 2) * qscale
        k_ref[:, sl] = _rope_chunk(z[:, D_MODEL + c * LANES:D_MODEL + (c + 1) * LANES], c64, s64, HEAD_DIM // 2)
    v_ref[...] = z[:, 2 * D_MODEL:]


def _l1_proj(x2, g, w_qkv, tables):
    n = x2.shape[0]
    tm = WIDE_ROW_TILE
    row = pl.BlockSpec((tm, D_MODEL), lambda i: (i, 0))
    tab = pl.BlockSpec((tm, LANES), lambda i: (i, 0))
    out = jax.ShapeDtypeStruct((n, D_MODEL), F32)
    return pl.pallas_call(
        _l1_proj_kernel,
        grid=(n // tm,),
        in_specs=[row, _const_spec((1, D_MODEL)),
                  pl.BlockSpec(w_qkv.shape, lambda i: (0, 0), pipeline_mode=pl.Buffered(1)), tab, tab],
        out_specs=[row] * 3,
        out_shape=[out] * 3,
        compiler_params=_cparams(("arbitrary",)),
        name="l1_proj",
    )(x2, g[None, :], w_qkv.astype(BF16), tables[0], tables[1])


def _banded_kernel(patterns, use_sink, span, *refs):
    refs = list(refs)
    sinks_ref = refs.pop(0) if use_sink else None
    q_ref, k_ref, v_ref, o_ref = refs[:4]
    kv_scr = refs[4:4 + 3 * len(patterns)]
    rest = refs[4 + 3 * len(patterns):]
    staged = any(dil > 1 for dil, _ in patterns)
    k4_s, v4_s = rest[:2] if staged else (None, None)
    stats = rest[2:] if staged else rest
    multi = len(patterns) > 1
    pair = pl.program_id(1)
    blk = pl.program_id(2)

    for first in (True, False):
        @pl.when(blk == 0 if first else blk > 0)
        def _(first=first):
            for pi, (dil, _) in enumerate(patterns):
                stream_len = BLOCK + span // dil
                for r in range(dil):
                    head = slice(r * stream_len, r * stream_len + BLOCK)
                    tail = slice((r + 1) * stream_len - BLOCK, (r + 1) * stream_len)
                    for ref in kv_scr[3 * pi:3 * pi + 3]:
                        ref[head, :] = jnp.zeros((BLOCK, LANES), BF16) if first else ref[tail, :]

    def put(pi, dst, count, k_rows, v_rows):
        k_s, vlo_s, vhi_s = kv_scr[3 * pi:3 * pi + 3]
        k_s[dst:dst + count, :] = k_rows.astype(BF16)
        v_rows = v_rows.astype(BF16)
        lo = lax.broadcasted_iota(jnp.int32, (count, LANES), 1) < HEAD_DIM
        vlo_s[dst:dst + count, :] = jnp.where(lo, v_rows, jnp.zeros_like(v_rows))
        vhi_s[dst:dst + count, :] = jnp.where(lo, jnp.zeros_like(v_rows), v_rows)

    quarter = span // STAGE_DIL
    if staged:
        for c in range(STAGE_DIL):
            k4_s[c * quarter:(c + 1) * quarter, :] = k_ref[pl.ds(c, quarter, stride=STAGE_DIL), :]
            v4_s[c * quarter:(c + 1) * quarter, :] = v_ref[pl.ds(c, quarter, stride=STAGE_DIL), :]
    for pi, (dil, _) in enumerate(patterns):
        cur_len = span // dil
        stream_len = BLOCK + cur_len
        if dil == 1:
            put(pi, BLOCK, span, k_ref[...], v_ref[...])
        elif dil == STAGE_DIL:
            for c in range(dil):
                src = slice(c * quarter, (c + 1) * quarter)
                put(pi, c * stream_len + BLOCK, cur_len, k4_s[src, :], v4_s[src, :])
        else:
            assert dil == STAGE_DIL * STAGE_DIL
            for r in range(dil):
                src = pl.ds((r % STAGE_DIL) * quarter + r // STAGE_DIL, cur_len, stride=STAGE_DIL)
                put(pi, r * stream_len + BLOCK, cur_len, k4_s[src, :], v4_s[src, :])

    lo_q = lax.broadcasted_iota(jnp.int32, (BLOCK, LANES), 1) < HEAD_DIM
    qi = lax.broadcasted_iota(jnp.int32, (2 * BLOCK, 2 * BLOCK), 0) & (BLOCK - 1)
    kj = lax.broadcasted_iota(jnp.int32, (2 * BLOCK, 2 * BLOCK), 1)
    dist = BLOCK + qi - kj
    in_cur = kj >= BLOCK
    if use_sink:
        top_rows = lax.broadcasted_iota(jnp.int32, (2 * BLOCK, 1), 0) < BLOCK
        sink2 = jnp.where(top_rows, sinks_ref[2 * pair], sinks_ref[2 * pair + 1]) * LOG2E

    def merge(rs):
        m_all = [stats[3 * pi + 1][rs, :] for pi in range(len(patterns))]
        m_top = functools.reduce(jnp.maximum, m_all)
        num = den = None
        for pi in range(len(patterns)):
            w = jnp.exp2(m_all[pi] - m_top)
            num = w * stats[3 * pi][rs, :] if num is None else num + w * stats[3 * pi][rs, :]
            den = w * stats[3 * pi + 2][rs, :] if den is None else den + w * stats[3 * pi + 2][rs, :]
        o_ref[rs, :] = (num / den).astype(o_ref.dtype)

    assert patterns[-1][0] == 1
    for pi, (dil, max_dist) in enumerate(patterns):
        per_stream = span // (BLOCK * dil)
        stream_len = BLOCK + span // dil
        band = (dist >= 0) & (dist <= max_dist)
        bias_full = jnp.where(band, 0.0, NEG)
        bias_first = jnp.where(blk > 0, bias_full, jnp.where(band & in_cur, 0.0, NEG))
        k_s, vlo_s, vhi_s = kv_scr[3 * pi:3 * pi + 3]

        def rows(start, count, dil=dil):
            return pl.ds(start, count) if dil == 1 else pl.ds(start, count, stride=dil)

        for gi in range(span // (BLOCK * BANDED_GROUP)):
            sub = range(BANDED_GROUP)
            idx = [gi * BANDED_GROUP + u for u in sub]
            j = [i % per_stream for i in idx]
            stream = [i // per_stream for i in idx]
            start = [r + dil * BLOCK * jj for r, jj in zip(stream, j)]
            krows = [pl.ds(r * stream_len + BLOCK * jj, 2 * BLOCK) for r, jj in zip(stream, j)]
            q = [q_ref[rows(st, BLOCK), :].astype(BF16) for st in start]
            q2 = [jnp.concatenate([jnp.where(lo_q, x, jnp.zeros_like(x)), jnp.where(lo_q, jnp.zeros_like(x), x)],
                                  axis=0) for x in q]
            s = [lax.dot_general(a, k_s[kr, :], (((1,), (1,)), ((), ())), preferred_element_type=F32)
                 for a, kr in zip(q2, krows)]
            s = [x + (bias_full if jj > 0 else bias_first) for x, jj in zip(s, j)]
            m = [jnp.max(x, axis=1, keepdims=True) for x in s]
            if use_sink:
                m = [jnp.maximum(x, sink2) for x in m]
            p = [jnp.exp2(x - mm) for x, mm in zip(s, m)]
            l = [jnp.sum(x, axis=1, keepdims=True) for x in p]
            if use_sink:
                l = [x + jnp.exp2(sink2 - mm) for x, mm in zip(l, m)]
            pb = [x.astype(BF16) for x in p]
            pcat = [jnp.concatenate([x[:BLOCK], x[BLOCK:]], axis=1) for x in pb]
            vcat = [jnp.concatenate([vlo_s[kr, :], vhi_s[kr, :]], axis=0) for kr in krows]
            acc = [jnp.dot(a, b, preferred_element_type=F32) for a, b in zip(pcat, vcat)]
            l_pair = [jnp.where(lo_q, x[:BLOCK], x[BLOCK:]) for x in l]
            for u in sub:
                dst = rows(start[u], BLOCK)
                if multi:
                    acc_s, m_s, l_s = stats[3 * pi:3 * pi + 3]
                    acc_s[dst, :] = acc[u]
                    m_s[dst, :] = jnp.where(lo_q, m[u][:BLOCK], m[u][BLOCK:])
                    l_s[dst, :] = l_pair[u]
                else:
                    o_ref[dst, :] = (acc[u] / l_pair[u]).astype(o_ref.dtype)
            if multi and pi == len(patterns) - 1:
                for u in sub:
                    merge(pl.ds(start[u], BLOCK))


def _banded_attention(q, k, v, batch, seq, n_pairs, kv_chunk_of_pair, patterns, sinks=None):
    n = q.shape[0]
    span = DIL_SPAN
    nblk = seq // span
    use_sink = sinks is not None

    def q_map(b, p, i):
        return (b * nblk + i, p)

    def kv_map(b, p, i):
        return (b * nblk + i, kv_chunk_of_pair(p))

    blk = (span, LANES)
    in_specs = [pl.BlockSpec(blk, q_map), pl.BlockSpec(blk, kv_map), pl.BlockSpec(blk, kv_map)]
    args = [q, k, v]
    if use_sink:
        in_specs = [pl.BlockSpec(memory_space=pltpu.SMEM)] + in_specs
        args = [sinks.astype(F32)] + args
    scratch = []
    for dil, _ in patterns:
        scratch += [pltpu.VMEM((dil * BLOCK + span, LANES), BF16)] * 3
    if any(dil > 1 for dil, _ in patterns):
        scratch += [pltpu.VMEM((span, LANES), k.dtype), pltpu.VMEM((span, LANES), v.dtype)]
    if len(patterns) > 1:
        scratch += [pltpu.VMEM((span, LANES), F32)] * (3 * len(patterns))
    return pl.pallas_call(
        functools.partial(_banded_kernel, patterns, use_sink, span),
        grid=(batch, n_pairs, nblk),
        in_specs=in_specs,
        out_specs=pl.BlockSpec(blk, q_map),
        out_shape=jax.ShapeDtypeStruct((n, n_pairs * LANES), BF16),
        scratch_shapes=scratch,
        compiler_params=_cparams(("arbitrary",) * 3),
        name="banded_attention_%d" % len(patterns),
    )(*args)


def _mla_kernel(tq, tk, q_ref, k_ref, v_ref, o_ref, *v_scr):
    qi = pl.program_id(2)
    n_heads = len(v_scr)
    lo_o = lax.broadcasted_iota(jnp.int32, (tq, LANES), 1) < MLA_V_DIM
    qs = [q_ref[:, e * LANES:(e + 1) * LANES] for e in range(n_heads)]

    @pl.when(qi == 0)
    def _():
        for pr in range(n_heads // 2):
            v = v_ref[:, pr * LANES:(pr + 1) * LANES]
            lo_v = lax.broadcasted_iota(jnp.int32, v.shape, 1) < MLA_V_DIM
            v_scr[2 * pr][...] = jnp.where(lo_v, v, jnp.ones_like(v))
            v_scr[2 * pr + 1][...] = jnp.where(lo_v, jnp.ones_like(v), v)

    def step(ks, width, carry, masked):
        out = []
        for e, v_s in enumerate(v_scr):
            m_old, acc = carry[2 * e:2 * e + 2]
            s = lax.dot_general(qs[e], k_ref[pl.ds(ks, width), e * LANES:(e + 1) * LANES],
                                (((1,), (1,)), ((), ())), preferred_element_type=F32)
            if masked:
                row_pos = qi * tq + lax.broadcasted_iota(jnp.int32, (tq, width), 0)
                col_pos = ks + lax.broadcasted_iota(jnp.int32, (tq, width), 1)
                s = jnp.where(row_pos >= col_pos, s, NEG)
            m_new = jnp.maximum(m_old, jnp.max(s, axis=1, keepdims=True))
            p = jnp.exp2(s - m_new).astype(BF16)
            pv = jnp.dot(p, v_s[pl.ds(ks, width), :], preferred_element_type=F32)
            out += [m_new, acc * jnp.exp2(m_old - m_new) + pv]
        return tuple(out)

    carry = (jnp.full((tq, 1), NEG, F32), jnp.zeros((tq, LANES), F32)) * n_heads
    q_start = qi * tq
    n_wide = q_start // tk
    carry = lax.fori_loop(0, n_wide, lambda kb, c: step(pl.multiple_of(kb * tk, tk), tk, c, False), carry)

    def tail(widths):
        def run(c):
            if widths:
                c = step(pl.multiple_of(n_wide * tk, tk), widths * tq, c, False)
            return step(pl.multiple_of(q_start, tq), tq, c, True)
        return run

    carry = lax.switch(qi % (tk // tq), [tail(w) for w in range(tk // tq)], carry)
    for pr in range(n_heads // 2):
        acc_lo, acc_hi = carry[4 * pr + 1], carry[4 * pr + 3]
        acc = jnp.where(lo_o, acc_lo, acc_hi)
        den = jnp.where(lo_o, pltpu.roll(acc_lo, MLA_V_DIM, 1), pltpu.roll(acc_hi, MLA_V_DIM, 1))
        o_ref[:, pr * LANES:(pr + 1) * LANES] = (acc / den).astype(o_ref.dtype)


def _mla_attention(qb, kb, vb, batch, seq):
    n = qb.shape[0]
    tq, tk = MLA_TQ, MLA_TK
    nq = seq // tq
    pairs = MLA_PAIRS_PER_STEP
    return pl.pallas_call(
        functools.partial(_mla_kernel, tq, tk),
        grid=(batch, MLA_HEADS // (2 * pairs), nq),
        in_specs=[pl.BlockSpec((tq, 2 * pairs * LANES), lambda b, p, i: (b * nq + i, p)),
                  pl.BlockSpec((seq, 2 * pairs * LANES), lambda b, p, i: (b, p)),
                  pl.BlockSpec((seq, pairs * LANES), lambda b, p, i: (b, p))],
        out_specs=pl.BlockSpec((tq, pairs * LANES), lambda b, p, i: (b * nq + i, p)),
        out_shape=jax.ShapeDtypeStruct((n, MLA_HEADS // 2 * LANES), BF16),
        scratch_shapes=[pltpu.VMEM((seq, LANES), BF16)] * (2 * pairs),
        compiler_params=_cparams(("arbitrary",) * 3),
        name="mla_attention",
    )(qb, kb, vb)


X_Q_LOG2_SCALE = X_HEAD_DIM ** -0.5 * LOG2E


def _mixout_xattn_kernel(n_in, *refs):
    x_ref = refs[0]
    a_refs = refs[1:1 + n_in]
    w_refs = refs[1 + n_in:1 + 2 * n_in]
    g_ref, wq_ref, kv_ref, wo_ref, o_ref = refs[1 + 2 * n_in:]
    x1 = x_ref[...]
    for a_ref, w_ref in zip(a_refs, w_refs):
        x1 = x1 + jnp.dot(a_ref[...], w_ref[...], preferred_element_type=F32)
    h = _rms(x1, g_ref[...]).astype(BF16)
    xq = (jnp.dot(h, wq_ref[...], preferred_element_type=F32) * X_Q_LOG2_SCALE).astype(BF16)
    heads = []
    for hd in range(X_HEADS):
        q = xq[:, hd * X_HEAD_DIM:(hd + 1) * X_HEAD_DIM]
        k = kv_ref[:, hd * X_HEAD_DIM:(hd + 1) * X_HEAD_DIM]
        v = kv_ref[:, (X_HEADS + hd) * X_HEAD_DIM:(X_HEADS + hd + 1) * X_HEAD_DIM]
        s = lax.dot_general(q, k, (((1,), (1,)), ((), ())), preferred_element_type=F32)
        p = jnp.exp2(s - jnp.max(s, axis=1, keepdims=True))
        l = jnp.sum(p, axis=1, keepdims=True)
        o = jnp.dot(p.astype(BF16), v, preferred_element_type=F32) / l
        heads.append(o.astype(BF16))
    o_all = jnp.concatenate(heads, axis=1)
    o_ref[...] = x1 + jnp.dot(o_all, wo_ref[...], preferred_element_type=F32)


def _mixout_xattn(x2, acts, weights, g_x, w_xq, kv, w_xo, seq):
    n = x2.shape[0]
    tm = WIDE_ROW_TILE
    per_batch = seq // tm
    xq_w = X_HEADS * X_HEAD_DIM

    def row(w):
        return pl.BlockSpec((tm, w), lambda i: (i, 0))

    weights = [w.astype(BF16) for w in weights]
    return pl.pallas_call(
        functools.partial(_mixout_xattn_kernel, len(acts)),
        grid=(n // tm,),
        in_specs=[row(D_MODEL)] + [row(a.shape[1]) for a in acts] + [_const_spec(w.shape) for w in weights]
        + [_const_spec((1, D_MODEL)), _const_spec((D_MODEL, xq_w)),
           pl.BlockSpec((MEM_LEN, 2 * xq_w), lambda i: (i // per_batch, 0)), _const_spec((xq_w, D_MODEL))],
        out_specs=row(D_MODEL),
        out_shape=jax.ShapeDtypeStruct((n, D_MODEL), F32),
        compiler_params=_cparams(("arbitrary",)),
        name="mixout_xattn",
    )(x2, *acts, *weights, g_x[None, :], w_xq.astype(BF16), kv, w_xo.astype(BF16))


def _memkv_kernel(n_layers, mem_ref, *refs):
    x = mem_ref[...]
    xn = x * lax.rsqrt(jnp.mean(x * x, axis=-1, keepdims=True) + NORM_EPS)
    for layer in range(n_layers):
        g_ref, w_ref = refs[2 * layer:2 * layer + 2]
        kv_ref = refs[2 * n_layers + layer]
        h = (xn * g_ref[...]).astype(BF16)
        kv_ref[...] = jnp.dot(h, w_ref[...], preferred_element_type=F32).astype(kv_ref.dtype)


def _memkv(mem2, gains, weights):
    n = mem2.shape[0]
    cols = 2 * X_HEADS * X_HEAD_DIM
    layers = len(gains)
    args = []
    for g, w in zip(gains, weights):
        args += [g[None, :], w.astype(BF16)]
    out_spec = pl.BlockSpec((MEM_LEN, cols), lambda i: (i, 0))
    return pl.pallas_call(
        functools.partial(_memkv_kernel, layers),
        grid=(n // MEM_LEN,),
        in_specs=[pl.BlockSpec((MEM_LEN, D_MODEL), lambda i: (i, 0))]
        + [_const_spec((1, D_MODEL)), _const_spec((D_MODEL, cols))] * layers,
        out_specs=[out_spec] * layers,
        out_shape=[jax.ShapeDtypeStruct((n, cols), BF16)] * layers,
        compiler_params=_cparams(("arbitrary",)),
        name="memkv",
    )(mem2, *args)


def _ffn_kernel(final, x_ref, g_ref, wg_ref, wu_ref, wd_ref, *rest):
    if final:
        gf_ref, o_ref = rest
    else:
        (o_ref,) = rest
    x = x_ref[...]
    h = _rms(x, g_ref[...]).astype(BF16)
    width = FFN_HIDDEN // FFN_CHUNKS
    acc = x
    for c in range(FFN_CHUNKS):
        sl = slice(c * width, (c + 1) * width)
        gate = jnp.dot(h, wg_ref[:, sl], preferred_element_type=F32)
        up = jnp.dot(h, wu_ref[:, sl], preferred_element_type=F32)
        act = (gate * jax.nn.sigmoid(gate) * up).astype(BF16)
        acc = acc + jnp.dot(act, wd_ref[sl, :], preferred_element_type=F32)
    o_ref[...] = _rms(acc, gf_ref[...]) if final else acc


def _ffn(x2, g, w_gate, w_up, w_down, g_final=None):
    n = x2.shape[0]
    tm = WIDE_ROW_TILE
    final = g_final is not None
    row = pl.BlockSpec((tm, D_MODEL), lambda i: (i, 0))

    def weight(shape):
        return pl.BlockSpec(shape, lambda i: (0, 0), pipeline_mode=pl.Buffered(1))

    in_specs = [row, _const_spec((1, D_MODEL)), weight((D_MODEL, FFN_HIDDEN)), weight((D_MODEL, FFN_HIDDEN)),
                weight((FFN_HIDDEN, D_MODEL))]
    args = [x2, g[None, :], w_gate.astype(BF16), w_up.astype(BF16), w_down.astype(BF16)]
    if final:
        in_specs.append(_const_spec((1, D_MODEL)))
        args.append(g_final[None, :])
    return pl.pallas_call(
        functools.partial(_ffn_kernel, final),
        grid=(n // tm,),
        in_specs=in_specs,
        out_specs=row,
        out_shape=jax.ShapeDtypeStruct((n, D_MODEL), F32),
        compiler_params=_cparams(("arbitrary",)),
        name="ffn",
    )(*args)


def kernel(x, mem, positions, l0_mix_norm, l0_w_in, l0_sinks, l0_q_norm, l0_w_uq, l0_kv_norm, l0_w_ukv, l0_w_out, l0_x_norm, l0_mem_norm, l0_w_xq, l0_w_xkv, l0_w_xo, l0_ffn_norm, l0_w_gate, l0_w_up, l0_w_down, l1_mix_norm, l1_w_qkv, l1_w_out, l1_x_norm, l1_mem_norm, l1_w_xq, l1_w_xkv, l1_w_xo, l1_ffn_norm, l1_w_gate, l1_w_up, l1_w_down, final_norm):
    batch, seq, _ = x.shape
    assert seq % DIL_SPAN == 0 and seq % MLA_TK == 0 and seq % WIDE_ROW_TILE == 0
    n = batch * seq
    x2 = x.reshape(n, D_MODEL)
    mem2 = mem.reshape(batch * MEM_LEN, D_MODEL)
    tables = _rope_tables(positions)
    kv0, kv1 = _memkv(mem2, [l0_mem_norm, l1_mem_norm], [l0_w_xkv, l1_w_xkv])

    qa, ka, va, qb, kb, vb = _l0_proj(x2, l0_mix_norm, l0_w_in, l0_q_norm, l0_w_uq, l0_kv_norm, l0_w_ukv, tables)
    oa = _banded_attention(qa, ka, va, batch, seq, SWA_HEADS // 2, lambda p: p // 2,
                           ((1, SWA_WINDOW - 1),), sinks=l0_sinks)
    ob = _mla_attention(qb, kb, vb, batch, seq)
    x2 = _mixout_xattn(x2, [oa, ob], [l0_w_out[:A_Q], l0_w_out[A_Q:]], l0_x_norm, l0_w_xq,
                       kv0, l0_w_xo, seq)
    x2 = _ffn(x2, l0_ffn_norm, l0_w_gate, l0_w_up, l0_w_down)

    q, k, v = _l1_proj(x2, l1_mix_norm, l1_w_qkv, tables)
    od = _banded_attention(q, k, v, batch, seq, DIL_HEADS // 2, lambda p: p,
                           tuple((dil, window // dil) for window, dil in reversed(DIL_PATTERNS)))
    x2 = _mixout_xattn(x2, [od], [l1_w_out], l1_x_norm, l1_w_xq,
                       kv1, l1_w_xo, seq)
    x2 = _ffn(x2, l1_ffn_norm, l1_w_gate, l1_w_up, l1_w_down, g_final=final_norm)
    return x2.reshape(batch, seq, D_MODEL)
```
